```python
import jax, jax.numpy as jnp
from jax import lax
import numpy as np

D_MODEL = 1024
BATCH = 8
SEQ = 4096
DEPTH = 2

CTX_LEN = 256
GRID_W = 64
N_MIXERS = 2
N_POOL_LAYERS = (DEPTH + 1) // 2
N_LRU_LAYERS = DEPTH // 2
POOL_WINDOWS = (2, 4, 8, 16)
N_POOL_GROUPS = 4
POOL_GROUP = D_MODEL // N_POOL_GROUPS
D_RNN = D_MODEL
N_LRU_BLOCKS = 4
LRU_BLOCK = D_RNN // N_LRU_BLOCKS
CONV_WIDTH = 4
CONV_LEFT = 2
LRU_C = 8.0
N_EXPERTS = 32
TOP_K = 4
D_FF = D_MODEL
SWIGLU_LIMIT = 7.0
SWIGLU_ALPHA = 1.702
EXPERT_BLOCK = 128
NORM_EPS = 1e-6

kernel_name = 'hybrid_pool_rglru_moe_prefix_dit'


def rms_norm(x, g):
    xf = x.astype(jnp.float32)
    y = xf * lax.rsqrt(jnp.mean(xf * xf, axis=-1, keepdims=True) + NORM_EPS)
    return (y * g.astype(jnp.float32)).astype(x.dtype)


def modulate(x, shift, scale):
    return x * (1.0 + scale) + shift


def window_bounds(n, w):
    idx = jnp.arange(n)
    return jnp.clip(idx - w // 2, 0, n), jnp.clip(idx + w // 2, 0, n)


def pool_grid(h, w):
    B, T, C = h.shape
    rows = T // GRID_W
    g = h.astype(jnp.float32).reshape(B, rows, GRID_W, C)
    sat = jnp.pad(jnp.cumsum(jnp.cumsum(g, axis=1), axis=2), ((0, 0), (1, 0), (1, 0), (0, 0)))
    r0, r1 = window_bounds(rows, w)
    c0, c1 = window_bounds(GRID_W, w)

    def corner(ri, ci):
        return jnp.take(jnp.take(sat, ri, axis=1), ci, axis=2)

    total = corner(r1, c1) - corner(r0, c1) - corner(r1, c0) + corner(r0, c0)
    count = ((r1 - r0)[:, None] * (c1 - c0)[None, :]).astype(jnp.float32)
    return (total / count[None, :, :, None]).reshape(B, T, C).astype(h.dtype)


def pool_seq(h, w):
    T = h.shape[1]
    cs = jnp.pad(jnp.cumsum(h.astype(jnp.float32), axis=1), ((0, 0), (1, 0), (0, 0)))
    lo, hi = window_bounds(T, w)
    total = jnp.take(cs, hi, axis=1) - jnp.take(cs, lo, axis=1)
    count = (hi - lo).astype(jnp.float32)
    return (total / count[None, :, None]).astype(h.dtype)


def pool_mixer(h, w_grp, scale, pool_fn):
    B, T, D = h.shape
    diffs = []
    for gi, w in enumerate(POOL_WINDOWS):
        hg = h[..., gi * POOL_GROUP:(gi + 1) * POOL_GROUP]
        diffs.append(pool_fn(hg, w) - hg)
    d = jnp.stack(diffs, axis=2)
    y = jnp.einsum('btgc,gcd->btgd', d, w_grp).reshape(B, T, D)
    return y * scale


def depthwise_conv(x, w, b):
    T = x.shape[1]
    xp = jnp.pad(x, ((0, 0), (CONV_LEFT, CONV_WIDTH - 1 - CONV_LEFT), (0, 0)))
    y = xp[:, 0:T] * w[0]
    for k in range(1, CONV_WIDTH):
        y = y + xp[:, k:k + T] * w[k]
    return y + b


def rglru_coeffs(u, w_r, b_r, w_i, b_i, lam):
    B, T, _ = u.shape
    ub = u.reshape(B, T, N_LRU_BLOCKS, LRU_BLOCK)
    r = jax.nn.sigmoid(jnp.einsum('btnd,nde->btne', ub, w_r).reshape(B, T, D_RNN) + b_r)
    i = jax.nn.sigmoid(jnp.einsum('btnd,nde->btne', ub, w_i).reshape(B, T, D_RNN) + b_i)
    log_a = -LRU_C * r.astype(jnp.float32) * jax.nn.softplus(-lam.astype(jnp.float32))
    a = jnp.exp(log_a)
    mult = jnp.sqrt(-jnp.expm1(2.0 * log_a))
    return a, mult * (i * u).astype(jnp.float32)


def linear_scan(a, b, h0, reverse):
    if h0 is not None:
        edge = -1 if reverse else 0
        b = b.at[:, edge].add(a[:, edge] * h0)

    def combine(p, q):
        a1, b1 = p
        a2, b2 = q
        return a1 * a2, a2 * b1 + b2

    _, h = lax.associative_scan(combine, (a, b), axis=1, reverse=reverse)
    return h


def lru_mixer(hl, hc, w_in, conv_w, conv_b, w_r, b_r, w_i, b_i, lam, w_out, need_ctx_out):
    gate_l, u_l = jnp.split(hl @ w_in, 2, axis=-1)
    u_l = depthwise_conv(u_l, conv_w, conv_b)
    u_c = depthwise_conv(hc @ w_in[:, D_RNN:], conv_w, conv_b)
    outs_l, outs_c = [], []
    for d, rev in enumerate((False, True)):
        a_c, b_c = rglru_coeffs(u_c, w_r[d], b_r[d], w_i[d], b_i[d], lam[d])
        h_c = linear_scan(a_c, b_c, None, rev)
        h0 = h_c[:, 0] if rev else h_c[:, -1]
        a_l, b_l = rglru_coeffs(u_l, w_r[d], b_r[d], w_i[d], b_i[d], lam[d])
        outs_l.append(linear_scan(a_l, b_l, h0, rev))
        outs_c.append(h_c)
    rec_l = (outs_l[0] + outs_l[1]).astype(hl.dtype)
    y_l = (jax.nn.gelu(gate_l) * rec_l) @ w_out
    if not need_ctx_out:
        return y_l, None
    rec_c = (outs_c[0] + outs_c[1]).astype(hc.dtype)
    y_c = (jax.nn.gelu(hc @ w_in[:, :D_RNN]) * rec_c) @ w_out
    return y_l, y_c


def moe(h, w_router, b_router, w_gu, b_gu, w_down, b_down):
    N, D = h.shape
    logits = (h @ w_router + b_router).astype(jnp.float32)
    top_val, top_idx = lax.top_k(logits, TOP_K)
    gate = jax.nn.softmax(top_val, axis=-1).astype(h.dtype)
    NK = N * TOP_K
    flat_e = top_idx.reshape(NK)
    flat_tok = jnp.arange(NK, dtype=jnp.int32) // TOP_K
    flat_w = gate.reshape(NK)
    order = jnp.argsort(flat_e)
    e_sorted, tok_sorted, w_sorted = flat_e[order], flat_tok[order], flat_w[order]
    counts = jnp.bincount(flat_e, length=N_EXPERTS)
    padded = (counts + EXPERT_BLOCK - 1) // EXPERT_BLOCK * EXPERT_BLOCK
    pad_end = jnp.cumsum(padded)
    pad_start = pad_end - padded
    start = jnp.cumsum(counts) - counts
    dest = pad_start[e_sorted] + jnp.arange(NK, dtype=jnp.int32) - start[e_sorted]
    n_blocks = (NK + N_EXPERTS * (EXPERT_BLOCK - 1) + EXPERT_BLOCK - 1) // EXPERT_BLOCK
    n_rows = n_blocks * EXPERT_BLOCK
    row_tok = jnp.full((n_rows,), N, dtype=jnp.int32).at[dest].set(tok_sorted)
    h_pad = jnp.concatenate([h, jnp.zeros((1, D), h.dtype)], axis=0)
    xb = h_pad[row_tok].reshape(n_blocks, EXPERT_BLOCK, D)
    block_start = jnp.arange(n_blocks, dtype=jnp.int32) * EXPERT_BLOCK
    block_e = jnp.minimum(jnp.searchsorted(pad_end, block_start, side='right'), N_EXPERTS - 1)

    def expert_block(args):
        xblk, e = args
        gu = xblk @ w_gu[e] + b_gu[e]
        g, u = gu[:, :D_FF], gu[:, D_FF:]
        g = jnp.minimum(g, SWIGLU_LIMIT)
        u = jnp.clip(u, -SWIGLU_LIMIT, SWIGLU_LIMIT)
        act = (u + 1.0) * g * jax.nn.sigmoid(SWIGLU_ALPHA * g)
        return act @ w_down[e] + b_down[e]

    y = lax.map(expert_block, (xb, block_e)).reshape(n_rows, D)
    return jnp.zeros((N, D), h.dtype).at[tok_sorted].add(y[dest] * w_sorted[:, None])


def setup_inputs(seed: int = 0) -> dict:
    key = jax.random.key(seed)
    ks = iter(jax.random.split(key, 32))
    D, E, F = D_MODEL, N_EXPERTS, D_FF

    def nrm(shape, scale):
        return scale * jax.random.normal(next(ks), shape, jnp.float32)

    def gain(shape):
        return 1.0 + nrm(shape, 0.05)

    a0 = jax.random.uniform(next(ks), (N_LRU_LAYERS, 2, D_RNN), jnp.float32, 0.9, 0.999)
    return {
        'x': nrm((BATCH, SEQ, D), 1.0),
        'c': nrm((BATCH, D), 1.0),
        'ctx': nrm((BATCH, CTX_LEN, D), 1.0),
        'c_ctx': nrm((D,), 1.0),
        'ada_w': nrm((DEPTH, D, 6 * D), 0.5 * D ** -0.5),
        'ada_b': nrm((DEPTH, 6 * D), 0.01),
        'norm_mix': gain((DEPTH, D)),
        'norm_ffn': gain((DEPTH, D)),
        'pool_w': nrm((N_POOL_LAYERS, N_POOL_GROUPS, POOL_GROUP, POOL_GROUP), POOL_GROUP ** -0.5),
        'pool_scale': gain((N_POOL_LAYERS, D)),
        'lru_w_in': nrm((N_LRU_LAYERS, D, 2 * D_RNN), D ** -0.5),
        'lru_conv_w': nrm((N_LRU_LAYERS, CONV_WIDTH, D_RNN), CONV_WIDTH ** -0.5),
        'lru_conv_b': nrm((N_LRU_LAYERS, D_RNN), 0.01),
        'lru_w_r': nrm((N_LRU_LAYERS, 2, N_LRU_BLOCKS, LRU_BLOCK, LRU_BLOCK), LRU_BLOCK ** -0.5),
        'lru_b_r': nrm((N_LRU_LAYERS, 2, D_RNN), 0.01),
        'lru_w_i': nrm((N_LRU_LAYERS, 2, N_LRU_BLOCKS, LRU_BLOCK, LRU_BLOCK), LRU_BLOCK ** -0.5),
        'lru_b_i': nrm((N_LRU_LAYERS, 2, D_RNN), 0.01),
        'lru_lam': jnp.log(a0) - jnp.log1p(-a0),
        'lru_w_out': nrm((N_LRU_LAYERS, D_RNN, D), D_RNN ** -0.5),
        'router_w': nrm((DEPTH, D, E), D ** -0.5),
        'router_b': nrm((DEPTH, E), 0.01),
        'exp_w_gu': nrm((DEPTH, E, D, 2 * F), D ** -0.5),
        'exp_b_gu': nrm((DEPTH, E, 2 * F), 0.01),
        'exp_w_down': nrm((DEPTH, E, F, D), F ** -0.5),
        'exp_b_down': nrm((DEPTH, E, D), 0.01),
        'final_norm': gain((D,)),
    }


def reference(x, c, ctx, c_ctx, ada_w, ada_b, norm_mix, norm_ffn, pool_w, pool_scale,
              lru_w_in, lru_conv_w, lru_conv_b, lru_w_r, lru_b_r, lru_w_i, lru_b_i, lru_lam, lru_w_out,
              router_w, router_b, exp_w_gu, exp_b_gu, exp_w_down, exp_b_down, final_norm):
    B, T, D = x.shape
    lat, cx = x, ctx
    silu_c = jax.nn.silu(c)
    silu_cc = jax.nn.silu(c_ctx)
    for i in range(DEPTH):
        last = i == DEPTH - 1
        is_pool = i % N_MIXERS == 0
        j = i // N_MIXERS
        need_ctx_out = not last
        need_ctx_in = need_ctx_out or not is_pool
        sh1, sc1, g1, sh2, sc2, g2 = jnp.split((silu_c @ ada_w[i] + ada_b[i])[:, None, :], 6, axis=-1)
        csh1, csc1, cg1, csh2, csc2, cg2 = jnp.split(silu_cc @ ada_w[i] + ada_b[i], 6)

        hl = modulate(rms_norm(lat, norm_mix[i]), sh1, sc1)
        hc = modulate(rms_norm(cx, norm_mix[i]), csh1, csc1) if need_ctx_in else None
        if is_pool:
            yl = pool_mixer(hl, pool_w[j], pool_scale[j], pool_grid)
            yc = pool_mixer(hc, pool_w[j], pool_scale[j], pool_seq) if need_ctx_out else None
        else:
            yl, yc = lru_mixer(hl, hc, lru_w_in[j], lru_conv_w[j], lru_conv_b[j], lru_w_r[j], lru_b_r[j],
                               lru_w_i[j], lru_b_i[j], lru_lam[j], lru_w_out[j], need_ctx_out)
        lat = lat + g1 * yl

        hl = modulate(rms_norm(lat, norm_ffn[i]), sh2, sc2)
        if need_ctx_out:
            cx = cx + cg1 * yc
            hc = modulate(rms_norm(cx, norm_ffn[i]), csh2, csc2)
            tokens = jnp.concatenate([hl.reshape(-1, D), hc.reshape(-1, D)], axis=0)
            y = moe(tokens, router_w[i], router_b[i], exp_w_gu[i], exp_b_gu[i], exp_w_down[i], exp_b_down[i])
            lat = lat + g2 * y[:B * T].reshape(B, T, D)
            cx = cx + cg2 * y[B * T:].reshape(cx.shape)
        else:
            y = moe(hl.reshape(-1, D), router_w[i], router_b[i], exp_w_gu[i], exp_b_gu[i], exp_w_down[i], exp_b_down[i])
            lat = lat + g2 * y.reshape(B, T, D)
    return rms_norm(lat, final_norm)
```

```python
import functools

import numpy as np
import jax
import jax.numpy as jnp
from jax import lax
from jax.experimental import pallas as pl
from jax.experimental.pallas import tpu as pltpu

f32 = jnp.float32
bf16 = jnp.bfloat16
i32 = jnp.int32
HIGHEST = lax.Precision.HIGHEST

LANES = 128
SUBLANES = 8
VMEM_LIMIT = 56 * 1024 * 1024

GRID_W = 64
GRID_SHIFT = 6
POOL_WINDOWS = (2, 4, 8, 16)
N_POOL_GROUPS = 4
N_LRU_BLOCKS = 4
CONV_LEFT = 2
CONV_WIDTH = 4
LRU_C = 8.0
N_EXPERTS = 32
TOP_K = 4
SWIGLU_LIMIT = 7.0
SWIGLU_ALPHA = 1.702
NORM_EPS = 1e-6
GELU_C = 0.7978845608028654

EXPERT_ROWS = 256
ROUTE_TILE = 512
COMBINE_TILE = 256
SCAN_CHUNK = 256


def _params(*sem):
    return pltpu.CompilerParams(dimension_semantics=sem, vmem_limit_bytes=VMEM_LIMIT)


def _pick_tile(pref, *sizes):
    t = pref
    while any(s % t for s in sizes):
        t //= 2
    assert t >= SUBLANES
    return t


def _norm_mod(x, gain, scale, shift):
    ms = jnp.mean(x * x, axis=-1, keepdims=True)
    y = x * lax.rsqrt(ms + NORM_EPS)
    return (y * gain) * (1.0 + scale) + shift


def _row_map(n_lat_tiles, tiles_per_batch, ctx_row):
    def row(i):
        return jnp.where(i < n_lat_tiles, i // tiles_per_batch, ctx_row)
    return row


def _mod_spec(row, chunk, width=None, d=None):
    del d
    return pl.BlockSpec((None, 1, width), lambda i, *_: (row(i), 0, chunk))


def _ada_kernel(c_ref, w_ref, b_ref, o_ref):
    c = c_ref[...]
    s = c * jax.nn.sigmoid(c)
    o_ref[...] = jnp.dot(s, w_ref[...], preferred_element_type=f32, precision=HIGHEST) + b_ref[...]


def _ada_call(cc, ada_w, ada_b):
    depth, d, d6 = ada_w.shape
    rows = cc.shape[0]
    bn = 1536
    return pl.pallas_call(
        _ada_kernel, grid=(depth, d6 // bn),
        in_specs=[pl.BlockSpec((rows, d), lambda l, j: (0, 0)),
                  pl.BlockSpec((None, d, bn), lambda l, j: (l, 0, j)),
                  pl.BlockSpec((None, 1, bn), lambda l, j: (l, 0, j))],
        out_specs=pl.BlockSpec((None, rows, bn), lambda l, j: (l, 0, j)),
        out_shape=jax.ShapeDtypeStruct((depth, rows, d6), f32),
        compiler_params=_params("arbitrary", "arbitrary"), name="ada_mod",
    )(cc, ada_w, ada_b.reshape(depth, 1, d6))


def _norm_kernel(x_ref, g_ref, sc_ref, sh_ref, o_ref):
    o_ref[...] = _norm_mod(x_ref[...], g_ref[...], sc_ref[...], sh_ref[...])


def _norm_call(tokens, gain, mod, sh_chunk, sc_chunk, row, tile):
    n, d = tokens.shape
    return pl.pallas_call(
        _norm_kernel, grid=(n // tile,),
        in_specs=[pl.BlockSpec((tile, d), lambda i: (i, 0)),
                  pl.BlockSpec((1, d), lambda i: (0, 0)),
                  _mod_spec(row, sc_chunk, d), _mod_spec(row, sh_chunk, d)],
        out_specs=pl.BlockSpec((tile, d), lambda i: (i, 0)),
        out_shape=jax.ShapeDtypeStruct((n, d), f32),
        compiler_params=_params("arbitrary"), name="norm_mod",
    )(tokens, gain, mod, mod)


def _shift(x, j, unit, pos, limit):
    n = x.shape[0]
    rolled = pltpu.roll(x, (j * unit) % n, 0)
    ok = (pos >= j) if j > 0 else (pos < limit + j)
    return jnp.where(ok, rolled, 0.0)


def _window_sum(x, w, unit, pos, limit):
    m = w // 2
    trail, lead, s = x, x, 1
    while s < m:
        trail = trail + _shift(trail, s, unit, pos, limit)
        lead = lead + _shift(lead, -s, unit, pos, limit)
        s *= 2
    return _shift(trail, 1, unit, pos, limit) + lead


def _window_count(w, pos, limit):
    m = w // 2
    return jnp.minimum(pos + m, limit) - jnp.maximum(pos - m, 0)


def _pool_grid(h, w):
    t = lax.broadcasted_iota(i32, h.shape, 0)
    col, row = t & (GRID_W - 1), t >> GRID_SHIFT
    rows = h.shape[0] // GRID_W
    total = _window_sum(_window_sum(h, w, 1, col, GRID_W), w, GRID_W, row, rows)
    count = (_window_count(w, row, rows) * _window_count(w, col, GRID_W)).astype(f32)
    return total / count


def _pool_seq(h, w):
    t = lax.broadcasted_iota(i32, h.shape, 0)
    n = h.shape[0]
    return _window_sum(h, w, 1, t, n) / _window_count(w, t, n).astype(f32)


def _pool_lat_kernel(h_ref, x_ref, w_ref, ps_ref, g1_ref, o_ref):
    grp = pl.program_id(1)
    for gi, w in enumerate(POOL_WINDOWS):
        @pl.when(grp == gi)
        def _(w=w):
            h = h_ref[...]
            d = _pool_grid(h, w) - h
            y = jnp.dot(d.astype(bf16), w_ref[...].astype(bf16), preferred_element_type=f32)
            o_ref[...] = x_ref[...] + g1_ref[...] * (y * ps_ref[...])


def _pool_lat_call(h, x, pool_w, pool_scale, mod, batch, seq):
    n, d = x.shape
    cg = d // N_POOL_GROUPS
    g1_chunk0 = 2 * N_POOL_GROUPS
    return pl.pallas_call(
        _pool_lat_kernel, grid=(batch, N_POOL_GROUPS),
        in_specs=[pl.BlockSpec((seq, cg), lambda b, g: (b, g)),
                  pl.BlockSpec((seq, cg), lambda b, g: (b, g)),
                  pl.BlockSpec((None, cg, cg), lambda b, g: (g, 0, 0)),
                  pl.BlockSpec((1, cg), lambda b, g: (0, g)),
                  pl.BlockSpec((None, 1, cg), lambda b, g: (b, 0, g1_chunk0 + g))],
        out_specs=pl.BlockSpec((seq, cg), lambda b, g: (b, g)),
        out_shape=jax.ShapeDtypeStruct((n, d), f32),
        compiler_params=_params("arbitrary", "arbitrary"), name="pool_lat",
    )(h, x, pool_w, pool_scale, mod)


def _pool_ctx_kernel(x_ref, gain_ref, sc_ref, sh_ref, w_ref, ps_ref, g1_ref, o_ref):
    x = x_ref[...]
    h = _norm_mod(x, gain_ref[...], sc_ref[...], sh_ref[...])
    cg = x.shape[1] // N_POOL_GROUPS
    for gi, w in enumerate(POOL_WINDOWS):
        sl = slice(gi * cg, (gi + 1) * cg)
        hg = h[:, sl]
        d = _pool_seq(hg, w) - hg
        y = jnp.dot(d.astype(bf16), w_ref[gi].astype(bf16), preferred_element_type=f32)
        o_ref[:, sl] = x[:, sl] + g1_ref[:, sl] * (y * ps_ref[:, sl])


def _pool_ctx_call(cx, gain, pool_w, pool_scale, mod, ctx_row, batch, ctx_len):
    n, d = cx.shape
    cg = d // N_POOL_GROUPS
    row = lambda i: ctx_row
    return pl.pallas_call(
        _pool_ctx_kernel, grid=(batch,),
        in_specs=[pl.BlockSpec((ctx_len, d), lambda b: (b, 0)),
                  pl.BlockSpec((1, d), lambda b: (0, 0)),
                  _mod_spec(row, 1, d), _mod_spec(row, 0, d),
                  pl.BlockSpec((N_POOL_GROUPS, cg, cg), lambda b: (0, 0, 0)),
                  pl.BlockSpec((1, d), lambda b: (0, 0)),
                  _mod_spec(row, 2, d)],
        out_specs=pl.BlockSpec((ctx_len, d), lambda b: (b, 0)),
        out_shape=jax.ShapeDtypeStruct((n, d), f32),
        compiler_params=_params("arbitrary"), name="pool_ctx",
    )(cx, gain, mod, mod, pool_w, pool_scale, mod)


def _route_kernel(x_ref, gain_ref, sc_ref, sh_ref, wr_ref, rb_ref,
                  ht_ref, eidx_ref, gate_ref, rank_ref, cnt_ref, run_ref):
    step = pl.program_id(0)
    tt = x_ref.shape[0]

    @pl.when(step == 0)
    def _():
        run_ref[...] = jnp.zeros(run_ref.shape, f32)

    h = _norm_mod(x_ref[...], gain_ref[...], sc_ref[...], sh_ref[...])
    for s in range(SUBLANES):
        ht_ref[pl.ds(s, tt, stride=SUBLANES), :] = h[:, s * LANES:(s + 1) * LANES]

    logits = lax.dot_general(wr_ref[...], h, (((1,), (1,)), ((), ())),
                             precision=HIGHEST, preferred_element_type=f32) + rb_ref[...]
    e_iota = lax.broadcasted_iota(i32, logits.shape, 0)
    vals, onehots = [], []
    work = logits
    for k in range(TOP_K):
        m = jnp.max(work, axis=0, keepdims=True)
        idx = jnp.min(jnp.where(work == m, e_iota, N_EXPERTS), axis=0, keepdims=True)
        hit = e_iota == idx
        eidx_ref[k:k + 1, :] = idx
        vals.append(m)
        onehots.append(jnp.where(hit, 1.0, 0.0))
        work = jnp.where(hit, -jnp.inf, work)

    ex = [jnp.exp(v - vals[0]) for v in vals]
    den = ex[0] + ex[1] + ex[2] + ex[3]
    for k in range(TOP_K):
        gate_ref[k:k + 1, :] = ex[k] / den

    tri = jnp.where(lax.broadcasted_iota(i32, (tt, tt), 0) <= lax.broadcasted_iota(i32, (tt, tt), 1), 1.0, 0.0)
    planes = jnp.concatenate(onehots, axis=0).astype(bf16)
    incl = jnp.dot(planes, tri.astype(bf16), preferred_element_type=f32)
    offset = run_ref[...][:, 0:1]
    for k in range(TOP_K):
        inc_k = incl[k * N_EXPERTS:(k + 1) * N_EXPERTS]
        before = inc_k - onehots[k] + offset
        rank_ref[k:k + 1, :] = jnp.sum(onehots[k] * before, axis=0, keepdims=True).astype(i32)
        offset = offset + inc_k[:, tt - 1:tt]
    run_ref[...] = jnp.broadcast_to(offset, run_ref.shape)
    cnt_ref[...] = jnp.broadcast_to(offset, cnt_ref.shape)


def _route_call(tokens, gain, mod, row, w_router_t, b_router, tile):
    n, d = tokens.shape
    e = w_router_t.shape[0]
    return pl.pallas_call(
        _route_kernel, grid=(n // tile,),
        in_specs=[pl.BlockSpec((tile, d), lambda i: (i, 0)),
                  pl.BlockSpec((1, d), lambda i: (0, 0)),
                  _mod_spec(row, 4, d), _mod_spec(row, 3, d),
                  pl.BlockSpec((e, d), lambda i: (0, 0)),
                  pl.BlockSpec((e, 1), lambda i: (0, 0))],
        out_specs=[pl.BlockSpec((tile * SUBLANES, LANES), lambda i: (i, 0)),
                   pl.BlockSpec((TOP_K, tile), lambda i: (0, i)),
                   pl.BlockSpec((TOP_K, tile), lambda i: (0, i)),
                   pl.BlockSpec((TOP_K, tile), lambda i: (0, i)),
                   pl.BlockSpec((e, LANES), lambda i: (0, 0))],
        out_shape=[jax.ShapeDtypeStruct((n * SUBLANES, LANES), f32),
                   jax.ShapeDtypeStruct((TOP_K, n), i32),
                   jax.ShapeDtypeStruct((TOP_K, n), f32),
                   jax.ShapeDtypeStruct((TOP_K, n), i32),
                   jax.ShapeDtypeStruct((e, LANES), f32)],
        scratch_shapes=[pltpu.VMEM((e, LANES), f32)],
        compiler_params=_params("arbitrary"), name="moe_route",
    )(tokens, gain, mod, mod, w_router_t, b_router.reshape(e, 1))


def _gmm_kernel(be_ref, bv_ref, rt_ref, ht_hbm, wgu_ref, bgu_ref, wd_ref, bd_ref, o_ref,
                xbuf, wgu_bf, wd_bf, sem, *, n_blocks):
    j = pl.program_id(0)
    rows = rt_ref.shape[0]
    d_ff = wd_ref.shape[0]
    slot = j % 2
    jc = jnp.minimum(j, n_blocks - 1)

    @pl.when(jnp.logical_and(j < n_blocks, bv_ref[jc] > 0))
    def _():
        def issue(r, carry):
            tok = rt_ref[r]
            pltpu.make_async_copy(
                ht_hbm.at[pl.ds(pl.multiple_of(tok * SUBLANES, SUBLANES), SUBLANES), :],
                xbuf.at[slot, pl.ds(pl.multiple_of(r * SUBLANES, SUBLANES), SUBLANES), :],
                sem.at[slot]).start()
            return carry
        lax.fori_loop(0, rows, issue, 0)

    jp = jnp.maximum(j - 1, 0)
    active = jnp.logical_and(j >= 1, bv_ref[jp] > 0)

    @pl.when(active)
    def _():
        pslot = 1 - slot
        pltpu.make_async_copy(ht_hbm.at[pl.ds(0, rows * SUBLANES), :], xbuf.at[pslot], sem.at[pslot]).wait()

        new_expert = jnp.logical_or(j == 1, be_ref[jp] != be_ref[jnp.maximum(j - 2, 0)])

        @pl.when(new_expert)
        def _():
            wgu_bf[...] = wgu_ref[...].astype(bf16)
            wd_bf[...] = wd_ref[...].astype(bf16)

        x = jnp.concatenate([xbuf[pslot, pl.ds(s, rows, stride=SUBLANES), :] for s in range(SUBLANES)], axis=-1)
        gu = jnp.dot(x.astype(bf16), wgu_bf[...], preferred_element_type=f32) + bgu_ref[...]
        g = jnp.minimum(gu[:, :d_ff], SWIGLU_LIMIT)
        u = jnp.clip(gu[:, d_ff:], -SWIGLU_LIMIT, SWIGLU_LIMIT)
        act = (u + 1.0) * g * jax.nn.sigmoid(SWIGLU_ALPHA * g)
        y = jnp.dot(act.astype(bf16), wd_bf[...], preferred_element_type=f32) + bd_ref[...]
        for s in range(SUBLANES):
            o_ref[pl.ds(s, rows, stride=SUBLANES), :] = y[:, s * LANES:(s + 1) * LANES]

    @pl.when(jnp.logical_not(active))
    def _():
        o_ref[...] = jnp.zeros(o_ref.shape, f32)


def _gmm_call(blk_e, blk_valid, row_tok, ht, w_gu, b_gu, w_down, b_down, n_blocks):
    e, d, f2 = w_gu.shape
    d_ff = w_down.shape[1]
    rows = EXPERT_ROWS
    prev = lambda j: jnp.maximum(j - 1, 0)
    grid_spec = pltpu.PrefetchScalarGridSpec(
        num_scalar_prefetch=2, grid=(n_blocks + 1,),
        in_specs=[pl.BlockSpec((rows,), lambda j, be, bv: (jnp.minimum(j, n_blocks - 1),), memory_space=pltpu.SMEM),
                  pl.BlockSpec(memory_space=pl.ANY),
                  pl.BlockSpec((None, d, f2), lambda j, be, bv: (be[prev(j)], 0, 0)),
                  pl.BlockSpec((None, 1, f2), lambda j, be, bv: (be[prev(j)], 0, 0)),
                  pl.BlockSpec((None, d_ff, d), lambda j, be, bv: (be[prev(j)], 0, 0)),
                  pl.BlockSpec((None, 1, d), lambda j, be, bv: (be[prev(j)], 0, 0))],
        out_specs=pl.BlockSpec((rows * SUBLANES, LANES), lambda j, be, bv: (prev(j), 0)),
        scratch_shapes=[pltpu.VMEM((2, rows * SUBLANES, LANES), f32),
                        pltpu.VMEM((d, f2), bf16), pltpu.VMEM((d_ff, d), bf16),
                        pltpu.SemaphoreType.DMA((2,))])
    return pl.pallas_call(
        functools.partial(_gmm_kernel, n_blocks=n_blocks), grid_spec=grid_spec,
        out_shape=jax.ShapeDtypeStruct((n_blocks * rows * SUBLANES, LANES), f32),
        compiler_params=_params("arbitrary"), name="moe_experts",
    )(blk_e, blk_valid, row_tok, ht, w_gu, b_gu.reshape(e, 1, f2), w_down, b_down.reshape(e, 1, d))


def _combine_kernel(dest_ref, ys_hbm, gate_ref, x_ref, g2_ref, fn_ref, o_ref, ybuf, sem, *, final):
    tc = x_ref.shape[0]
    n_rows = TOP_K * tc

    def issue(r, carry):
        dst = dest_ref[r]
        pltpu.make_async_copy(
            ys_hbm.at[pl.ds(pl.multiple_of(dst * SUBLANES, SUBLANES), SUBLANES), :],
            ybuf.at[pl.ds(pl.multiple_of(r * SUBLANES, SUBLANES), SUBLANES), :], sem).start()
        return carry
    lax.fori_loop(0, n_rows, issue, 0)
    pltpu.make_async_copy(ys_hbm.at[pl.ds(0, n_rows * SUBLANES), :], ybuf, sem).wait()

    gates = gate_ref[...]
    pieces = []
    for s in range(SUBLANES):
        acc = None
        for k in range(TOP_K):
            term = gates[:, k:k + 1] * ybuf[pl.ds(k * tc * SUBLANES + s, tc, stride=SUBLANES), :]
            acc = term if acc is None else acc + term
        sl = slice(s * LANES, (s + 1) * LANES)
        pieces.append(x_ref[:, sl] + g2_ref[:, sl] * acc)
    if final:
        y = jnp.concatenate(pieces, axis=-1)
        ms = jnp.mean(y * y, axis=-1, keepdims=True)
        o_ref[...] = (y * lax.rsqrt(ms + NORM_EPS)) * fn_ref[...]
    else:
        for s in range(SUBLANES):
            o_ref[:, s * LANES:(s + 1) * LANES] = pieces[s]


def _combine_call(dest_tiles, ys, gates_t, tokens, mod, row, final_gain, tile, final):
    n, d = tokens.shape
    return pl.pallas_call(
        functools.partial(_combine_kernel, final=final), grid=(n // tile,),
        in_specs=[pl.BlockSpec((TOP_K * tile,), lambda i: (i,), memory_space=pltpu.SMEM),
                  pl.BlockSpec(memory_space=pl.ANY),
                  pl.BlockSpec((tile, TOP_K), lambda i: (i, 0)),
                  pl.BlockSpec((tile, d), lambda i: (i, 0)),
                  _mod_spec(row, 5, d),
                  pl.BlockSpec((1, d), lambda i: (0, 0))],
        out_specs=pl.BlockSpec((tile, d), lambda i: (i, 0)),
        out_shape=jax.ShapeDtypeStruct((n, d), f32),
        scratch_shapes=[pltpu.VMEM((TOP_K * tile * SUBLANES, LANES), f32), pltpu.SemaphoreType.DMA],
        compiler_params=_params("arbitrary"), name="moe_combine",
    )(dest_tiles, ys, gates_t, tokens, mod, final_gain)


def _moe_layer(tokens, gain, mod, n_lat, seq, ctx_row, w_router, b_router, w_gu, b_gu, w_down, b_down,
               final_gain, final):
    n, d = tokens.shape
    rt = _pick_tile(ROUTE_TILE, n_lat, n - n_lat if n > n_lat else n_lat, seq)
    ct = _pick_tile(COMBINE_TILE, n_lat, n - n_lat if n > n_lat else n_lat, seq)
    row_r = _row_map(n_lat // rt, seq // rt, ctx_row)
    row_c = _row_map(n_lat // ct, seq // ct, ctx_row)

    ht, eidx, gate, rank, cnt = _route_call(tokens, gain, mod, row_r, w_router.T, b_router, rt)

    rows = EXPERT_ROWS
    counts = cnt[:, 0].astype(i32)
    padded = (counts + rows - 1) // rows * rows
    pad_end = jnp.cumsum(padded)
    pad_start = pad_end - padded
    dest = pad_start[eidx] + rank
    n_blocks = (n * TOP_K + N_EXPERTS * (rows - 1) + rows - 1) // rows
    blk_start = jnp.arange(n_blocks, dtype=i32) * rows
    blk_e = jnp.minimum(jnp.searchsorted(pad_end, blk_start, side='right'), N_EXPERTS - 1).astype(i32)
    blk_valid = jnp.clip(counts[blk_e] - (blk_start - pad_start[blk_e]), 0, rows).astype(i32)
    tok_ids = jnp.broadcast_to(jnp.arange(n, dtype=i32)[None, :], (TOP_K, n))
    row_tok = jnp.zeros((n_blocks * rows,), i32).at[dest.reshape(-1)].set(tok_ids.reshape(-1))

    ys = _gmm_call(blk_e, blk_valid, row_tok, ht, w_gu, b_gu, w_down, b_down, n_blocks)

    dest_tiles = dest.reshape(TOP_K, n // ct, ct).transpose(1, 0, 2).reshape(-1)
    return _combine_call(dest_tiles, ys, gate.T, tokens, mod, row_c, final_gain, ct, final)


def _lru_in_kernel(x_ref, gain_ref, sc_ref, sh_ref, w_ref, gg_ref, u_ref):
    d_rnn = gg_ref.shape[1]
    h = _norm_mod(x_ref[...], gain_ref[...], sc_ref[...], sh_ref[...])
    gu = jnp.dot(h.astype(bf16), w_ref[...], preferred_element_type=f32)
    gate = gu[:, :d_rnn]
    gg_ref[...] = 0.5 * gate * (1.0 + jnp.tanh(GELU_C * (gate + 0.044715 * (gate * gate * gate))))
    u_ref[...] = gu[:, d_rnn:]


def _lru_in_call(tokens, gain, mod, row, w_in_bf, tile):
    n, d = tokens.shape
    d2 = w_in_bf.shape[1]
    d_rnn = d2 // 2
    return pl.pallas_call(
        _lru_in_kernel, grid=(n // tile,),
        in_specs=[pl.BlockSpec((tile, d), lambda i: (i, 0)),
                  pl.BlockSpec((1, d), lambda i: (0, 0)),
                  _mod_spec(row, 1, d), _mod_spec(row, 0, d),
                  pl.BlockSpec((d, d2), lambda i: (0, 0))],
        out_specs=[pl.BlockSpec((tile, d_rnn), lambda i: (i, 0)), pl.BlockSpec((tile, d_rnn), lambda i: (i, 0))],
        out_shape=[jax.ShapeDtypeStruct((n, d_rnn), f32), jax.ShapeDtypeStruct((n, d_rnn), f32)],
        compiler_params=_params("arbitrary"), name="lru_in",
    )(tokens, gain, mod, mod, w_in_bf)


def _conv(u, cw_ref, cb_ref):
    n = u.shape[0]
    t = lax.broadcasted_iota(i32, u.shape, 0)
    y = None
    for k in range(CONV_WIDTH):
        j = CONV_LEFT - k
        term = (u if j == 0 else _shift(u, j, 1, t, n)) * cw_ref[k:k + 1, :]
        y = term if y is None else y + term
    return y + cb_ref[...]


def _lru_kernel(ul_ref, uc_ref, gg_ref, cw_ref, cb_ref, wr_ref, br_ref, wi_ref, bi_ref, lam_ref, o_ref,
                ucl, ucc, hf, a_s, b_s, *, chunk):
    seq = ul_ref.shape[0]
    ctx_len = uc_ref.shape[0]
    width = ul_ref.shape[1]
    sub = lax.broadcasted_iota(i32, (SUBLANES, width), 0)

    ucl[...] = _conv(ul_ref[...], cw_ref, cb_ref)
    ucc[...] = _conv(uc_ref[...], cw_ref, cb_ref)

    def run_direction(src, length, d, reverse, carry, store):
        w_r = wr_ref[d].astype(bf16)
        w_i = wi_ref[d].astype(bf16)
        b_r, b_i = br_ref[d], bi_ref[d]
        lam = lam_ref[d]
        softplus_neg = jnp.maximum(-lam, 0.0) + jnp.log1p(jnp.exp(-jnp.abs(lam)))
        n_chunks = length // chunk
        n_groups = chunk // SUBLANES

        def chunk_body(ci, carry):
            c = (n_chunks - 1 - ci) if reverse else ci
            base = pl.multiple_of(c * chunk, chunk)
            u = src[pl.ds(base, chunk), :]
            ub = u.astype(bf16)
            r = jax.nn.sigmoid(jnp.dot(ub, w_r, preferred_element_type=f32) + b_r)
            gi = jax.nn.sigmoid(jnp.dot(ub, w_i, preferred_element_type=f32) + b_i)
            log_a = (-LRU_C * r) * softplus_neg
            a = jnp.exp(log_a)
            mult = jnp.sqrt(-jnp.tanh(log_a) * (a * a + 1.0))
            a_s[...] = a
            b_s[...] = mult * (gi * u)

            def group_body(gidx, h8):
                g = (n_groups - 1 - gidx) if reverse else gidx
                off = pl.multiple_of(g * SUBLANES, SUBLANES)
                av = a_s[pl.ds(off, SUBLANES), :]
                bv = b_s[pl.ds(off, SUBLANES), :]
                for s in (1, 2, 4):
                    if reverse:
                        ok = sub < SUBLANES - s
                        a_sh = jnp.where(ok, pltpu.roll(av, SUBLANES - s, 0), 1.0)
                        b_sh = jnp.where(ok, pltpu.roll(bv, SUBLANES - s, 0), 0.0)
                    else:
                        ok = sub >= s
                        a_sh = jnp.where(ok, pltpu.roll(av, s, 0), 1.0)
                        b_sh = jnp.where(ok, pltpu.roll(bv, s, 0), 0.0)
                    bv = bv + av * b_sh
                    av = av * a_sh
                h = av * h8 + bv
                if store is not None:
                    store(pl.multiple_of(base + off, SUBLANES), h)
                edge = h[0:1, :] if reverse else h[SUBLANES - 1:SUBLANES, :]
                return jnp.broadcast_to(edge, h.shape)

            return lax.fori_loop(0, n_groups, group_body, carry)

        return lax.fori_loop(0, n_chunks, chunk_body, carry)

    zero = jnp.zeros((SUBLANES, width), f32)

    def store_fwd(row, h):
        hf[pl.ds(row, SUBLANES), :] = h

    def store_out(row, h):
        o_ref[pl.ds(row, SUBLANES), :] = gg_ref[pl.ds(row, SUBLANES), :] * (hf[pl.ds(row, SUBLANES), :] + h)

    h0 = run_direction(ucc, ctx_len, 0, False, zero, None)
    run_direction(ucl, seq, 0, False, h0, store_fwd)
    h0 = run_direction(ucc, ctx_len, 1, True, zero, None)
    run_direction(ucl, seq, 1, True, h0, store_out)


def _lru_call(gg, u, conv_w, conv_b, w_r, b_r, w_i, b_i, lam, batch, seq, ctx_len):
    n_lat = batch * seq
    d_rnn = u.shape[1]
    wb = d_rnn // N_LRU_BLOCKS
    chunk = _pick_tile(SCAN_CHUNK, seq, ctx_len)
    ctx_blk0 = n_lat // ctx_len
    vec = lambda a: a.reshape(2, 1, d_rnn)
    vspec = pl.BlockSpec((2, 1, wb), lambda b, n: (0, 0, n))
    wspec = pl.BlockSpec((2, None, wb, wb), lambda b, n: (0, n, 0, 0))
    return pl.pallas_call(
        functools.partial(_lru_kernel, chunk=chunk), grid=(batch, N_LRU_BLOCKS),
        in_specs=[pl.BlockSpec((seq, wb), lambda b, n: (b, n)),
                  pl.BlockSpec((ctx_len, wb), lambda b, n: (ctx_blk0 + b, n)),
                  pl.BlockSpec((seq, wb), lambda b, n: (b, n)),
                  pl.BlockSpec((CONV_WIDTH, wb), lambda b, n: (0, n)),
                  pl.BlockSpec((1, wb), lambda b, n: (0, n)),
                  wspec, vspec, wspec, vspec, vspec],
        out_specs=pl.BlockSpec((seq, wb), lambda b, n: (b, n)),
        out_shape=jax.ShapeDtypeStruct((n_lat, d_rnn), f32),
        scratch_shapes=[pltpu.VMEM((seq, wb), f32), pltpu.VMEM((ctx_len, wb), f32), pltpu.VMEM((seq, wb), f32),
                        pltpu.VMEM((chunk, wb), f32), pltpu.VMEM((chunk, wb), f32)],
        compiler_params=_params("arbitrary", "arbitrary"), name="lru_scan",
    )(u, u, gg, conv_w, conv_b.reshape(1, d_rnn), w_r, vec(b_r), w_i, vec(b_i), vec(lam))


def _lru_out_kernel(z_ref, x_ref, w_ref, g1_ref, o_ref):
    y = jnp.dot(z_ref[...].astype(bf16), w_ref[...], preferred_element_type=f32)
    o_ref[...] = x_ref[...] + g1_ref[...] * y


def _lru_out_call(z, tokens, w_out_bf, mod, row, tile):
    n, d = z.shape[0], tokens.shape[1]
    d_rnn = z.shape[1]
    return pl.pallas_call(
        _lru_out_kernel, grid=(n // tile,),
        in_specs=[pl.BlockSpec((tile, d_rnn), lambda i: (i, 0)),
                  pl.BlockSpec((tile, d), lambda i: (i, 0)),
                  pl.BlockSpec((d_rnn, d), lambda i: (0, 0)),
                  _mod_spec(row, 2, d)],
        out_specs=pl.BlockSpec((tile, d), lambda i: (i, 0)),
        out_shape=jax.ShapeDtypeStruct((n, d), f32),
        compiler_params=_params("arbitrary"), name="lru_out",
    )(z, tokens, w_out_bf, mod)


def kernel(x, c, ctx, c_ctx, ada_w, ada_b, norm_mix, norm_ffn, pool_w, pool_scale, lru_w_in, lru_conv_w, lru_conv_b, lru_w_r, lru_b_r, lru_w_i, lru_b_i, lru_lam, lru_w_out, router_w, router_b, exp_w_gu, exp_b_gu, exp_w_down, exp_b_down, final_norm):
    batch, seq, d = x.shape
    ctx_len = ctx.shape[1]
    assert d == SUBLANES * LANES and seq % GRID_W == 0
    n_lat, n_ctx = batch * seq, batch * ctx_len
    ctx_row = batch

    mod_rows = -(-(batch + 1) // SUBLANES) * SUBLANES
    cc = jnp.zeros((mod_rows, d), f32).at[:batch].set(c).at[batch].set(c_ctx)
    mod_all = _ada_call(cc, ada_w, ada_b)
    mods = [mod_all[l].reshape(mod_rows, 1, 6 * d) for l in range(ada_w.shape[0])]

    lat = x.reshape(n_lat, d)
    cx = ctx.reshape(n_ctx, d)
    tile = _pick_tile(ROUTE_TILE, seq, n_ctx)
    row_lat = _row_map(n_lat // tile, seq // tile, ctx_row)
    gain = lambda g: g.reshape(1, d)

    hl = _norm_call(lat, gain(norm_mix[0]), mods[0], 0, 1, row_lat, tile)
    lat = _pool_lat_call(hl, lat, pool_w[0], gain(pool_scale[0]), mods[0], batch, seq)
    cx = _pool_ctx_call(cx, gain(norm_mix[0]), pool_w[0], gain(pool_scale[0]), mods[0], ctx_row, batch, ctx_len)
    tokens = jnp.concatenate([lat, cx], axis=0)
    tokens = _moe_layer(tokens, gain(norm_ffn[0]), mods[0], n_lat, seq, ctx_row, router_w[0], router_b[0],
                        exp_w_gu[0], exp_b_gu[0], exp_w_down[0], exp_b_down[0], gain(final_norm), False)

    row_all = _row_map(n_lat // tile, seq // tile, ctx_row)
    gg, u = _lru_in_call(tokens, gain(norm_mix[1]), mods[1], row_all, lru_w_in[0].astype(bf16), tile)
    z = _lru_call(gg, u, lru_conv_w[0], lru_conv_b[0], lru_w_r[0], lru_b_r[0], lru_w_i[0], lru_b_i[0],
                  lru_lam[0], batch, seq, ctx_len)
    lat = _lru_out_call(z, tokens, lru_w_out[0].astype(bf16), mods[1], row_lat, tile)
    out = _moe_layer(lat, gain(norm_ffn[1]), mods[1], n_lat, seq, ctx_row, router_w[1], router_b[1],
                     exp_w_gu[1], exp_b_gu[1], exp_w_down[1], exp_b_down[1], gain(final_norm), True)
    return out.reshape(batch, seq, d)
```

```python
import functools

import numpy as np
import jax
import jax.numpy as jnp
from jax import lax
from jax.experimental import pallas as pl
from jax.experimental.pallas import tpu as pltpu

f32 = jnp.float32
bf16 = jnp.bfloat16
i32 = jnp.int32
HIGHEST = lax.Precision.HIGHEST

LANES = 128
SUBLANES = 8
VMEM_LIMIT = 56 * 1024 * 1024

GRID_W = 64
GRID_SHIFT = 6
POOL_WINDOWS = (2, 4, 8, 16)
N_POOL_GROUPS = 4
N_LRU_BLOCKS = 4
CONV_LEFT = 2
CONV_WIDTH = 4
LRU_C = 8.0
N_EXPERTS = 32
TOP_K = 4
SWIGLU_LIMIT = 7.0
SWIGLU_ALPHA = 1.702
NORM_EPS = 1e-6
GELU_C = 0.7978845608028654

EXPERT_ROWS = 256
ROUTE_TILE = 512
COMBINE_TILE = 256
SCAN_CHUNK = 256
FF_CHUNKS = 1
ISSUE_UNROLL = 8


def _params(*sem):
    return pltpu.CompilerParams(dimension_semantics=sem, vmem_limit_bytes=VMEM_LIMIT)


def _pick_tile(pref, *sizes):
    t = pref
    while any(s % t for s in sizes):
        t //= 2
    assert t >= SUBLANES
    return t


def _norm_mod(x, gain, scale, shift):
    ms = jnp.mean(x * x, axis=-1, keepdims=True)
    y = x * lax.rsqrt(ms + NORM_EPS)
    return (y * gain) * (1.0 + scale) + shift


def _row_map(n_lat_tiles, tiles_per_batch, ctx_row):
    def row(i):
        return jnp.where(i < n_lat_tiles, i // tiles_per_batch, ctx_row)
    return row


def _mod_spec(row, chunk, width=None, d=None):
    del d
    return pl.BlockSpec((None, 1, width), lambda i, *_: (row(i), 0, chunk))


def _ada_kernel(c_ref, w_ref, b_ref, o_ref):
    c = c_ref[...]
    s = c * jax.nn.sigmoid(c)
    o_ref[...] = jnp.dot(s, w_ref[...], preferred_element_type=f32, precision=HIGHEST) + b_ref[...]


def _ada_call(cc, ada_w, ada_b):
    depth, d, d6 = ada_w.shape
    rows = cc.shape[0]
    bn = 1536
    return pl.pallas_call(
        _ada_kernel, grid=(depth, d6 // bn),
        in_specs=[pl.BlockSpec((rows, d), lambda l, j: (0, 0)),
                  pl.BlockSpec((None, d, bn), lambda l, j: (l, 0, j)),
                  pl.BlockSpec((None, 1, bn), lambda l, j: (l, 0, j))],
        out_specs=pl.BlockSpec((None, rows, bn), lambda l, j: (l, 0, j)),
        out_shape=jax.ShapeDtypeStruct((depth, rows, d6), f32),
        compiler_params=_params("arbitrary", "arbitrary"), name="ada_mod",
    )(cc, ada_w, ada_b.reshape(depth, 1, d6))


def _norm_kernel(x_ref, g_ref, sc_ref, sh_ref, o_ref):
    o_ref[...] = _norm_mod(x_ref[...], g_ref[...], sc_ref[...], sh_ref[...])


def _norm_call(tokens, gain, mod, sh_chunk, sc_chunk, row, tile):
    n, d = tokens.shape
    return pl.pallas_call(
        _norm_kernel, grid=(n // tile,),
        in_specs=[pl.BlockSpec((tile, d), lambda i: (i, 0)),
                  pl.BlockSpec((1, d), lambda i: (0, 0)),
                  _mod_spec(row, sc_chunk, d), _mod_spec(row, sh_chunk, d)],
        out_specs=pl.BlockSpec((tile, d), lambda i: (i, 0)),
        out_shape=jax.ShapeDtypeStruct((n, d), f32),
        compiler_params=_params("arbitrary"), name="norm_mod",
    )(tokens, gain, mod, mod)


def _shift(x, j, unit, pos, limit):
    n = x.shape[0]
    rolled = pltpu.roll(x, (j * unit) % n, 0)
    ok = (pos >= j) if j > 0 else (pos < limit + j)
    return jnp.where(ok, rolled, 0.0)


def _window_sum(x, w, unit, pos, limit):
    m = w // 2
    trail, lead, s = x, x, 1
    while s < m:
        trail = trail + _shift(trail, s, unit, pos, limit)
        lead = lead + _shift(lead, -s, unit, pos, limit)
        s *= 2
    return _shift(trail, 1, unit, pos, limit) + lead


def _window_count(w, pos, limit):
    m = w // 2
    return jnp.minimum(pos + m, limit) - jnp.maximum(pos - m, 0)


def _pool_grid(h, w):
    t = lax.broadcasted_iota(i32, h.shape, 0)
    col, row = t & (GRID_W - 1), t >> GRID_SHIFT
    rows = h.shape[0] // GRID_W
    total = _window_sum(_window_sum(h, w, 1, col, GRID_W), w, GRID_W, row, rows)
    count = (_window_count(w, row, rows) * _window_count(w, col, GRID_W)).astype(f32)
    return total / count


def _pool_seq(h, w):
    t = lax.broadcasted_iota(i32, h.shape, 0)
    n = h.shape[0]
    return _window_sum(h, w, 1, t, n) / _window_count(w, t, n).astype(f32)


def _pool_lat_kernel(h_ref, x_ref, w_ref, ps_ref, g1_ref, o_ref):
    grp = pl.program_id(1)
    for gi, w in enumerate(POOL_WINDOWS):
        @pl.when(grp == gi)
        def _(w=w):
            h = h_ref[...]
            d = _pool_grid(h, w) - h
            y = jnp.dot(d.astype(bf16), w_ref[...].astype(bf16), preferred_element_type=f32)
            o_ref[...] = x_ref[...] + g1_ref[...] * (y * ps_ref[...])


def _pool_lat_call(h, x, pool_w, pool_scale, mod, batch, seq):
    n, d = x.shape
    cg = d // N_POOL_GROUPS
    g1_chunk0 = 2 * N_POOL_GROUPS
    return pl.pallas_call(
        _pool_lat_kernel, grid=(batch, N_POOL_GROUPS),
        in_specs=[pl.BlockSpec((seq, cg), lambda b, g: (b, g)),
                  pl.BlockSpec((seq, cg), lambda b, g: (b, g)),
                  pl.BlockSpec((None, cg, cg), lambda b, g: (g, 0, 0)),
                  pl.BlockSpec((1, cg), lambda b, g: (0, g)),
                  pl.BlockSpec((None, 1, cg), lambda b, g: (b, 0, g1_chunk0 + g))],
        out_specs=pl.BlockSpec((seq, cg), lambda b, g: (b, g)),
        out_shape=jax.ShapeDtypeStruct((n, d), f32),
        compiler_params=_params("arbitrary", "arbitrary"), name="pool_lat",
    )(h, x, pool_w, pool_scale, mod)


def _pool_ctx_kernel(x_ref, gain_ref, sc_ref, sh_ref, w_ref, ps_ref, g1_ref, o_ref):
    x = x_ref[...]
    h = _norm_mod(x, gain_ref[...], sc_ref[...], sh_ref[...])
    cg = x.shape[1] // N_POOL_GROUPS
    for gi, w in enumerate(POOL_WINDOWS):
        sl = slice(gi * cg, (gi + 1) * cg)
        hg = h[:, sl]
        d = _pool_seq(hg, w) - hg
        y = jnp.dot(d.astype(bf16), w_ref[gi].astype(bf16), preferred_element_type=f32)
        o_ref[:, sl] = x[:, sl] + g1_ref[:, sl] * (y * ps_ref[:, sl])


def _pool_ctx_call(cx, gain, pool_w, pool_scale, mod, ctx_row, batch, ctx_len):
    n, d = cx.shape
    cg = d // N_POOL_GROUPS
    row = lambda i: ctx_row
    return pl.pallas_call(
        _pool_ctx_kernel, grid=(batch,),
        in_specs=[pl.BlockSpec((ctx_len, d), lambda b: (b, 0)),
                  pl.BlockSpec((1, d), lambda b: (0, 0)),
                  _mod_spec(row, 1, d), _mod_spec(row, 0, d),
                  pl.BlockSpec((N_POOL_GROUPS, cg, cg), lambda b: (0, 0, 0)),
                  pl.BlockSpec((1, d), lambda b: (0, 0)),
                  _mod_spec(row, 2, d)],
        out_specs=pl.BlockSpec((ctx_len, d), lambda b: (b, 0)),
        out_shape=jax.ShapeDtypeStruct((n, d), f32),
        compiler_params=_params("arbitrary"), name="pool_ctx",
    )(cx, gain, mod, mod, pool_w, pool_scale, mod)


def _route_kernel(x_ref, gain_ref, sc_ref, sh_ref, wr_ref, rb_ref,
                  ht_ref, eidx_ref, gate_ref, rank_ref, cnt_ref, run_ref):
    step = pl.program_id(0)
    tt = x_ref.shape[0]

    @pl.when(step == 0)
    def _():
        run_ref[...] = jnp.zeros(run_ref.shape, f32)

    h = _norm_mod(x_ref[...], gain_ref[...], sc_ref[...], sh_ref[...])
    for s in range(SUBLANES):
        ht_ref[pl.ds(s, tt, stride=SUBLANES), :] = h[:, s * LANES:(s + 1) * LANES]

    logits = lax.dot_general(wr_ref[...], h, (((1,), (1,)), ((), ())),
                             precision=HIGHEST, preferred_element_type=f32) + rb_ref[...]
    e_iota = lax.broadcasted_iota(i32, logits.shape, 0)
    vals, onehots = [], []
    work = logits
    for k in range(TOP_K):
        m = jnp.max(work, axis=0, keepdims=True)
        idx = jnp.min(jnp.where(work == m, e_iota, N_EXPERTS), axis=0, keepdims=True)
        hit = e_iota == idx
        eidx_ref[k:k + 1, :] = idx
        vals.append(m)
        onehots.append(jnp.where(hit, 1.0, 0.0))
        work = jnp.where(hit, -jnp.inf, work)

    ex = [jnp.exp(v - vals[0]) for v in vals]
    den = ex[0] + ex[1] + ex[2] + ex[3]
    for k in range(TOP_K):
        gate_ref[k:k + 1, :] = ex[k] / den

    tri = jnp.where(lax.broadcasted_iota(i32, (tt, tt), 0) <= lax.broadcasted_iota(i32, (tt, tt), 1), 1.0, 0.0)
    planes = jnp.concatenate(onehots, axis=0).astype(bf16)
    incl = jnp.dot(planes, tri.astype(bf16), preferred_element_type=f32)
    offset = run_ref[...][:, 0:1]
    for k in range(TOP_K):
        inc_k = incl[k * N_EXPERTS:(k + 1) * N_EXPERTS]
        before = inc_k - onehots[k] + offset
        rank_ref[k:k + 1, :] = jnp.sum(onehots[k] * before, axis=0, keepdims=True).astype(i32)
        offset = offset + inc_k[:, tt - 1:tt]
    run_ref[...] = jnp.broadcast_to(offset, run_ref.shape)
    cnt_ref[...] = jnp.broadcast_to(offset, cnt_ref.shape)


def _route_call(tokens, gain, mod, row, w_router_t, b_router, tile):
    n, d = tokens.shape
    e = w_router_t.shape[0]
    return pl.pallas_call(
        _route_kernel, grid=(n // tile,),
        in_specs=[pl.BlockSpec((tile, d), lambda i: (i, 0)),
                  pl.BlockSpec((1, d), lambda i: (0, 0)),
                  _mod_spec(row, 4, d), _mod_spec(row, 3, d),
                  pl.BlockSpec((e, d), lambda i: (0, 0)),
                  pl.BlockSpec((e, 1), lambda i: (0, 0))],
        out_specs=[pl.BlockSpec((tile * SUBLANES, LANES), lambda i: (i, 0)),
                   pl.BlockSpec((TOP_K, tile), lambda i: (0, i)),
                   pl.BlockSpec((TOP_K, tile), lambda i: (0, i)),
                   pl.BlockSpec((TOP_K, tile), lambda i: (0, i)),
                   pl.BlockSpec((e, LANES), lambda i: (0, 0))],
        out_shape=[jax.ShapeDtypeStruct((n * SUBLANES, LANES), f32),
                   jax.ShapeDtypeStruct((TOP_K, n), i32),
                   jax.ShapeDtypeStruct((TOP_K, n), f32),
                   jax.ShapeDtypeStruct((TOP_K, n), i32),
                   jax.ShapeDtypeStruct((e, LANES), f32)],
        scratch_shapes=[pltpu.VMEM((e, LANES), f32)],
        compiler_params=_params("arbitrary"), name="moe_route",
    )(tokens, gain, mod, mod, w_router_t, b_router.reshape(e, 1))


def _gmm_kernel(be_ref, bv_ref, rt_ref, ht_hbm, wgu_ref, bgu_ref, wd_ref, bd_ref, o_ref,
                xbuf, wgu_bf, wd_bf, sem, *, n_blocks):
    j = pl.program_id(0)
    rows = rt_ref.shape[0]
    d_ff = wd_ref.shape[0]
    slot = j % 2
    pslot = 1 - slot
    jc = jnp.minimum(j, n_blocks - 1)
    jp = jnp.maximum(j - 1, 0)
    gather_now = jnp.logical_and(j < n_blocks, bv_ref[jc] > 0)
    compute_now = jnp.logical_and(j >= 1, bv_ref[jp] > 0)

    def issue():
        for r in range(rows):
            tok = rt_ref[r]
            pltpu.make_async_copy(
                ht_hbm.at[pl.ds(pl.multiple_of(tok * SUBLANES, SUBLANES), SUBLANES), :],
                xbuf.at[slot, pl.ds(r * SUBLANES, SUBLANES), :], sem.at[slot]).start()

    def load_rows():
        return jnp.concatenate([xbuf[pslot, pl.ds(s, rows, stride=SUBLANES), :] for s in range(SUBLANES)],
                               axis=-1).astype(bf16)

    def compute(x):
        cw = d_ff // FF_CHUNKS
        y = None
        for c in range(FF_CHUNKS):
            gs, us = slice(c * cw, (c + 1) * cw), slice(d_ff + c * cw, d_ff + (c + 1) * cw)
            g = jnp.dot(x, wgu_bf[:, gs], preferred_element_type=f32) + bgu_ref[:, gs]
            u = jnp.dot(x, wgu_bf[:, us], preferred_element_type=f32) + bgu_ref[:, us]
            g = jnp.minimum(g, SWIGLU_LIMIT)
            u = jnp.clip(u, -SWIGLU_LIMIT, SWIGLU_LIMIT)
            act = ((u + 1.0) * g * jax.nn.sigmoid(SWIGLU_ALPHA * g)).astype(bf16)
            part = jnp.dot(act, wd_bf[gs, :], preferred_element_type=f32)
            y = part if y is None else y + part
        y = y + bd_ref[...]
        for s in range(SUBLANES):
            o_ref[pl.ds(s, rows, stride=SUBLANES), :] = y[:, s * LANES:(s + 1) * LANES]

    @pl.when(compute_now)
    def _():
        pltpu.make_async_copy(ht_hbm.at[pl.ds(0, rows * SUBLANES), :], xbuf.at[pslot], sem.at[pslot]).wait()
        new_expert = jnp.logical_or(j == 1, be_ref[jp] != be_ref[jnp.maximum(j - 2, 0)])

        @pl.when(new_expert)
        def _():
            wgu_bf[...] = wgu_ref[...].astype(bf16)
            wd_bf[...] = wd_ref[...].astype(bf16)

    @pl.when(jnp.logical_and(compute_now, gather_now))
    def _():
        x = load_rows()
        issue()
        compute(x)

    @pl.when(jnp.logical_and(compute_now, jnp.logical_not(gather_now)))
    def _():
        compute(load_rows())

    @pl.when(jnp.logical_and(jnp.logical_not(compute_now), gather_now))
    def _():
        issue()

    @pl.when(jnp.logical_not(compute_now))
    def _():
        o_ref[...] = jnp.zeros(o_ref.shape, f32)


def _gmm_call(layer, blk_e, blk_valid, row_tok, ht, w_gu, b_gu, w_down, b_down, n_blocks):
    depth, e, d, f2 = w_gu.shape
    d_ff = w_down.shape[2]
    rows = EXPERT_ROWS
    prev = lambda j: jnp.maximum(j - 1, 0)
    grid_spec = pltpu.PrefetchScalarGridSpec(
        num_scalar_prefetch=2, grid=(n_blocks + 1,),
        in_specs=[pl.BlockSpec((rows,), lambda j, be, bv: (jnp.minimum(j, n_blocks - 1),), memory_space=pltpu.SMEM),
                  pl.BlockSpec(memory_space=pl.ANY),
                  pl.BlockSpec((None, None, d, f2), lambda j, be, bv: (layer, be[prev(j)], 0, 0)),
                  pl.BlockSpec((None, None, 1, f2), lambda j, be, bv: (layer, be[prev(j)], 0, 0)),
                  pl.BlockSpec((None, None, d_ff, d), lambda j, be, bv: (layer, be[prev(j)], 0, 0)),
                  pl.BlockSpec((None, None, 1, d), lambda j, be, bv: (layer, be[prev(j)], 0, 0))],
        out_specs=pl.BlockSpec((rows * SUBLANES, LANES), lambda j, be, bv: (prev(j), 0)),
        scratch_shapes=[pltpu.VMEM((2, rows * SUBLANES, LANES), f32),
                        pltpu.VMEM((d, f2), bf16), pltpu.VMEM((d_ff, d), bf16),
                        pltpu.SemaphoreType.DMA((2,))])
    return pl.pallas_call(
        functools.partial(_gmm_kernel, n_blocks=n_blocks), grid_spec=grid_spec,
        out_shape=jax.ShapeDtypeStruct((n_blocks * rows * SUBLANES, LANES), f32),
        compiler_params=_params("arbitrary"), name="moe_experts",
    )(blk_e, blk_valid, row_tok, ht, w_gu, b_gu.reshape(depth, e, 1, f2), w_down, b_down.reshape(depth, e, 1, d))


def _combine_kernel(dest_ref, ys_hbm, gate_ref, x_ref, g2_ref, fn_ref, o_ref, ybuf, sem, *, final):
    tc = x_ref.shape[0]
    n_rows = TOP_K * tc

    def issue(q, carry):
        for i in range(ISSUE_UNROLL):
            r = q * ISSUE_UNROLL + i
            dst = dest_ref[r]
            pltpu.make_async_copy(
                ys_hbm.at[pl.ds(pl.multiple_of(dst * SUBLANES, SUBLANES), SUBLANES), :],
                ybuf.at[pl.ds(pl.multiple_of(r * SUBLANES, SUBLANES), SUBLANES), :], sem).start()
        return carry
    lax.fori_loop(0, n_rows // ISSUE_UNROLL, issue, 0)
    pltpu.make_async_copy(ys_hbm.at[pl.ds(0, n_rows * SUBLANES), :], ybuf, sem).wait()

    gates = gate_ref[...]
    pieces = []
    for s in range(SUBLANES):
        acc = None
        for k in range(TOP_K):
            term = gates[:, k:k + 1] * ybuf[pl.ds(k * tc * SUBLANES + s, tc, stride=SUBLANES), :]
            acc = term if acc is None else acc + term
        sl = slice(s * LANES, (s + 1) * LANES)
        pieces.append(x_ref[:, sl] + g2_ref[:, sl] * acc)
    if final:
        y = jnp.concatenate(pieces, axis=-1)
        ms = jnp.mean(y * y, axis=-1, keepdims=True)
        o_ref[...] = (y * lax.rsqrt(ms + NORM_EPS)) * fn_ref[...]
    else:
        for s in range(SUBLANES):
            o_ref[:, s * LANES:(s + 1) * LANES] = pieces[s]


def _combine_call(dest_tiles, ys, gates_t, tokens, mod, row, final_gain, tile, final):
    n, d = tokens.shape
    return pl.pallas_call(
        functools.partial(_combine_kernel, final=final), grid=(n // tile,),
        in_specs=[pl.BlockSpec((TOP_K * tile,), lambda i: (i,), memory_space=pltpu.SMEM),
                  pl.BlockSpec(memory_space=pl.ANY),
                  pl.BlockSpec((tile, TOP_K), lambda i: (i, 0)),
                  pl.BlockSpec((tile, d), lambda i: (i, 0)),
                  _mod_spec(row, 5, d),
                  pl.BlockSpec((1, d), lambda i: (0, 0))],
        out_specs=pl.BlockSpec((tile, d), lambda i: (i, 0)),
        out_shape=jax.ShapeDtypeStruct((n, d), f32),
        scratch_shapes=[pltpu.VMEM((TOP_K * tile * SUBLANES, LANES), f32), pltpu.SemaphoreType.DMA],
        compiler_params=_params("arbitrary"), name="moe_combine",
    )(dest_tiles, ys, gates_t, tokens, mod, final_gain)


def _moe_layer(layer, tokens, gain, mod, n_lat, seq, ctx_row, w_router, b_router, w_gu, b_gu, w_down, b_down,
               final_gain, final):
    n, d = tokens.shape
    rt = _pick_tile(ROUTE_TILE, n_lat, n - n_lat if n > n_lat else n_lat, seq)
    ct = _pick_tile(COMBINE_TILE, n_lat, n - n_lat if n > n_lat else n_lat, seq)
    row_r = _row_map(n_lat // rt, seq // rt, ctx_row)
    row_c = _row_map(n_lat // ct, seq // ct, ctx_row)

    ht, eidx, gate, rank, cnt = _route_call(tokens, gain, mod, row_r, w_router[layer].T, b_router[layer], rt)

    rows = EXPERT_ROWS
    counts = cnt[:, 0].astype(i32)
    padded = (counts + rows - 1) // rows * rows
    pad_end = jnp.cumsum(padded)
    pad_start = pad_end - padded
    experts = jnp.arange(N_EXPERTS, dtype=i32)
    start_of = jnp.sum(jnp.where(eidx[None] == experts[:, None, None], pad_start[:, None, None], 0), axis=0)
    dest = start_of + rank
    n_blocks = (n * TOP_K + N_EXPERTS * (rows - 1) + rows - 1) // rows
    blk_start = jnp.arange(n_blocks, dtype=i32) * rows
    blk_e = jnp.minimum(jnp.sum(pad_end[None, :] <= blk_start[:, None], axis=1), N_EXPERTS - 1).astype(i32)
    blk_hot = blk_e[:, None] == experts[None, :]
    blk_cnt = jnp.sum(jnp.where(blk_hot, counts[None, :], 0), axis=1)
    blk_base = jnp.sum(jnp.where(blk_hot, pad_start[None, :], 0), axis=1)
    blk_valid = jnp.clip(blk_cnt - (blk_start - blk_base), 0, rows).astype(i32)
    tok_ids = jnp.broadcast_to(jnp.arange(n, dtype=i32)[None, :], (TOP_K, n))
    row_tok = jnp.zeros((n_blocks * rows,), i32).at[dest.reshape(-1)].set(
        tok_ids.reshape(-1), unique_indices=True, mode='promise_in_bounds')

    ys = _gmm_call(layer, blk_e, blk_valid, row_tok, ht, w_gu, b_gu, w_down, b_down, n_blocks)

    dest_tiles = dest.reshape(TOP_K, n // ct, ct).transpose(1, 0, 2).reshape(-1)
    return _combine_call(dest_tiles, ys, gate.T, tokens, mod, row_c, final_gain, ct, final)


def _lru_in_kernel(x_ref, gain_ref, sc_ref, sh_ref, w_ref, gg_ref, u_ref):
    d_rnn = gg_ref.shape[1]
    h = _norm_mod(x_ref[...], gain_ref[...], sc_ref[...], sh_ref[...])
    gu = jnp.dot(h.astype(bf16), w_ref[...], preferred_element_type=f32)
    gate = gu[:, :d_rnn]
    gg_ref[...] = 0.5 * gate * (1.0 + jnp.tanh(GELU_C * (gate + 0.044715 * (gate * gate * gate))))
    u_ref[...] = gu[:, d_rnn:]


def _lru_in_call(tokens, gain, mod, row, w_in_bf, tile):
    n, d = tokens.shape
    d2 = w_in_bf.shape[1]
    d_rnn = d2 // 2
    return pl.pallas_call(
        _lru_in_kernel, grid=(n // tile,),
        in_specs=[pl.BlockSpec((tile, d), lambda i: (i, 0)),
                  pl.BlockSpec((1, d), lambda i: (0, 0)),
                  _mod_spec(row, 1, d), _mod_spec(row, 0, d),
                  pl.BlockSpec((d, d2), lambda i: (0, 0))],
        out_specs=[pl.BlockSpec((tile, d_rnn), lambda i: (i, 0)), pl.BlockSpec((tile, d_rnn), lambda i: (i, 0))],
        out_shape=[jax.ShapeDtypeStruct((n, d_rnn), f32), jax.ShapeDtypeStruct((n, d_rnn), f32)],
        compiler_params=_params("arbitrary"), name="lru_in",
    )(tokens, gain, mod, mod, w_in_bf)


def _conv(u, cw_ref, cb_ref):
    n = u.shape[0]
    t = lax.broadcasted_iota(i32, u.shape, 0)
    y = None
    for k in range(CONV_WIDTH):
        j = CONV_LEFT - k
        term = (u if j == 0 else _shift(u, j, 1, t, n)) * cw_ref[k:k + 1, :]
        y = term if y is None else y + term
    return y + cb_ref[...]


def _lru_kernel(ul_ref, uc_ref, gg_ref, cw_ref, cb_ref, wr_ref, br_ref, wi_ref, bi_ref, lam_ref, o_ref,
                ucl, ucc, hf, a_s, b_s, *, chunk):
    seq = ul_ref.shape[0]
    ctx_len = uc_ref.shape[0]
    width = ul_ref.shape[1]
    sub = lax.broadcasted_iota(i32, (SUBLANES, width), 0)

    ucl[...] = _conv(ul_ref[...], cw_ref, cb_ref)
    ucc[...] = _conv(uc_ref[...], cw_ref, cb_ref)

    def run_direction(src, length, d, reverse, carry, store):
        w_r = wr_ref[d].astype(bf16)
        w_i = wi_ref[d].astype(bf16)
        b_r, b_i = br_ref[d], bi_ref[d]
        lam = lam_ref[d]
        softplus_neg = jnp.maximum(-lam, 0.0) + jnp.log1p(jnp.exp(-jnp.abs(lam)))
        n_chunks = length // chunk
        n_groups = chunk // SUBLANES

        def chunk_body(ci, carry):
            c = (n_chunks - 1 - ci) if reverse else ci
            base = pl.multiple_of(c * chunk, chunk)
            u = src[pl.ds(base, chunk), :]
            ub = u.astype(bf16)
            r = jax.nn.sigmoid(jnp.dot(ub, w_r, preferred_element_type=f32) + b_r)
            gi = jax.nn.sigmoid(jnp.dot(ub, w_i, preferred_element_type=f32) + b_i)
            log_a = (-LRU_C * r) * softplus_neg
            a = jnp.exp(log_a)
            mult = jnp.sqrt(-jnp.tanh(log_a) * (a * a + 1.0))
            a_s[...] = a
            b_s[...] = mult * (gi * u)

            def group_body(gidx, h8):
                g = (n_groups - 1 - gidx) if reverse else gidx
                off = pl.multiple_of(g * SUBLANES, SUBLANES)
                av = a_s[pl.ds(off, SUBLANES), :]
                bv = b_s[pl.ds(off, SUBLANES), :]
                for s in (1, 2, 4):
                    if reverse:
                        ok = sub < SUBLANES - s
                        a_sh = jnp.where(ok, pltpu.roll(av, SUBLANES - s, 0), 1.0)
                        b_sh = jnp.where(ok, pltpu.roll(bv, SUBLANES - s, 0), 0.0)
                    else:
                        ok = sub >= s
                        a_sh = jnp.where(ok, pltpu.roll(av, s, 0), 1.0)
                        b_sh = jnp.where(ok, pltpu.roll(bv, s, 0), 0.0)
                    bv = bv + av * b_sh
                    av = av * a_sh
                h = av * h8 + bv
                if store is not None:
                    store(pl.multiple_of(base + off, SUBLANES), h)
                edge = h[0:1, :] if reverse else h[SUBLANES - 1:SUBLANES, :]
                return jnp.broadcast_to(edge, h.shape)

            return lax.fori_loop(0, n_groups, group_body, carry)

        return lax.fori_loop(0, n_chunks, chunk_body, carry)

    zero = jnp.zeros((SUBLANES, width), f32)

    def store_fwd(row, h):
        hf[pl.ds(row, SUBLANES), :] = h

    def store_out(row, h):
        o_ref[pl.ds(row, SUBLANES), :] = gg_ref[pl.ds(row, SUBLANES), :] * (hf[pl.ds(row, SUBLANES), :] + h)

    h0 = run_direction(ucc, ctx_len, 0, False, zero, None)
    run_direction(ucl, seq, 0, False, h0, store_fwd)
    h0 = run_direction(ucc, ctx_len, 1, True, zero, None)
    run_direction(ucl, seq, 1, True, h0, store_out)


def _lru_call(gg, u, conv_w, conv_b, w_r, b_r, w_i, b_i, lam, batch, seq, ctx_len):
    n_lat = batch * seq
    d_rnn = u.shape[1]
    wb = d_rnn // N_LRU_BLOCKS
    chunk = _pick_tile(SCAN_CHUNK, seq, ctx_len)
    ctx_blk0 = n_lat // ctx_len
    vec = lambda a: a.reshape(2, 1, d_rnn)
    vspec = pl.BlockSpec((2, 1, wb), lambda b, n: (0, 0, n))
    wspec = pl.BlockSpec((2, None, wb, wb), lambda b, n: (0, n, 0, 0))
    return pl.pallas_call(
        functools.partial(_lru_kernel, chunk=chunk), grid=(batch, N_LRU_BLOCKS),
        in_specs=[pl.BlockSpec((seq, wb), lambda b, n: (b, n)),
                  pl.BlockSpec((ctx_len, wb), lambda b, n: (ctx_blk0 + b, n)),
                  pl.BlockSpec((seq, wb), lambda b, n: (b, n)),
                  pl.BlockSpec((CONV_WIDTH, wb), lambda b, n: (0, n)),
                  pl.BlockSpec((1, wb), lambda b, n: (0, n)),
                  wspec, vspec, wspec, vspec, vspec],
        out_specs=pl.BlockSpec((seq, wb), lambda b, n: (b, n)),
        out_shape=jax.ShapeDtypeStruct((n_lat, d_rnn), f32),
        scratch_shapes=[pltpu.VMEM((seq, wb), f32), pltpu.VMEM((ctx_len, wb), f32), pltpu.VMEM((seq, wb), f32),
                        pltpu.VMEM((chunk, wb), f32), pltpu.VMEM((chunk, wb), f32)],
        compiler_params=_params("arbitrary", "arbitrary"), name="lru_scan",
    )(u, u, gg, conv_w, conv_b.reshape(1, d_rnn), w_r, vec(b_r), w_i, vec(b_i), vec(lam))


def _lru_out_kernel(z_ref, x_ref, w_ref, g1_ref, o_ref):
    y = jnp.dot(z_ref[...].astype(bf16), w_ref[...], preferred_element_type=f32)
    o_ref[...] = x_ref[...] + g1_ref[...] * y


def _lru_out_call(z, tokens, w_out_bf, mod, row, tile):
    n, d = z.shape[0], tokens.shape[1]
    d_rnn = z.shape[1]
    return pl.pallas_call(
        _lru_out_kernel, grid=(n // tile,),
        in_specs=[pl.BlockSpec((tile, d_rnn), lambda i: (i, 0)),
                  pl.BlockSpec((tile, d), lambda i: (i, 0)),
                  pl.BlockSpec((d_rnn, d), lambda i: (0, 0)),
                  _mod_spec(row, 2, d)],
        out_specs=pl.BlockSpec((tile, d), lambda i: (i, 0)),
        out_shape=jax.ShapeDtypeStruct((n, d), f32),
        compiler_params=_params("arbitrary"), name="lru_out",
    )(z, tokens, w_out_bf, mod)


def kernel(x, c, ctx, c_ctx, ada_w, ada_b, norm_mix, norm_ffn, pool_w, pool_scale, lru_w_in, lru_conv_w, lru_conv_b, lru_w_r, lru_b_r, lru_w_i, lru_b_i, lru_lam, lru_w_out, router_w, router_b, exp_w_gu, exp_b_gu, exp_w_down, exp_b_down, final_norm):
    batch, seq, d = x.shape
    ctx_len = ctx.shape[1]
    assert d == SUBLANES * LANES and seq % GRID_W == 0
    n_lat, n_ctx = batch * seq, batch * ctx_len
    ctx_row = batch

    mod_rows = -(-(batch + 1) // SUBLANES) * SUBLANES
    cc = jnp.zeros((mod_rows, d), f32).at[:batch].set(c).at[batch].set(c_ctx)
    mod_all = _ada_call(cc, ada_w, ada_b)
    mods = [mod_all[l].reshape(mod_rows, 1, 6 * d) for l in range(ada_w.shape[0])]

    lat = x.reshape(n_lat, d)
    cx = ctx.reshape(n_ctx, d)
    tile = _pick_tile(ROUTE_TILE, seq, n_ctx)
    row_lat = _row_map(n_lat // tile, seq // tile, ctx_row)
    gain = lambda g: g.reshape(1, d)

    hl = _norm_call(lat, gain(norm_mix[0]), mods[0], 0, 1, row_lat, tile)
    lat = _pool_lat_call(hl, lat, pool_w[0], gain(pool_scale[0]), mods[0], batch, seq)
    cx = _pool_ctx_call(cx, gain(norm_mix[0]), pool_w[0], gain(pool_scale[0]), mods[0], ctx_row, batch, ctx_len)
    tokens = jnp.concatenate([lat, cx], axis=0)
    tokens = _moe_layer(0, tokens, gain(norm_ffn[0]), mods[0], n_lat, seq, ctx_row, router_w, router_b,
                        exp_w_gu, exp_b_gu, exp_w_down, exp_b_down, gain(final_norm), False)

    row_all = _row_map(n_lat // tile, seq // tile, ctx_row)
    gg, u = _lru_in_call(tokens, gain(norm_mix[1]), mods[1], row_all, lru_w_in[0].astype(bf16), tile)
    z = _lru_call(gg, u, lru_conv_w[0], lru_conv_b[0], lru_w_r[0], lru_b_r[0], lru_w_i[0], lru_b_i[0],
                  lru_lam[0], batch, seq, ctx_len)
    lat = _lru_out_call(z, tokens, lru_w_out[0].astype(bf16), mods[1], row_lat, tile)
    out = _moe_layer(1, lat, gain(norm_ffn[1]), mods[1], n_lat, seq, ctx_row, router_w, router_b,
                     exp_w_gu, exp_b_gu, exp_w_down, exp_b_down, gain(final_norm), True)
    return out.reshape(batch, seq, d)
```

```python
import functools

import numpy as np
import jax
import jax.numpy as jnp
from jax import lax
from jax.experimental import pallas as pl
from jax.experimental.pallas import tpu as pltpu

f32 = jnp.float32
bf16 = jnp.bfloat16
i32 = jnp.int32
HIGHEST = lax.Precision.HIGHEST

LANES = 128
SUBLANES = 8
VMEM_LIMIT = 56 * 1024 * 1024

GRID_W = 64
GRID_SHIFT = 6
POOL_WINDOWS = (2, 4, 8, 16)
N_POOL_GROUPS = 4
N_LRU_BLOCKS = 4
CONV_LEFT = 2
CONV_WIDTH = 4
LRU_C = 8.0
N_EXPERTS = 32
TOP_K = 4
SWIGLU_LIMIT = 7.0
SWIGLU_ALPHA = 1.702
NORM_EPS = 1e-6
GELU_C = 0.7978845608028654

EXPERT_ROWS = 256
ROUTE_TILE = 512
SCAN_CHUNK = 256


def _params(*sem):
    return pltpu.CompilerParams(dimension_semantics=sem, vmem_limit_bytes=VMEM_LIMIT)


def _pick_tile(pref, *sizes):
    t = pref
    while any(s % t for s in sizes):
        t //= 2
    assert t >= SUBLANES
    return t


def _norm_mod(x, gain, scale, shift):
    ms = jnp.mean(x * x, axis=-1, keepdims=True)
    y = x * lax.rsqrt(ms + NORM_EPS)
    return (y * gain) * (1.0 + scale) + shift


def _row_map(n_lat_tiles, tiles_per_batch, ctx_row):
    def row(i):
        return jnp.where(i < n_lat_tiles, i // tiles_per_batch, ctx_row)
    return row


def _mod_spec(row, chunk, width):
    return pl.BlockSpec((None, 1, width), lambda i, *_: (row(i), 0, chunk))


def _ada_kernel(c_ref, w_ref, b_ref, o_ref):
    c = c_ref[...]
    s = c * jax.nn.sigmoid(c)
    o_ref[...] = jnp.dot(s, w_ref[...], preferred_element_type=f32, precision=HIGHEST) + b_ref[...]


def _ada_call(cc, ada_w, ada_b):
    depth, d, d6 = ada_w.shape
    rows = cc.shape[0]
    bn = 1536
    return pl.pallas_call(
        _ada_kernel, grid=(depth, d6 // bn),
        in_specs=[pl.BlockSpec((rows, d), lambda l, j: (0, 0)),
                  pl.BlockSpec((None, d, bn), lambda l, j: (l, 0, j)),
                  pl.BlockSpec((None, 1, bn), lambda l, j: (l, 0, j))],
        out_specs=pl.BlockSpec((None, rows, bn), lambda l, j: (l, 0, j)),
        out_shape=jax.ShapeDtypeStruct((depth, rows, d6), f32),
        compiler_params=_params("arbitrary", "arbitrary"), name="ada_mod",
    )(cc, ada_w, ada_b.reshape(depth, 1, d6))


def _norm_kernel(x_ref, g_ref, sc_ref, sh_ref, o_ref):
    o_ref[...] = _norm_mod(x_ref[...], g_ref[...], sc_ref[...], sh_ref[...])


def _norm_call(tokens, gain, mod, sh_chunk, sc_chunk, row, tile):
    n, d = tokens.shape
    return pl.pallas_call(
        _norm_kernel, grid=(n // tile,),
        in_specs=[pl.BlockSpec((tile, d), lambda i: (i, 0)),
                  pl.BlockSpec((1, d), lambda i: (0, 0)),
                  _mod_spec(row, sc_chunk, d), _mod_spec(row, sh_chunk, d)],
        out_specs=pl.BlockSpec((tile, d), lambda i: (i, 0)),
        out_shape=jax.ShapeDtypeStruct((n, d), f32),
        compiler_params=_params("arbitrary"), name="norm_mod",
    )(tokens, gain, mod, mod)


def _shift(x, j, unit, pos, limit):
    n = x.shape[0]
    rolled = pltpu.roll(x, (j * unit) % n, 0)
    ok = (pos >= j) if j > 0 else (pos < limit + j)
    return jnp.where(ok, rolled, 0.0)


def _window_sum(x, w, unit, pos, limit):
    m = w // 2
    trail, lead, s = x, x, 1
    while s < m:
        trail = trail + _shift(trail, s, unit, pos, limit)
        lead = lead + _shift(lead, -s, unit, pos, limit)
        s *= 2
    return _shift(trail, 1, unit, pos, limit) + lead


def _window_count(w, pos, limit):
    m = w // 2
    return jnp.minimum(pos + m, limit) - jnp.maximum(pos - m, 0)


def _pool_grid(h, w):
    t = lax.broadcasted_iota(i32, h.shape, 0)
    col, row = t & (GRID_W - 1), t >> GRID_SHIFT
    rows = h.shape[0] // GRID_W
    total = _window_sum(_window_sum(h, w, 1, col, GRID_W), w, GRID_W, row, rows)
    count = (_window_count(w, row, rows) * _window_count(w, col, GRID_W)).astype(f32)
    return total / count


def _pool_seq(h, w):
    t = lax.broadcasted_iota(i32, h.shape, 0)
    n = h.shape[0]
    return _window_sum(h, w, 1, t, n) / _window_count(w, t, n).astype(f32)


def _pool_lat_kernel(h_ref, x_ref, w_ref, ps_ref, g1_ref, o_ref):
    grp = pl.program_id(1)
    for gi, w in enumerate(POOL_WINDOWS):
        @pl.when(grp == gi)
        def _(w=w):
            h = h_ref[...]
            d = _pool_grid(h, w) - h
            y = jnp.dot(d.astype(bf16), w_ref[...].astype(bf16), preferred_element_type=f32)
            o_ref[...] = x_ref[...] + g1_ref[...] * (y * ps_ref[...])


def _pool_lat_call(h, x, pool_w, pool_scale, mod, batch, seq):
    n, d = x.shape
    cg = d // N_POOL_GROUPS
    g1_chunk0 = 2 * N_POOL_GROUPS
    return pl.pallas_call(
        _pool_lat_kernel, grid=(batch, N_POOL_GROUPS),
        in_specs=[pl.BlockSpec((seq, cg), lambda b, g: (b, g)),
                  pl.BlockSpec((seq, cg), lambda b, g: (b, g)),
                  pl.BlockSpec((None, cg, cg), lambda b, g: (g, 0, 0)),
                  pl.BlockSpec((1, cg), lambda b, g: (0, g)),
                  pl.BlockSpec((None, 1, cg), lambda b, g: (b, 0, g1_chunk0 + g))],
        out_specs=pl.BlockSpec((seq, cg), lambda b, g: (b, g)),
        out_shape=jax.ShapeDtypeStruct((n, d), f32),
        compiler_params=_params("arbitrary", "arbitrary"), name="pool_lat",
    )(h, x, pool_w, pool_scale, mod)


def _pool_ctx_kernel(x_ref, gain_ref, sc_ref, sh_ref, w_ref, ps_ref, g1_ref, o_ref):
    x = x_ref[...]
    h = _norm_mod(x, gain_ref[...], sc_ref[...], sh_ref[...])
    cg = x.shape[1] // N_POOL_GROUPS
    for gi, w in enumerate(POOL_WINDOWS):
        sl = slice(gi * cg, (gi + 1) * cg)
        hg = h[:, sl]
        d = _pool_seq(hg, w) - hg
        y = jnp.dot(d.astype(bf16), w_ref[gi].astype(bf16), preferred_element_type=f32)
        o_ref[:, sl] = x[:, sl] + g1_ref[:, sl] * (y * ps_ref[:, sl])


def _pool_ctx_call(cx, gain, pool_w, pool_scale, mod, ctx_row, batch, ctx_len):
    n, d = cx.shape
    cg = d // N_POOL_GROUPS
    row = lambda i: ctx_row
    return pl.pallas_call(
        _pool_ctx_kernel, grid=(batch,),
        in_specs=[pl.BlockSpec((ctx_len, d), lambda b: (b, 0)),
                  pl.BlockSpec((1, d), lambda b: (0, 0)),
                  _mod_spec(row, 1, d), _mod_spec(row, 0, d),
                  pl.BlockSpec((N_POOL_GROUPS, cg, cg), lambda b: (0, 0, 0)),
                  pl.BlockSpec((1, d), lambda b: (0, 0)),
                  _mod_spec(row, 2, d)],
        out_specs=pl.BlockSpec((ctx_len, d), lambda b: (b, 0)),
        out_shape=jax.ShapeDtypeStruct((n, d), f32),
        compiler_params=_params("arbitrary"), name="pool_ctx",
    )(cx, gain, mod, mod, pool_w, pool_scale, mod)


def _route_kernel(x_ref, gain_ref, sc_ref, sh_ref, wr_ref, rb_ref, loc_ref, gate_ref, cnt_ref):
    tt = x_ref.shape[0]
    h = _norm_mod(x_ref[...], gain_ref[...], sc_ref[...], sh_ref[...])

    logits = lax.dot_general(wr_ref[...], h, (((1,), (1,)), ((), ())),
                             precision=HIGHEST, preferred_element_type=f32) + rb_ref[...]
    e_iota = lax.broadcasted_iota(i32, logits.shape, 0)
    vals, onehots = [], []
    work = logits
    for k in range(TOP_K):
        m = jnp.max(work, axis=0, keepdims=True)
        idx = jnp.min(jnp.where(work == m, e_iota, N_EXPERTS), axis=0, keepdims=True)
        hit = e_iota == idx
        vals.append(m)
        onehots.append(jnp.where(hit, 1.0, 0.0))
        work = jnp.where(hit, -jnp.inf, work)

    ex = [jnp.exp(v - vals[0]) for v in vals]
    den = ex[0] + ex[1] + ex[2] + ex[3]
    for k in range(TOP_K):
        gate_ref[k:k + 1, :] = ex[k] / den

    tri = jnp.where(lax.broadcasted_iota(i32, (tt, tt), 0) <= lax.broadcasted_iota(i32, (tt, tt), 1), 1.0, 0.0)
    planes = jnp.concatenate(onehots, axis=0).astype(bf16)
    incl = jnp.dot(planes, tri.astype(bf16), preferred_element_type=f32)
    totals = [incl[k * N_EXPERTS:(k + 1) * N_EXPERTS, tt - 1:tt] for k in range(TOP_K)]
    count = jnp.broadcast_to(totals[0] + totals[1] + totals[2] + totals[3], (N_EXPERTS, LANES))
    padded = (((count.astype(i32) + (SUBLANES - 1)) // SUBLANES) * SUBLANES).astype(f32)
    below = jnp.where(lax.broadcasted_iota(i32, (N_EXPERTS, N_EXPERTS), 1)
                      < lax.broadcasted_iota(i32, (N_EXPERTS, N_EXPERTS), 0), 1.0, 0.0)
    offset = jnp.dot(below.astype(bf16), padded.astype(bf16), preferred_element_type=f32)[:, 0:1]
    for k in range(TOP_K):
        inc_k = incl[k * N_EXPERTS:(k + 1) * N_EXPERTS]
        before = inc_k - onehots[k] + offset
        loc_ref[k:k + 1, :] = jnp.sum(onehots[k] * before, axis=0, keepdims=True).astype(i32)
        offset = offset + totals[k]
    cnt_ref[...] = count


def _route_call(tokens, gain, mod, row, w_router_t, b_router, tile):
    n, d = tokens.shape
    e = w_router_t.shape[0]
    return pl.pallas_call(
        _route_kernel, grid=(n // tile,),
        in_specs=[pl.BlockSpec((tile, d), lambda i: (i, 0)),
                  pl.BlockSpec((1, d), lambda i: (0, 0)),
                  _mod_spec(row, 4, d), _mod_spec(row, 3, d),
                  pl.BlockSpec((e, d), lambda i: (0, 0)),
                  pl.BlockSpec((e, 1), lambda i: (0, 0))],
        out_specs=[pl.BlockSpec((TOP_K, tile), lambda i: (0, i)),
                   pl.BlockSpec((TOP_K, tile), lambda i: (0, i)),
                   pl.BlockSpec((None, e, LANES), lambda i: (i, 0, 0))],
        out_shape=[jax.ShapeDtypeStruct((TOP_K, n), i32),
                   jax.ShapeDtypeStruct((TOP_K, n), f32),
                   jax.ShapeDtypeStruct((n // tile, e, LANES), f32)],
        compiler_params=_params("arbitrary"), name="moe_route",
    )(tokens, gain, mod, mod, w_router_t, b_router.reshape(e, 1))


def _piece_wait(n, src_piece, dst_piece, sem):
    def body(q, carry):
        pltpu.make_async_copy(src_piece, dst_piece, sem).wait()
        return carry
    lax.fori_loop(0, n, body, 0)


def _dispatch_kernel(go_ref, lb_ref, p8_ref, np_ref, ts_ref, tn_ref, nu_ref,
                     x_ref, gain_ref, sc_ref, sh_ref, loc_ref, xs_hbm, xloc, zbuf, sem, *, n_tiles, n_blocks):
    i = pl.program_id(0)
    tt = x_ref.shape[0]
    rl = xloc.shape[1]
    slot = i % 2
    h = _norm_mod(x_ref[...], gain_ref[...], sc_ref[...], sh_ref[...]).astype(bf16)
    r_iota = lax.broadcasted_iota(i32, (rl, tt), 0)
    sel = None
    for k in range(TOP_K):
        m = jnp.where(r_iota == loc_ref[k:k + 1, :], 1.0, 0.0)
        sel = m if sel is None else sel + m
    xloc[slot] = jnp.dot(sel.astype(bf16), h, preferred_element_type=f32)

    piece_src = xloc.at[0, pl.ds(0, SUBLANES), :]
    piece_dst = xs_hbm.at[pl.ds(0, SUBLANES), :]

    @pl.when(i > 0)
    def _():
        _piece_wait(np_ref[jnp.maximum(i - 1, 0)], piece_src, piece_dst, sem.at[1 - slot])

    def per_expert(e, carry):
        idx = i * N_EXPERTS + e
        src0, dst0 = lb_ref[idx], go_ref[idx]

        def piece(q, c2):
            s = pl.multiple_of(src0 + q * SUBLANES, SUBLANES)
            t = pl.multiple_of(dst0 + q * SUBLANES, SUBLANES)
            pltpu.make_async_copy(xloc.at[slot, pl.ds(s, SUBLANES), :], xs_hbm.at[pl.ds(t, SUBLANES), :],
                                  sem.at[slot]).start()
            return c2
        lax.fori_loop(0, p8_ref[idx] // SUBLANES, piece, 0)
        return carry
    lax.fori_loop(0, N_EXPERTS, per_expert, 0)

    @pl.when(i == n_tiles - 1)
    def _():
        _piece_wait(np_ref[i], piece_src, piece_dst, sem.at[slot])
        zbuf[...] = jnp.zeros(zbuf.shape, f32)
        zpiece = zbuf.at[pl.ds(0, SUBLANES), :]

        def tail(e, carry):
            t0 = ts_ref[e]

            def piece(q, c2):
                t = pl.multiple_of(t0 + q * SUBLANES, SUBLANES)
                pltpu.make_async_copy(zpiece, xs_hbm.at[pl.ds(t, SUBLANES), :], sem.at[slot]).start()
                return c2
            lax.fori_loop(0, tn_ref[e], piece, 0)
            return carry
        lax.fori_loop(0, N_EXPERTS, tail, 0)

        def tail_wait(e, carry):
            _piece_wait(tn_ref[e], zpiece, piece_dst, sem.at[slot])
            return carry
        lax.fori_loop(0, N_EXPERTS, tail_wait, 0)

        def spare(b, carry):
            t = pl.multiple_of(b * EXPERT_ROWS, EXPERT_ROWS)
            pltpu.make_async_copy(zbuf, xs_hbm.at[pl.ds(t, EXPERT_ROWS), :], sem.at[slot]).start()
            return carry
        lax.fori_loop(nu_ref[0], n_blocks, spare, 0)

        def spare_wait(b, carry):
            pltpu.make_async_copy(zbuf, xs_hbm.at[pl.ds(0, EXPERT_ROWS), :], sem.at[slot]).wait()
            return carry
        lax.fori_loop(nu_ref[0], n_blocks, spare_wait, 0)


def _dispatch_call(tables, tokens, gain, mod, row, loc, n_rows, tile):
    n, d = tokens.shape
    n_tiles = n // tile
    rl = TOP_K * tile + N_EXPERTS * SUBLANES
    spec = lambda shape, fn: pl.BlockSpec(shape, lambda i, *_: fn(i))
    grid_spec = pltpu.PrefetchScalarGridSpec(
        num_scalar_prefetch=7, grid=(n_tiles,),
        in_specs=[spec((tile, d), lambda i: (i, 0)),
                  spec((1, d), lambda i: (0, 0)),
                  _mod_spec(row, 4, d), _mod_spec(row, 3, d),
                  spec((TOP_K, tile), lambda i: (0, i))],
        out_specs=pl.BlockSpec(memory_space=pl.ANY),
        scratch_shapes=[pltpu.VMEM((2, rl, d), f32), pltpu.VMEM((EXPERT_ROWS, d), f32),
                        pltpu.SemaphoreType.DMA((2,))])
    return pl.pallas_call(
        functools.partial(_dispatch_kernel, n_tiles=n_tiles, n_blocks=n_rows // EXPERT_ROWS), grid_spec=grid_spec,
        out_shape=jax.ShapeDtypeStruct((n_rows, d), f32),
        compiler_params=_params("arbitrary"), name="moe_dispatch",
    )(*tables, tokens, gain, mod, mod, loc)


def _gmm_kernel(be_ref, bv_ref, lu_ref, x_ref, wgu_ref, bgu_ref, wd_ref, bd_ref, o_ref, wgu_bf, wd_bf):
    j = pl.program_id(0)
    d_ff = wd_ref.shape[0]
    used = bv_ref[j] > 0

    @pl.when(used)
    def _():
        new_expert = jnp.logical_or(j == 0, be_ref[j] != be_ref[jnp.maximum(j - 1, 0)])

        @pl.when(new_expert)
        def _():
            wgu_bf[...] = wgu_ref[...].astype(bf16)
            wd_bf[...] = wd_ref[...].astype(bf16)

        x = x_ref[...].astype(bf16)
        gu = jnp.dot(x, wgu_bf[...], preferred_element_type=f32) + bgu_ref[...]
        g = jnp.minimum(gu[:, :d_ff], SWIGLU_LIMIT)
        u = jnp.clip(gu[:, d_ff:], -SWIGLU_LIMIT, SWIGLU_LIMIT)
        act = (u + 1.0) * g * jax.nn.sigmoid(SWIGLU_ALPHA * g)
        o_ref[...] = jnp.dot(act.astype(bf16), wd_bf[...], preferred_element_type=f32) + bd_ref[...]

    @pl.when(jnp.logical_not(used))
    def _():
        o_ref[...] = jnp.zeros(o_ref.shape, f32)


def _gmm_call(layer, blk_e, blk_valid, last_used, xs, w_gu, b_gu, w_down, b_down):
    depth, e, d, f2 = w_gu.shape
    d_ff = w_down.shape[2]
    rows = EXPERT_ROWS
    n_blocks = xs.shape[0] // rows
    grid_spec = pltpu.PrefetchScalarGridSpec(
        num_scalar_prefetch=3, grid=(n_blocks,),
        in_specs=[pl.BlockSpec((rows, d), lambda j, be, bv, lu: (jnp.minimum(j, lu[0]), 0)),
                  pl.BlockSpec((None, None, d, f2), lambda j, be, bv, lu: (layer, be[j], 0, 0)),
                  pl.BlockSpec((None, None, 1, f2), lambda j, be, bv, lu: (layer, be[j], 0, 0)),
                  pl.BlockSpec((None, None, d_ff, d), lambda j, be, bv, lu: (layer, be[j], 0, 0)),
                  pl.BlockSpec((None, None, 1, d), lambda j, be, bv, lu: (layer, be[j], 0, 0))],
        out_specs=pl.BlockSpec((rows, d), lambda j, be, bv, lu: (j, 0)),
        scratch_shapes=[pltpu.VMEM((d, f2), bf16), pltpu.VMEM((d_ff, d), bf16)])
    return pl.pallas_call(
        _gmm_kernel, grid_spec=grid_spec,
        out_shape=jax.ShapeDtypeStruct((n_blocks * rows, d), f32),
        compiler_params=_params("arbitrary"), name="moe_experts",
    )(blk_e, blk_valid, last_used, xs, w_gu, b_gu.reshape(depth, e, 1, f2), w_down, b_down.reshape(depth, e, 1, d))


def _combine_kernel(go_ref, lb_ref, p8_ref, np_ref,
                    ys_hbm, loc_ref, gate_ref, x_ref, g2_ref, fn_ref, o_ref, yloc, sem, *, n_tiles, final):
    i = pl.program_id(0)
    tt = x_ref.shape[0]
    rl = yloc.shape[1]
    slot = i % 2

    def fetch(tile, buf):
        def per_expert(e, carry):
            idx = tile * N_EXPERTS + e
            src0, dst0 = go_ref[idx], lb_ref[idx]

            def piece(q, c2):
                s = pl.multiple_of(src0 + q * SUBLANES, SUBLANES)
                t = pl.multiple_of(dst0 + q * SUBLANES, SUBLANES)
                pltpu.make_async_copy(ys_hbm.at[pl.ds(s, SUBLANES), :], yloc.at[buf, pl.ds(t, SUBLANES), :],
                                      sem.at[buf]).start()
                return c2
            lax.fori_loop(0, p8_ref[idx] // SUBLANES, piece, 0)
            return carry
        lax.fori_loop(0, N_EXPERTS, per_expert, 0)

    @pl.when(i == 0)
    def _():
        yloc[...] = jnp.zeros(yloc.shape, f32)
        fetch(0, 0)

    @pl.when(i + 1 < n_tiles)
    def _():
        fetch(jnp.minimum(i + 1, n_tiles - 1), 1 - slot)

    _piece_wait(np_ref[i], ys_hbm.at[pl.ds(0, SUBLANES), :], yloc.at[0, pl.ds(0, SUBLANES), :], sem.at[slot])

    r_iota = lax.broadcasted_iota(i32, (tt, rl), 1)
    weights = None
    for k in range(TOP_K):
        m = jnp.where(r_iota == loc_ref[:, k:k + 1], gate_ref[:, k:k + 1], 0.0)
        weights = m if weights is None else weights + m
    y = jnp.dot(weights.astype(bf16), yloc[slot].astype(bf16), preferred_element_type=f32)
    out = x_ref[...] + g2_ref[...] * y
    if final:
        ms = jnp.mean(out * out, axis=-1, keepdims=True)
        out = (out * lax.rsqrt(ms + NORM_EPS)) * fn_ref[...]
    o_ref[...] = out


def _combine_call(tables, ys, loc_t, gates_t, tokens, mod, row, final_gain, tile, final):
    n, d = tokens.shape
    n_tiles = n // tile
    rl = TOP_K * tile + N_EXPERTS * SUBLANES
    spec = lambda shape, fn: pl.BlockSpec(shape, lambda i, *_: fn(i))
    grid_spec = pltpu.PrefetchScalarGridSpec(
        num_scalar_prefetch=4, grid=(n_tiles,),
        in_specs=[pl.BlockSpec(memory_space=pl.ANY),
                  spec((tile, TOP_K), lambda i: (i, 0)),
                  spec((tile, TOP_K), lambda i: (i, 0)),
                  spec((tile, d), lambda i: (i, 0)),
                  _mod_spec(row, 5, d),
                  spec((1, d), lambda i: (0, 0))],
        out_specs=spec((tile, d), lambda i: (i, 0)),
        scratch_shapes=[pltpu.VMEM((2, rl, d), f32), pltpu.SemaphoreType.DMA((2,))])
    return pl.pallas_call(
        functools.partial(_combine_kernel, n_tiles=n_tiles, final=final), grid_spec=grid_spec,
        out_shape=jax.ShapeDtypeStruct((n, d), f32),
        compiler_params=_params("arbitrary"), name="moe_combine",
    )(*tables, ys, loc_t, gates_t, tokens, mod, final_gain)


def _moe_layer(layer, tokens, gain, mod, n_lat, seq, ctx_row, w_router, b_router, w_gu, b_gu, w_down, b_down,
               final_gain, final):
    n, d = tokens.shape
    tile = _pick_tile(ROUTE_TILE, n_lat, n - n_lat if n > n_lat else n_lat, seq)
    n_tiles = n // tile
    row = _row_map(n_lat // tile, seq // tile, ctx_row)

    loc, gate, cnt = _route_call(tokens, gain, mod, row, w_router[layer].T, b_router[layer], tile)

    rows = EXPERT_ROWS
    count = cnt[:, :, 0].astype(i32)
    seg = (count + SUBLANES - 1) // SUBLANES * SUBLANES
    local_base = jnp.cumsum(seg, axis=1) - seg
    pieces = jnp.sum(seg, axis=1) // SUBLANES
    expert_rows = jnp.sum(seg, axis=0)
    region = (expert_rows + rows - 1) // rows * rows
    region_end = jnp.cumsum(region)
    region_start = region_end - region
    global_off = region_start[None, :] + jnp.cumsum(seg, axis=0) - seg
    n_blocks = (n * TOP_K + n_tiles * N_EXPERTS * (SUBLANES - 1) + N_EXPERTS * (rows - 1) + rows - 1) // rows
    blk_start = jnp.arange(n_blocks, dtype=i32) * rows
    experts = jnp.arange(N_EXPERTS, dtype=i32)
    blk_e = jnp.minimum(jnp.sum(region_end[None, :] <= blk_start[:, None], axis=1), N_EXPERTS - 1).astype(i32)
    content_end = region_start + expert_rows
    blk_end = jnp.sum(jnp.where(blk_e[:, None] == experts[None, :], content_end[None, :], 0), axis=1)
    blk_valid = (blk_start < blk_end).astype(i32)
    last_used = jnp.maximum(region_end[-1:] // rows - 1, 0).astype(i32)
    tail_n = ((region - expert_rows) // SUBLANES).astype(i32)
    flat = lambda a: a.reshape(-1).astype(i32)
    tables = (flat(global_off), flat(local_base), flat(seg), pieces.astype(i32))

    xs = _dispatch_call(tables + (content_end.astype(i32), tail_n, last_used + 1), tokens, gain, mod, row, loc,
                        n_blocks * rows, tile)
    ys = _gmm_call(layer, blk_e, blk_valid, last_used, xs, w_gu, b_gu, w_down, b_down)
    return _combine_call(tables, ys, loc.T, gate.T, tokens, mod, row, final_gain, tile, final)


def _lru_in_kernel(x_ref, gain_ref, sc_ref, sh_ref, w_ref, gg_ref, u_ref):
    d_rnn = gg_ref.shape[1]
    h = _norm_mod(x_ref[...], gain_ref[...], sc_ref[...], sh_ref[...])
    gu = jnp.dot(h.astype(bf16), w_ref[...], preferred_element_type=f32)
    gate = gu[:, :d_rnn]
    gg_ref[...] = 0.5 * gate * (1.0 + jnp.tanh(GELU_C * (gate + 0.044715 * (gate * gate * gate))))
    u_ref[...] = gu[:, d_rnn:]


def _lru_in_call(tokens, gain, mod, row, w_in_bf, tile):
    n, d = tokens.shape
    d2 = w_in_bf.shape[1]
    d_rnn = d2 // 2
    return pl.pallas_call(
        _lru_in_kernel, grid=(n // tile,),
        in_specs=[pl.BlockSpec((tile, d), lambda i: (i, 0)),
                  pl.BlockSpec((1, d), lambda i: (0, 0)),
                  _mod_spec(row, 1, d), _mod_spec(row, 0, d),
                  pl.BlockSpec((d, d2), lambda i: (0, 0))],
        out_specs=[pl.BlockSpec((tile, d_rnn), lambda i: (i, 0)), pl.BlockSpec((tile, d_rnn), lambda i: (i, 0))],
        out_shape=[jax.ShapeDtypeStruct((n, d_rnn), f32), jax.ShapeDtypeStruct((n, d_rnn), f32)],
        compiler_params=_params("arbitrary"), name="lru_in",
    )(tokens, gain, mod, mod, w_in_bf)


def _conv(u, cw_ref, cb_ref):
    n = u.shape[0]
    t = lax.broadcasted_iota(i32, u.shape, 0)
    y = None
    for k in range(CONV_WIDTH):
        j = CONV_LEFT - k
        term = (u if j == 0 else _shift(u, j, 1, t, n)) * cw_ref[k:k + 1, :]
        y = term if y is None else y + term
    return y + cb_ref[...]


def _lru_kernel(ul_ref, uc_ref, gg_ref, cw_ref, cb_ref, wr_ref, br_ref, wi_ref, bi_ref, lam_ref, o_ref,
                ucl, ucc, hf, a_s, b_s, *, chunk):
    seq = ul_ref.shape[0]
    ctx_len = uc_ref.shape[0]
    width = ul_ref.shape[1]
    sub = lax.broadcasted_iota(i32, (SUBLANES, width), 0)

    ucl[...] = _conv(ul_ref[...], cw_ref, cb_ref)
    ucc[...] = _conv(uc_ref[...], cw_ref, cb_ref)

    def run_direction(src, length, d, reverse, carry, store):
        w_r = wr_ref[d].astype(bf16)
        w_i = wi_ref[d].astype(bf16)
        b_r, b_i = br_ref[d], bi_ref[d]
        lam = lam_ref[d]
        softplus_neg = jnp.maximum(-lam, 0.0) + jnp.log1p(jnp.exp(-jnp.abs(lam)))
        n_chunks = length // chunk
        n_groups = chunk // SUBLANES

        def chunk_body(ci, carry):
            c = (n_chunks - 1 - ci) if reverse else ci
            base = pl.multiple_of(c * chunk, chunk)
            u = src[pl.ds(base, chunk), :]
            ub = u.astype(bf16)
            r = jax.nn.sigmoid(jnp.dot(ub, w_r, preferred_element_type=f32) + b_r)
            gi = jax.nn.sigmoid(jnp.dot(ub, w_i, preferred_element_type=f32) + b_i)
            log_a = (-LRU_C * r) * softplus_neg
            a = jnp.exp(log_a)
            mult = jnp.sqrt(-jnp.tanh(log_a) * (a * a + 1.0))
            a_s[...] = a
            b_s[...] = mult * (gi * u)

            def group_body(gidx, h8):
                g = (n_groups - 1 - gidx) if reverse else gidx
                off = pl.multiple_of(g * SUBLANES, SUBLANES)
                av = a_s[pl.ds(off, SUBLANES), :]
                bv = b_s[pl.ds(off, SUBLANES), :]
                for s in (1, 2, 4):
                    if reverse:
                        ok = sub < SUBLANES - s
                        a_sh = jnp.where(ok, pltpu.roll(av, SUBLANES - s, 0), 1.0)
                        b_sh = jnp.where(ok, pltpu.roll(bv, SUBLANES - s, 0), 0.0)
                    else:
                        ok = sub >= s
                        a_sh = jnp.where(ok, pltpu.roll(av, s, 0), 1.0)
                        b_sh = jnp.where(ok, pltpu.roll(bv, s, 0), 0.0)
                    bv = bv + av * b_sh
                    av = av * a_sh
                h = av * h8 + bv
                if store is not None:
                    store(pl.multiple_of(base + off, SUBLANES), h)
                edge = h[0:1, :] if reverse else h[SUBLANES - 1:SUBLANES, :]
                return jnp.broadcast_to(edge, h.shape)

            return lax.fori_loop(0, n_groups, group_body, carry)

        return lax.fori_loop(0, n_chunks, chunk_body, carry)

    zero = jnp.zeros((SUBLANES, width), f32)

    def store_fwd(row, h):
        hf[pl.ds(row, SUBLANES), :] = h

    def store_out(row, h):
        o_ref[pl.ds(row, SUBLANES), :] = gg_ref[pl.ds(row, SUBLANES), :] * (hf[pl.ds(row, SUBLANES), :] + h)

    h0 = run_direction(ucc, ctx_len, 0, False, zero, None)
    run_direction(ucl, seq, 0, False, h0, store_fwd)
    h0 = run_direction(ucc, ctx_len, 1, True, zero, None)
    run_direction(ucl, seq, 1, True, h0, store_out)


def _lru_call(gg, u, conv_w, conv_b, w_r, b_r, w_i, b_i, lam, batch, seq, ctx_len):
    n_lat = batch * seq
    d_rnn = u.shape[1]
    wb = d_rnn // N_LRU_BLOCKS
    chunk = _pick_tile(SCAN_CHUNK, seq, ctx_len)
    ctx_blk0 = n_lat // ctx_len
    vec = lambda a: a.reshape(2, 1, d_rnn)
    vspec = pl.BlockSpec((2, 1, wb), lambda b, n: (0, 0, n))
    wspec = pl.BlockSpec((2, None, wb, wb), lambda b, n: (0, n, 0, 0))
    return pl.pallas_call(
        functools.partial(_lru_kernel, chunk=chunk), grid=(batch, N_LRU_BLOCKS),
        in_specs=[pl.BlockSpec((seq, wb), lambda b, n: (b, n)),
                  pl.BlockSpec((ctx_len, wb), lambda b, n: (ctx_blk0 + b, n)),
                  pl.BlockSpec((seq, wb), lambda b, n: (b, n)),
                  pl.BlockSpec((CONV_WIDTH, wb), lambda b, n: (0, n)),
                  pl.BlockSpec((1, wb), lambda b, n: (0, n)),
                  wspec, vspec, wspec, vspec, vspec],
        out_specs=pl.BlockSpec((seq, wb), lambda b, n: (b, n)),
        out_shape=jax.ShapeDtypeStruct((n_lat, d_rnn), f32),
        scratch_shapes=[pltpu.VMEM((seq, wb), f32), pltpu.VMEM((ctx_len, wb), f32), pltpu.VMEM((seq, wb), f32),
                        pltpu.VMEM((chunk, wb), f32), pltpu.VMEM((chunk, wb), f32)],
        compiler_params=_params("arbitrary", "arbitrary"), name="lru_scan",
    )(u, u, gg, conv_w, conv_b.reshape(1, d_rnn), w_r, vec(b_r), w_i, vec(b_i), vec(lam))


def _lru_out_kernel(z_ref, x_ref, w_ref, g1_ref, o_ref):
    y = jnp.dot(z_ref[...].astype(bf16), w_ref[...], preferred_element_type=f32)
    o_ref[...] = x_ref[...] + g1_ref[...] * y


def _lru_out_call(z, tokens, w_out_bf, mod, row, tile):
    n, d = z.shape[0], tokens.shape[1]
    d_rnn = z.shape[1]
    return pl.pallas_call(
        _lru_out_kernel, grid=(n // tile,),
        in_specs=[pl.BlockSpec((tile, d_rnn), lambda i: (i, 0)),
                  pl.BlockSpec((tile, d), lambda i: (i, 0)),
                  pl.BlockSpec((d_rnn, d), lambda i: (0, 0)),
                  _mod_spec(row, 2, d)],
        out_specs=pl.BlockSpec((tile, d), lambda i: (i, 0)),
        out_shape=jax.ShapeDtypeStruct((n, d), f32),
        compiler_params=_params("arbitrary"), name="lru_out",
    )(z, tokens, w_out_bf, mod)


def kernel(x, c, ctx, c_ctx, ada_w, ada_b, norm_mix, norm_ffn, pool_w, pool_scale, lru_w_in, lru_conv_w, lru_conv_b, lru_w_r, lru_b_r, lru_w_i, lru_b_i, lru_lam, lru_w_out, router_w, router_b, exp_w_gu, exp_b_gu, exp_w_down, exp_b_down, final_norm):
    batch, seq, d = x.shape
    ctx_len = ctx.shape[1]
    assert d == SUBLANES * LANES and seq % GRID_W == 0
    n_lat, n_ctx = batch * seq, batch * ctx_len
    ctx_row = batch

    mod_rows = -(-(batch + 1) // SUBLANES) * SUBLANES
    cc = jnp.zeros((mod_rows, d), f32).at[:batch].set(c).at[batch].set(c_ctx)
    mod_all = _ada_call(cc, ada_w, ada_b)
    mods = [mod_all[l].reshape(mod_rows, 1, 6 * d) for l in range(ada_w.shape[0])]

    lat = x.reshape(n_lat, d)
    cx = ctx.reshape(n_ctx, d)
    tile = _pick_tile(ROUTE_TILE, seq, n_ctx)
    row_lat = _row_map(n_lat // tile, seq // tile, ctx_row)
    gain = lambda g: g.reshape(1, d)

    hl = _norm_call(lat, gain(norm_mix[0]), mods[0], 0, 1, row_lat, tile)
    lat = _pool_lat_call(hl, lat, pool_w[0], gain(pool_scale[0]), mods[0], batch, seq)
    cx = _pool_ctx_call(cx, gain(norm_mix[0]), pool_w[0], gain(pool_scale[0]), mods[0], ctx_row, batch, ctx_len)
    tokens = jnp.concatenate([lat, cx], axis=0)
    tokens = _moe_layer(0, tokens, gain(norm_ffn[0]), mods[0], n_lat, seq, ctx_row, router_w, router_b,
                        exp_w_gu, exp_b_gu, exp_w_down, exp_b_down, gain(final_norm), False)

    row_all = _row_map(n_lat // tile, seq // tile, ctx_row)
    gg, u = _lru_in_call(tokens, gain(norm_mix[1]), mods[1], row_all, lru_w_in[0].astype(bf16), tile)
    z = _lru_call(gg, u, lru_conv_w[0], lru_conv_b[0], lru_w_r[0], lru_b_r[0], lru_w_i[0], lru_b_i[0],
                  lru_lam[0], batch, seq, ctx_len)
    lat = _lru_out_call(z, tokens, lru_w_out[0].astype(bf16), mods[1], row_lat, tile)
    out = _moe_layer(1, lat, gain(norm_ffn[1]), mods[1], n_lat, seq, ctx_row, router_w, router_b,
                     exp_w_gu, exp_b_gu, exp_w_down, exp_b_down, gain(final_norm), True)
    return out.reshape(batch, seq, d)
```

```python
import functools

import numpy as np
import jax
import jax.numpy as jnp
from jax import lax
from jax.experimental import pallas as pl
from jax.experimental.pallas import tpu as pltpu

f32 = jnp.float32
bf16 = jnp.bfloat16
i32 = jnp.int32
HIGHEST = lax.Precision.HIGHEST

LANES = 128
SUBLANES = 8
VMEM_LIMIT = 56 * 1024 * 1024

GRID_W = 64
GRID_SHIFT = 6
POOL_WINDOWS = (2, 4, 8, 16)
N_POOL_GROUPS = 4
N_LRU_BLOCKS = 4
CONV_LEFT = 2
CONV_WIDTH = 4
LRU_C = 8.0
N_EXPERTS = 32
TOP_K = 4
SWIGLU_LIMIT = 7.0
SWIGLU_ALPHA = 1.702
NORM_EPS = 1e-6
GELU_C = 0.7978845608028654

EXPERT_ROWS = 256
ROUTE_TILE = 512
LRU_TILE = 128
SCAN_CHUNK = 64
SCAN_UNROLL = 8


def _params(*sem):
    return pltpu.CompilerParams(dimension_semantics=sem, vmem_limit_bytes=VMEM_LIMIT)


def _pick_tile(pref, *sizes):
    t = pref
    while any(s % t for s in sizes):
        t //= 2
    assert t >= SUBLANES
    return t


def _norm_mod(x, gain, scale, shift):
    ms = jnp.mean(x * x, axis=-1, keepdims=True)
    y = x * lax.rsqrt(ms + NORM_EPS)
    return (y * gain) * (1.0 + scale) + shift


def _row_map(n_lat_tiles, tiles_per_batch, ctx_row):
    def row(i):
        return jnp.where(i < n_lat_tiles, i // tiles_per_batch, ctx_row)
    return row


def _mod_spec(row, chunk, width):
    return pl.BlockSpec((None, 1, width), lambda i, *_: (row(i), 0, chunk))


def _ada_kernel(c_ref, w_ref, b_ref, o_ref):
    c = c_ref[...]
    s = c * jax.nn.sigmoid(c)
    o_ref[...] = jnp.dot(s, w_ref[...], preferred_element_type=f32, precision=HIGHEST) + b_ref[...]


def _ada_call(cc, ada_w, ada_b):
    depth, d, d6 = ada_w.shape
    rows = cc.shape[0]
    bn = 1536
    return pl.pallas_call(
        _ada_kernel, grid=(depth, d6 // bn),
        in_specs=[pl.BlockSpec((rows, d), lambda l, j: (0, 0)),
                  pl.BlockSpec((None, d, bn), lambda l, j: (l, 0, j)),
                  pl.BlockSpec((None, 1, bn), lambda l, j: (l, 0, j))],
        out_specs=pl.BlockSpec((None, rows, bn), lambda l, j: (l, 0, j)),
        out_shape=jax.ShapeDtypeStruct((depth, rows, d6), f32),
        compiler_params=_params("arbitrary", "arbitrary"), name="ada_mod",
    )(cc, ada_w, ada_b.reshape(depth, 1, d6))


def _norm_kernel(x_ref, g_ref, sc_ref, sh_ref, o_ref):
    o_ref[...] = _norm_mod(x_ref[...], g_ref[...], sc_ref[...], sh_ref[...])


def _norm_call(tokens, gain, mod, sh_chunk, sc_chunk, row, tile):
    n, d = tokens.shape
    return pl.pallas_call(
        _norm_kernel, grid=(n // tile,),
        in_specs=[pl.BlockSpec((tile, d), lambda i: (i, 0)),
                  pl.BlockSpec((1, d), lambda i: (0, 0)),
                  _mod_spec(row, sc_chunk, d), _mod_spec(row, sh_chunk, d)],
        out_specs=pl.BlockSpec((tile, d), lambda i: (i, 0)),
        out_shape=jax.ShapeDtypeStruct((n, d), f32),
        compiler_params=_params("arbitrary"), name="norm_mod",
    )(tokens, gain, mod, mod)


def _shift(x, j, unit, pos, limit):
    n = x.shape[0]
    rolled = pltpu.roll(x, (j * unit) % n, 0)
    ok = (pos >= j) if j > 0 else (pos < limit + j)
    return jnp.where(ok, rolled, 0.0)


def _window_sum(x, w, unit, pos, limit):
    m = w // 2
    trail, lead, s = x, x, 1
    while s < m:
        trail = trail + _shift(trail, s, unit, pos, limit)
        lead = lead + _shift(lead, -s, unit, pos, limit)
        s *= 2
    return _shift(trail, 1, unit, pos, limit) + lead


def _window_count(w, pos, limit):
    m = w // 2
    return jnp.minimum(pos + m, limit) - jnp.maximum(pos - m, 0)


def _pool_grid(h, w):
    t = lax.broadcasted_iota(i32, h.shape, 0)
    col, row = t & (GRID_W - 1), t >> GRID_SHIFT
    rows = h.shape[0] // GRID_W
    total = _window_sum(_window_sum(h, w, 1, col, GRID_W), w, GRID_W, row, rows)
    count = (_window_count(w, row, rows) * _window_count(w, col, GRID_W)).astype(f32)
    return total / count


def _pool_seq(h, w):
    t = lax.broadcasted_iota(i32, h.shape, 0)
    n = h.shape[0]
    return _window_sum(h, w, 1, t, n) / _window_count(w, t, n).astype(f32)


def _pool_lat_kernel(h_ref, x_ref, w_ref, ps_ref, g1_ref, o_ref):
    grp = pl.program_id(1)
    for gi, w in enumerate(POOL_WINDOWS):
        @pl.when(grp == gi)
        def _(w=w):
            h = h_ref[...]
            d = _pool_grid(h, w) - h
            y = jnp.dot(d.astype(bf16), w_ref[...].astype(bf16), preferred_element_type=f32)
            o_ref[...] = x_ref[...] + g1_ref[...] * (y * ps_ref[...])


def _pool_lat_call(h, x, pool_w, pool_scale, mod, batch, seq):
    n, d = x.shape
    cg = d // N_POOL_GROUPS
    g1_chunk0 = 2 * N_POOL_GROUPS
    return pl.pallas_call(
        _pool_lat_kernel, grid=(batch, N_POOL_GROUPS),
        in_specs=[pl.BlockSpec((seq, cg), lambda b, g: (b, g)),
                  pl.BlockSpec((seq, cg), lambda b, g: (b, g)),
                  pl.BlockSpec((None, cg, cg), lambda b, g: (g, 0, 0)),
                  pl.BlockSpec((1, cg), lambda b, g: (0, g)),
                  pl.BlockSpec((None, 1, cg), lambda b, g: (b, 0, g1_chunk0 + g))],
        out_specs=pl.BlockSpec((seq, cg), lambda b, g: (b, g)),
        out_shape=jax.ShapeDtypeStruct((n, d), f32),
        compiler_params=_params("arbitrary", "arbitrary"), name="pool_lat",
    )(h, x, pool_w, pool_scale, mod)


def _pool_ctx_kernel(x_ref, gain_ref, sc_ref, sh_ref, w_ref, ps_ref, g1_ref, o_ref):
    x = x_ref[...]
    h = _norm_mod(x, gain_ref[...], sc_ref[...], sh_ref[...])
    cg = x.shape[1] // N_POOL_GROUPS
    for gi, w in enumerate(POOL_WINDOWS):
        sl = slice(gi * cg, (gi + 1) * cg)
        hg = h[:, sl]
        d = _pool_seq(hg, w) - hg
        y = jnp.dot(d.astype(bf16), w_ref[gi].astype(bf16), preferred_element_type=f32)
        o_ref[:, sl] = x[:, sl] + g1_ref[:, sl] * (y * ps_ref[:, sl])


def _pool_ctx_call(cx, gain, pool_w, pool_scale, mod, ctx_row, batch, ctx_len):
    n, d = cx.shape
    cg = d // N_POOL_GROUPS
    row = lambda i: ctx_row
    return pl.pallas_call(
        _pool_ctx_kernel, grid=(batch,),
        in_specs=[pl.BlockSpec((ctx_len, d), lambda b: (b, 0)),
                  pl.BlockSpec((1, d), lambda b: (0, 0)),
                  _mod_spec(row, 1, d), _mod_spec(row, 0, d),
                  pl.BlockSpec((N_POOL_GROUPS, cg, cg), lambda b: (0, 0, 0)),
                  pl.BlockSpec((1, d), lambda b: (0, 0)),
                  _mod_spec(row, 2, d)],
        out_specs=pl.BlockSpec((ctx_len, d), lambda b: (b, 0)),
        out_shape=jax.ShapeDtypeStruct((n, d), f32),
        compiler_params=_params("arbitrary"), name="pool_ctx",
    )(cx, gain, mod, mod, pool_w, pool_scale, mod)


def _route_kernel(x_ref, gain_ref, sc_ref, sh_ref, wr_ref, rb_ref, loc_ref, gate_ref, cnt_ref):
    tt = x_ref.shape[0]
    h = _norm_mod(x_ref[...], gain_ref[...], sc_ref[...], sh_ref[...])

    logits = lax.dot_general(wr_ref[...], h, (((1,), (1,)), ((), ())),
                             precision=HIGHEST, preferred_element_type=f32) + rb_ref[...]
    e_iota = lax.broadcasted_iota(i32, logits.shape, 0)
    vals, onehots = [], []
    work = logits
    for k in range(TOP_K):
        m = jnp.max(work, axis=0, keepdims=True)
        idx = jnp.min(jnp.where(work == m, e_iota, N_EXPERTS), axis=0, keepdims=True)
        hit = e_iota == idx
        vals.append(m)
        onehots.append(jnp.where(hit, 1.0, 0.0))
        work = jnp.where(hit, -jnp.inf, work)

    ex = [jnp.exp(v - vals[0]) for v in vals]
    den = ex[0] + ex[1] + ex[2] + ex[3]
    for k in range(TOP_K):
        gate_ref[k:k + 1, :] = ex[k] / den

    tri = jnp.where(lax.broadcasted_iota(i32, (tt, tt), 0) <= lax.broadcasted_iota(i32, (tt, tt), 1), 1.0, 0.0)
    planes = jnp.concatenate(onehots, axis=0).astype(bf16)
    incl = jnp.dot(planes, tri.astype(bf16), preferred_element_type=f32)
    totals = [incl[k * N_EXPERTS:(k + 1) * N_EXPERTS, tt - 1:tt] for k in range(TOP_K)]
    count = jnp.broadcast_to(totals[0] + totals[1] + totals[2] + totals[3], (N_EXPERTS, LANES))
    padded = (((count.astype(i32) + (SUBLANES - 1)) // SUBLANES) * SUBLANES).astype(f32)
    below = jnp.where(lax.broadcasted_iota(i32, (N_EXPERTS, N_EXPERTS), 1)
                      < lax.broadcasted_iota(i32, (N_EXPERTS, N_EXPERTS), 0), 1.0, 0.0)
    offset = jnp.dot(below.astype(bf16), padded.astype(bf16), preferred_element_type=f32)[:, 0:1]
    for k in range(TOP_K):
        inc_k = incl[k * N_EXPERTS:(k + 1) * N_EXPERTS]
        before = inc_k - onehots[k] + offset
        loc_ref[k:k + 1, :] = jnp.sum(onehots[k] * before, axis=0, keepdims=True).astype(i32)
        offset = offset + totals[k]
    cnt_ref[...] = count


def _route_call(tokens, gain, mod, row, w_router_t, b_router, tile):
    n, d = tokens.shape
    e = w_router_t.shape[0]
    return pl.pallas_call(
        _route_kernel, grid=(n // tile,),
        in_specs=[pl.BlockSpec((tile, d), lambda i: (i, 0)),
                  pl.BlockSpec((1, d), lambda i: (0, 0)),
                  _mod_spec(row, 4, d), _mod_spec(row, 3, d),
                  pl.BlockSpec((e, d), lambda i: (0, 0)),
                  pl.BlockSpec((e, 1), lambda i: (0, 0))],
        out_specs=[pl.BlockSpec((TOP_K, tile), lambda i: (0, i)),
                   pl.BlockSpec((TOP_K, tile), lambda i: (0, i)),
                   pl.BlockSpec((None, e, LANES), lambda i: (i, 0, 0))],
        out_shape=[jax.ShapeDtypeStruct((TOP_K, n), i32),
                   jax.ShapeDtypeStruct((TOP_K, n), f32),
                   jax.ShapeDtypeStruct((n // tile, e, LANES), f32)],
        compiler_params=_params("arbitrary"), name="moe_route",
    )(tokens, gain, mod, mod, w_router_t, b_router.reshape(e, 1))


def _copy_run(src_at, dst_at, n_pieces, n_bits, sem, wait=False):
    for k in reversed(range(n_bits)):
        size = SUBLANES << k
        done = (n_pieces >> (k + 1)) << (k + 1)

        @pl.when(((n_pieces >> k) & 1) == 1)
        def _(size=size, done=done):
            off = 0 if wait else pl.multiple_of(done * SUBLANES, SUBLANES)
            copy = pltpu.make_async_copy(src_at(off, size), dst_at(off, size), sem)
            copy.wait() if wait else copy.start()


def _dispatch_kernel(go_ref, lb_ref, p8_ref, np_ref, ts_ref, tn_ref, nu_ref,
                     x_ref, gain_ref, sc_ref, sh_ref, loc_ref, xs_hbm, xloc, zbuf, sem, *, n_tiles, n_blocks):
    i = pl.program_id(0)
    tt = x_ref.shape[0]
    rl = xloc.shape[1]
    slot = i % 2
    h = _norm_mod(x_ref[...], gain_ref[...], sc_ref[...], sh_ref[...]).astype(bf16)
    r_iota = lax.broadcasted_iota(i32, (rl, tt), 0)
    sel = None
    for k in range(TOP_K):
        m = jnp.where(r_iota == loc_ref[k:k + 1, :], 1.0, 0.0)
        sel = m if sel is None else sel + m
    xloc[slot] = jnp.dot(sel.astype(bf16), h, preferred_element_type=f32)

    seg_bits = (tt // SUBLANES).bit_length()
    tile_bits = (rl // SUBLANES).bit_length()
    buf_at = lambda off, size: xloc.at[0, pl.ds(off, size), :]
    hbm_at = lambda off, size: xs_hbm.at[pl.ds(off, size), :]

    @pl.when(i > 0)
    def _():
        _copy_run(buf_at, hbm_at, np_ref[jnp.maximum(i - 1, 0)], tile_bits, sem.at[1 - slot], wait=True)

    def per_expert(e, carry):
        idx = i * N_EXPERTS + e
        src0 = pl.multiple_of(lb_ref[idx], SUBLANES)
        dst0 = pl.multiple_of(go_ref[idx], SUBLANES)
        _copy_run(lambda off, size: xloc.at[slot, pl.ds(src0 + off, size), :],
                  lambda off, size: xs_hbm.at[pl.ds(dst0 + off, size), :],
                  p8_ref[idx] // SUBLANES, seg_bits, sem.at[slot])
        return carry
    lax.fori_loop(0, N_EXPERTS, per_expert, 0)

    @pl.when(i == n_tiles - 1)
    def _():
        _copy_run(buf_at, hbm_at, np_ref[i], tile_bits, sem.at[slot], wait=True)
        zbuf[...] = jnp.zeros(zbuf.shape, f32)
        tail_bits = (EXPERT_ROWS // SUBLANES - 1).bit_length()
        zero_at = lambda off, size: zbuf.at[pl.ds(0, size), :]

        def tail(e, carry):
            t0 = pl.multiple_of(ts_ref[e], SUBLANES)
            _copy_run(zero_at, lambda off, size: xs_hbm.at[pl.ds(t0 + off, size), :], tn_ref[e], tail_bits,
                      sem.at[slot])
            return carry
        lax.fori_loop(0, N_EXPERTS, tail, 0)

        def tail_wait(e, carry):
            _copy_run(zero_at, hbm_at, tn_ref[e], tail_bits, sem.at[slot], wait=True)
            return carry
        lax.fori_loop(0, N_EXPERTS, tail_wait, 0)

        def spare(b, carry):
            t = pl.multiple_of(b * EXPERT_ROWS, EXPERT_ROWS)
            pltpu.make_async_copy(zbuf, xs_hbm.at[pl.ds(t, EXPERT_ROWS), :], sem.at[slot]).start()
            return carry
        lax.fori_loop(nu_ref[0], n_blocks, spare, 0)

        def spare_wait(b, carry):
            pltpu.make_async_copy(zbuf, xs_hbm.at[pl.ds(0, EXPERT_ROWS), :], sem.at[slot]).wait()
            return carry
        lax.fori_loop(nu_ref[0], n_blocks, spare_wait, 0)


def _dispatch_call(tables, tokens, gain, mod, row, loc, n_rows, tile):
    n, d = tokens.shape
    n_tiles = n // tile
    rl = TOP_K * tile + N_EXPERTS * SUBLANES
    spec = lambda shape, fn: pl.BlockSpec(shape, lambda i, *_: fn(i))
    grid_spec = pltpu.PrefetchScalarGridSpec(
        num_scalar_prefetch=7, grid=(n_tiles,),
        in_specs=[spec((tile, d), lambda i: (i, 0)),
                  spec((1, d), lambda i: (0, 0)),
                  _mod_spec(row, 4, d), _mod_spec(row, 3, d),
                  spec((TOP_K, tile), lambda i: (0, i))],
        out_specs=pl.BlockSpec(memory_space=pl.ANY),
        scratch_shapes=[pltpu.VMEM((2, rl, d), f32), pltpu.VMEM((EXPERT_ROWS, d), f32),
                        pltpu.SemaphoreType.DMA((2,))])
    return pl.pallas_call(
        functools.partial(_dispatch_kernel, n_tiles=n_tiles, n_blocks=n_rows // EXPERT_ROWS), grid_spec=grid_spec,
        out_shape=jax.ShapeDtypeStruct((n_rows, d), f32),
        compiler_params=_params("arbitrary"), name="moe_dispatch",
    )(*tables, tokens, gain, mod, mod, loc)


def _gmm_kernel(be_ref, bv_ref, lu_ref, x_ref, wgu_ref, bgu_ref, wd_ref, bd_ref, o_ref, wgu_bf, wd_bf):
    j = pl.program_id(0)
    d_ff = wd_ref.shape[0]
    used = bv_ref[j] > 0

    @pl.when(used)
    def _():
        new_expert = jnp.logical_or(j == 0, be_ref[j] != be_ref[jnp.maximum(j - 1, 0)])

        @pl.when(new_expert)
        def _():
            wgu_bf[...] = wgu_ref[...].astype(bf16)
            wd_bf[...] = wd_ref[...].astype(bf16)

        x = x_ref[...].astype(bf16)
        gu = jnp.dot(x, wgu_bf[...], preferred_element_type=f32) + bgu_ref[...]
        g = jnp.minimum(gu[:, :d_ff], SWIGLU_LIMIT)
        u = jnp.clip(gu[:, d_ff:], -SWIGLU_LIMIT, SWIGLU_LIMIT)
        act = (u + 1.0) * g * jax.nn.sigmoid(SWIGLU_ALPHA * g)
        o_ref[...] = jnp.dot(act.astype(bf16), wd_bf[...], preferred_element_type=f32) + bd_ref[...]

    @pl.when(jnp.logical_not(used))
    def _():
        o_ref[...] = jnp.zeros(o_ref.shape, f32)


def _gmm_call(layer, blk_e, blk_valid, last_used, xs, w_gu, b_gu, w_down, b_down):
    depth, e, d, f2 = w_gu.shape
    d_ff = w_down.shape[2]
    rows = EXPERT_ROWS
    n_blocks = xs.shape[0] // rows
    grid_spec = pltpu.PrefetchScalarGridSpec(
        num_scalar_prefetch=3, grid=(n_blocks,),
        in_specs=[pl.BlockSpec((rows, d), lambda j, be, bv, lu: (jnp.minimum(j, lu[0]), 0)),
                  pl.BlockSpec((None, None, d, f2), lambda j, be, bv, lu: (layer, be[j], 0, 0)),
                  pl.BlockSpec((None, None, 1, f2), lambda j, be, bv, lu: (layer, be[j], 0, 0)),
                  pl.BlockSpec((None, None, d_ff, d), lambda j, be, bv, lu: (layer, be[j], 0, 0)),
                  pl.BlockSpec((None, None, 1, d), lambda j, be, bv, lu: (layer, be[j], 0, 0))],
        out_specs=pl.BlockSpec((rows, d), lambda j, be, bv, lu: (j, 0)),
        scratch_shapes=[pltpu.VMEM((d, f2), bf16), pltpu.VMEM((d_ff, d), bf16)])
    return pl.pallas_call(
        _gmm_kernel, grid_spec=grid_spec,
        out_shape=jax.ShapeDtypeStruct((n_blocks * rows, d), f32),
        compiler_params=_params("arbitrary"), name="moe_experts",
    )(blk_e, blk_valid, last_used, xs, w_gu, b_gu.reshape(depth, e, 1, f2), w_down, b_down.reshape(depth, e, 1, d))


def _combine_kernel(go_ref, lb_ref, p8_ref, np_ref,
                    ys_hbm, loc_ref, gate_ref, x_ref, g2_ref, fn_ref, o_ref, yloc, sem, *, n_tiles, final):
    i = pl.program_id(0)
    tt = x_ref.shape[0]
    rl = yloc.shape[1]
    slot = i % 2

    seg_bits = (tt // SUBLANES).bit_length()
    tile_bits = (rl // SUBLANES).bit_length()

    def fetch(tile, buf):
        def per_expert(e, carry):
            idx = tile * N_EXPERTS + e
            src0 = pl.multiple_of(go_ref[idx], SUBLANES)
            dst0 = pl.multiple_of(lb_ref[idx], SUBLANES)
            _copy_run(lambda off, size: ys_hbm.at[pl.ds(src0 + off, size), :],
                      lambda off, size: yloc.at[buf, pl.ds(dst0 + off, size), :],
                      p8_ref[idx] // SUBLANES, seg_bits, sem.at[buf])
            return carry
        lax.fori_loop(0, N_EXPERTS, per_expert, 0)

    @pl.when(i == 0)
    def _():
        yloc[...] = jnp.zeros(yloc.shape, f32)
        fetch(0, 0)

    @pl.when(i + 1 < n_tiles)
    def _():
        fetch(jnp.minimum(i + 1, n_tiles - 1), 1 - slot)

    _copy_run(lambda off, size: ys_hbm.at[pl.ds(off, size), :], lambda off, size: yloc.at[0, pl.ds(off, size), :],
              np_ref[i], tile_bits, sem.at[slot], wait=True)

    r_iota = lax.broadcasted_iota(i32, (tt, rl), 1)
    weights = None
    for k in range(TOP_K):
        m = jnp.where(r_iota == loc_ref[:, k:k + 1], gate_ref[:, k:k + 1], 0.0)
        weights = m if weights is None else weights + m
    y = jnp.dot(weights.astype(bf16), yloc[slot].astype(bf16), preferred_element_type=f32)
    out = x_ref[...] + g2_ref[...] * y
    if final:
        ms = jnp.mean(out * out, axis=-1, keepdims=True)
        out = (out * lax.rsqrt(ms + NORM_EPS)) * fn_ref[...]
    o_ref[...] = out


def _combine_call(tables, ys, loc_t, gates_t, tokens, mod, row, final_gain, tile, final):
    n, d = tokens.shape
    n_tiles = n // tile
    rl = TOP_K * tile + N_EXPERTS * SUBLANES
    spec = lambda shape, fn: pl.BlockSpec(shape, lambda i, *_: fn(i))
    grid_spec = pltpu.PrefetchScalarGridSpec(
        num_scalar_prefetch=4, grid=(n_tiles,),
        in_specs=[pl.BlockSpec(memory_space=pl.ANY),
                  spec((tile, TOP_K), lambda i: (i, 0)),
                  spec((tile, TOP_K), lambda i: (i, 0)),
                  spec((tile, d), lambda i: (i, 0)),
                  _mod_spec(row, 5, d),
                  spec((1, d), lambda i: (0, 0))],
        out_specs=spec((tile, d), lambda i: (i, 0)),
        scratch_shapes=[pltpu.VMEM((2, rl, d), f32), pltpu.SemaphoreType.DMA((2,))])
    return pl.pallas_call(
        functools.partial(_combine_kernel, n_tiles=n_tiles, final=final), grid_spec=grid_spec,
        out_shape=jax.ShapeDtypeStruct((n, d), f32),
        compiler_params=_params("arbitrary"), name="moe_combine",
    )(*tables, ys, loc_t, gates_t, tokens, mod, final_gain)


def _moe_layer(layer, tokens, gain, mod, n_lat, seq, ctx_row, w_router, b_router, w_gu, b_gu, w_down, b_down,
               final_gain, final):
    n, d = tokens.shape
    tile = _pick_tile(ROUTE_TILE, n_lat, n - n_lat if n > n_lat else n_lat, seq)
    n_tiles = n // tile
    row = _row_map(n_lat // tile, seq // tile, ctx_row)

    loc, gate, cnt = _route_call(tokens, gain, mod, row, w_router[layer].T, b_router[layer], tile)

    rows = EXPERT_ROWS
    count = cnt[:, :, 0].astype(i32)
    seg = (count + SUBLANES - 1) // SUBLANES * SUBLANES
    local_base = jnp.cumsum(seg, axis=1) - seg
    pieces = jnp.sum(seg, axis=1) // SUBLANES
    expert_rows = jnp.sum(seg, axis=0)
    region = (expert_rows + rows - 1) // rows * rows
    region_end = jnp.cumsum(region)
    region_start = region_end - region
    global_off = region_start[None, :] + jnp.cumsum(seg, axis=0) - seg
    n_blocks = (n * TOP_K + n_tiles * N_EXPERTS * (SUBLANES - 1) + N_EXPERTS * (rows - 1) + rows - 1) // rows
    blk_start = jnp.arange(n_blocks, dtype=i32) * rows
    experts = jnp.arange(N_EXPERTS, dtype=i32)
    blk_e = jnp.minimum(jnp.sum(region_end[None, :] <= blk_start[:, None], axis=1), N_EXPERTS - 1).astype(i32)
    content_end = region_start + expert_rows
    blk_end = jnp.sum(jnp.where(blk_e[:, None] == experts[None, :], content_end[None, :], 0), axis=1)
    blk_valid = (blk_start < blk_end).astype(i32)
    last_used = jnp.maximum(region_end[-1:] // rows - 1, 0).astype(i32)
    tail_n = ((region - expert_rows) // SUBLANES).astype(i32)
    flat = lambda a: a.reshape(-1).astype(i32)
    tables = (flat(global_off), flat(local_base), flat(seg), pieces.astype(i32))

    xs = _dispatch_call(tables + (content_end.astype(i32), tail_n, last_used + 1), tokens, gain, mod, row, loc,
                        n_blocks * rows, tile)
    ys = _gmm_call(layer, blk_e, blk_valid, last_used, xs, w_gu, b_gu, w_down, b_down)
    return _combine_call(tables, ys, loc.T, gate.T, tokens, mod, row, final_gain, tile, final)


def _time_tile_blocks(n_lat_t, n_ctx_t, n_lat_blocks, b):
    def index(i):
        lat_blk = b * n_lat_t + jnp.minimum(i, n_lat_t - 1)
        ctx_blk = n_lat_blocks + b * n_ctx_t + jnp.maximum(i - n_lat_t, 0)
        return (jnp.where(i < n_lat_t, lat_blk, ctx_blk), 0)
    return index


def _lru_in_kernel(*refs, batch, n_lat_t):
    x_refs = refs[:batch]
    gain_ref, sc_ref, sh_ref, scc_ref, shc_ref, w_ref, gg_ref, u_ref, tm_s = refs[batch:]
    tt = x_refs[0].shape[0]
    d_rnn = gg_ref.shape[1]
    is_ctx = pl.program_id(0) >= n_lat_t
    hs = []
    for b in range(batch):
        scale = jnp.where(is_ctx, scc_ref[...], sc_ref[b])
        shift = jnp.where(is_ctx, shc_ref[...], sh_ref[b])
        hs.append(_norm_mod(x_refs[b][...], gain_ref[...], scale, shift).astype(bf16))
    gu = jnp.dot(jnp.concatenate(hs, axis=0), w_ref[...], preferred_element_type=f32)
    gate = gu[:, :d_rnn]
    gelu = 0.5 * gate * (1.0 + jnp.tanh(GELU_C * (gate + 0.044715 * (gate * gate * gate))))
    for out_ref, val in ((gg_ref, gelu), (u_ref, gu[:, d_rnn:])):
        for b in range(batch):
            for s in range(d_rnn // LANES):
                tm_s[s, pl.ds(b, tt, stride=batch), :] = val[b * tt:(b + 1) * tt, s * LANES:(s + 1) * LANES]
        for s in range(d_rnn // LANES):
            out_ref[:, s * LANES:(s + 1) * LANES] = tm_s[s]


def _lru_in_call(tokens, gain, mod, ctx_row, w_in_bf, batch, seq, ctx_len, tt):
    n, d = tokens.shape
    d2 = w_in_bf.shape[1]
    d_rnn = d2 // 2
    n_lat_t, n_ctx_t = seq // tt, ctx_len // tt
    n_lat_blocks = batch * n_lat_t
    x_specs = [pl.BlockSpec((tt, d), _time_tile_blocks(n_lat_t, n_ctx_t, n_lat_blocks, b)) for b in range(batch)]
    mod_lat = lambda chunk: pl.BlockSpec((batch, 1, d), lambda i: (0, 0, chunk))
    mod_ctx = lambda chunk: pl.BlockSpec((None, 1, d), lambda i: (ctx_row, 0, chunk))
    out_spec = pl.BlockSpec((tt * batch, d_rnn), lambda i: (i, 0))
    out_shape = jax.ShapeDtypeStruct(((seq + ctx_len) * batch, d_rnn), f32)
    return pl.pallas_call(
        functools.partial(_lru_in_kernel, batch=batch, n_lat_t=n_lat_t), grid=(n_lat_t + n_ctx_t,),
        in_specs=x_specs + [pl.BlockSpec((1, d), lambda i: (0, 0)), mod_lat(1), mod_lat(0), mod_ctx(1), mod_ctx(0),
                            pl.BlockSpec((d, d2), lambda i: (0, 0))],
        out_specs=[out_spec, out_spec], out_shape=[out_shape, out_shape],
        scratch_shapes=[pltpu.VMEM((d_rnn // LANES, tt * batch, LANES), f32)],
        compiler_params=_params("arbitrary"), name="lru_in",
    )(*([tokens] * batch), gain, mod, mod, mod, mod, w_in_bf)


def _lru_scan_kernel(*refs, batch, n_ctx_c, n_lat_c, reverse):
    if reverse:
        (prev_ref, cur_ref, next_ref, cw_ref, cb_ref, wr_ref, br_ref, wi_ref, bi_ref, lam_ref, hf_ref, gg_ref,
         o_ref, a_s, b_s, h_s, z_s) = refs
    else:
        (prev_ref, cur_ref, next_ref, cw_ref, cb_ref, wr_ref, br_ref, wi_ref, bi_ref, lam_ref,
         o_ref, a_s, b_s, h_s) = refs
    s = pl.program_id(0)
    rows, d_rnn = cur_ref.shape
    steps = rows // batch
    wb = d_rnn // N_LRU_BLOCKS

    @pl.when(s == 0)
    def _():
        h_s[...] = jnp.zeros(h_s.shape, f32)

    c = jnp.where(s < n_ctx_c, s, s - n_ctx_c)
    seg_chunks = jnp.where(s < n_ctx_c, n_ctx_c, n_lat_c)
    c = (seg_chunks - 1 - c) if reverse else c
    has_prev = c > 0
    has_next = c < seg_chunks - 1
    ext = jnp.concatenate([jnp.where(has_prev, prev_ref[...], 0.0), cur_ref[...],
                           jnp.where(has_next, next_ref[...], 0.0)], axis=0)
    u = cb_ref[...]
    for k in range(CONV_WIDTH):
        u = u + ext[k * batch:k * batch + rows] * cw_ref[k:k + 1, :]

    lam = lam_ref[...]
    softplus_neg = jnp.maximum(-lam, 0.0) + jnp.log1p(jnp.exp(-jnp.abs(lam)))
    for n in range(N_LRU_BLOCKS):
        cols = slice(n * wb, (n + 1) * wb)
        un = u[:, cols]
        ub = un.astype(bf16)
        pre_r = jnp.dot(ub, wr_ref[n].astype(bf16), preferred_element_type=f32) + br_ref[:, cols]
        pre_i = jnp.dot(ub, wi_ref[n].astype(bf16), preferred_element_type=f32) + bi_ref[:, cols]
        r = 0.5 * (1.0 + jnp.tanh(0.5 * pre_r))
        gi = 0.5 * (1.0 + jnp.tanh(0.5 * pre_i))
        log_a = (-LRU_C * r) * softplus_neg[:, cols]
        a = jnp.exp(log_a)
        mult = jnp.sqrt(-jnp.tanh(log_a) * (a * a + 1.0))
        a_s[:, cols] = a
        b_s[:, cols] = mult * (gi * un)

    def step(q, h):
        for j in range(SCAN_UNROLL):
            t = q * SCAN_UNROLL + j
            t = (steps - 1 - t) if reverse else t
            row = pl.multiple_of(t * batch, batch)
            h = a_s[pl.ds(row, batch), :] * h + b_s[pl.ds(row, batch), :]
            if reverse:
                z_s[pl.ds(row, batch), :] = h
            else:
                o_ref[pl.ds(row, batch), :] = h
        return h
    h_s[...] = lax.fori_loop(0, steps // SCAN_UNROLL, step, h_s[...])
    if reverse:
        o_ref[...] = (gg_ref[...] * (hf_ref[...] + z_s[...])).astype(o_ref.dtype)


def _lru_scan_call(u, hf, gg, conv_w, conv_b, w_r, b_r, w_i, b_i, lam, batch, seq, ctx_len, chunk, reverse):
    n_rows, d_rnn = u.shape
    rows = chunk * batch
    n_lat_c, n_ctx_c = seq // chunk, ctx_len // chunk
    pre, post = CONV_LEFT * batch, (CONV_WIDTH - 1 - CONV_LEFT) * batch
    wb = d_rnn // N_LRU_BLOCKS

    def blk(s):
        ci = jnp.where(s < n_ctx_c, s, s - n_ctx_c)
        if reverse:
            ci = jnp.where(s < n_ctx_c, n_ctx_c - 1 - ci, n_lat_c - 1 - ci)
        return jnp.where(s < n_ctx_c, n_lat_c + ci, ci)

    last_post = n_rows // post - 1
    cur = pl.BlockSpec((rows, d_rnn), lambda s: (blk(s), 0))
    in_specs = [pl.BlockSpec((pre, d_rnn), lambda s: (jnp.maximum(blk(s) * (rows // pre) - 1, 0), 0)),
                cur,
                pl.BlockSpec((post, d_rnn), lambda s: (jnp.minimum((blk(s) + 1) * (rows // post), last_post), 0)),
                pl.BlockSpec((CONV_WIDTH, d_rnn), lambda s: (0, 0)),
                pl.BlockSpec((1, d_rnn), lambda s: (0, 0)),
                pl.BlockSpec((N_LRU_BLOCKS, wb, wb), lambda s: (0, 0, 0)),
                pl.BlockSpec((1, d_rnn), lambda s: (0, 0)),
                pl.BlockSpec((N_LRU_BLOCKS, wb, wb), lambda s: (0, 0, 0)),
                pl.BlockSpec((1, d_rnn), lambda s: (0, 0)),
                pl.BlockSpec((1, d_rnn), lambda s: (0, 0))]
    args = [u, u, u, conv_w, conv_b.reshape(1, d_rnn), w_r, b_r.reshape(1, d_rnn), w_i, b_i.reshape(1, d_rnn),
            lam.reshape(1, d_rnn)]
    scratch = [pltpu.VMEM((rows, d_rnn), f32), pltpu.VMEM((rows, d_rnn), f32), pltpu.VMEM((batch, d_rnn), f32)]
    if reverse:
        in_specs += [cur, cur]
        args += [hf, gg]
        scratch.append(pltpu.VMEM((rows, d_rnn), f32))
    return pl.pallas_call(
        functools.partial(_lru_scan_kernel, batch=batch, n_ctx_c=n_ctx_c, n_lat_c=n_lat_c, reverse=reverse),
        grid=(n_ctx_c + n_lat_c,), in_specs=in_specs, out_specs=cur,
        out_shape=jax.ShapeDtypeStruct((n_rows, d_rnn), bf16 if reverse else f32),
        scratch_shapes=scratch, compiler_params=_params("arbitrary"),
        name="lru_scan_rev" if reverse else "lru_scan_fwd",
    )(*args)


def _lru_out_kernel(*refs, batch):
    z_ref = refs[0]
    x_refs = refs[1:1 + batch]
    w_ref, g1_ref, o_ref, y_s = refs[1 + batch:]
    tt = x_refs[0].shape[0]
    y = jnp.dot(z_ref[...], w_ref[...], preferred_element_type=f32)
    n_planes = y.shape[1] // LANES
    for s in range(n_planes):
        y_s[s] = y[:, s * LANES:(s + 1) * LANES]
    for b in range(batch):
        y_b = jnp.concatenate([y_s[s, pl.ds(b, tt, stride=batch), :] for s in range(n_planes)], axis=-1)
        o_ref[b] = x_refs[b][...] + g1_ref[b] * y_b


def _lru_out_call(z, tokens, w_out_bf, mod, batch, seq, tt):
    d = tokens.shape[1]
    d_rnn = z.shape[1]
    n_t = seq // tt
    x_specs = [pl.BlockSpec((tt, d), lambda i, b=b: (b * n_t + i, 0)) for b in range(batch)]
    return pl.pallas_call(
        functools.partial(_lru_out_kernel, batch=batch), grid=(n_t,),
        in_specs=[pl.BlockSpec((tt * batch, d_rnn), lambda i: (i, 0))] + x_specs
                 + [pl.BlockSpec((d_rnn, d), lambda i: (0, 0)), pl.BlockSpec((batch, 1, d), lambda i: (0, 0, 2))],
        out_specs=pl.BlockSpec((batch, tt, d), lambda i: (0, i, 0)),
        out_shape=jax.ShapeDtypeStruct((batch, seq, d), f32),
        scratch_shapes=[pltpu.VMEM((d // LANES, tt * batch, LANES), f32)],
        compiler_params=_params("arbitrary"), name="lru_out",
    )(z, *([tokens] * batch), w_out_bf, mod)


def kernel(x, c, ctx, c_ctx, ada_w, ada_b, norm_mix, norm_ffn, pool_w, pool_scale, lru_w_in, lru_conv_w, lru_conv_b, lru_w_r, lru_b_r, lru_w_i, lru_b_i, lru_lam, lru_w_out, router_w, router_b, exp_w_gu, exp_b_gu, exp_w_down, exp_b_down, final_norm):
    batch, seq, d = x.shape
    ctx_len = ctx.shape[1]
    assert d == SUBLANES * LANES and seq % GRID_W == 0
    n_lat, n_ctx = batch * seq, batch * ctx_len
    ctx_row = batch

    mod_rows = -(-(batch + 1) // SUBLANES) * SUBLANES
    cc = jnp.zeros((mod_rows, d), f32).at[:batch].set(c).at[batch].set(c_ctx)
    mod_all = _ada_call(cc, ada_w, ada_b)
    mods = [mod_all[l].reshape(mod_rows, 1, 6 * d) for l in range(ada_w.shape[0])]

    lat = x.reshape(n_lat, d)
    cx = ctx.reshape(n_ctx, d)
    tile = _pick_tile(ROUTE_TILE, seq, n_ctx)
    row_lat = _row_map(n_lat // tile, seq // tile, ctx_row)
    gain = lambda g: g.reshape(1, d)

    hl = _norm_call(lat, gain(norm_mix[0]), mods[0], 0, 1, row_lat, tile)
    lat = _pool_lat_call(hl, lat, pool_w[0], gain(pool_scale[0]), mods[0], batch, seq)
    cx = _pool_ctx_call(cx, gain(norm_mix[0]), pool_w[0], gain(pool_scale[0]), mods[0], ctx_row, batch, ctx_len)
    tokens = jnp.concatenate([lat, cx], axis=0)
    tokens = _moe_layer(0, tokens, gain(norm_ffn[0]), mods[0], n_lat, seq, ctx_row, router_w, router_b,
                        exp_w_gu, exp_b_gu, exp_w_down, exp_b_down, gain(final_norm), False)

    assert batch == SUBLANES
    tt = _pick_tile(LRU_TILE, seq, ctx_len)
    chunk = _pick_tile(SCAN_CHUNK, seq, ctx_len)
    gg, u = _lru_in_call(tokens, gain(norm_mix[1]), mods[1], ctx_row, lru_w_in[0].astype(bf16), batch, seq,
                         ctx_len, tt)
    scan = functools.partial(_lru_scan_call, conv_w=lru_conv_w[0], conv_b=lru_conv_b[0], batch=batch, seq=seq,
                             ctx_len=ctx_len, chunk=chunk)
    hf = scan(u, None, None, w_r=lru_w_r[0, 0], b_r=lru_b_r[0, 0], w_i=lru_w_i[0, 0], b_i=lru_b_i[0, 0],
              lam=lru_lam[0, 0], reverse=False)
    z = scan(u, hf, gg, w_r=lru_w_r[0, 1], b_r=lru_b_r[0, 1], w_i=lru_w_i[0, 1], b_i=lru_b_i[0, 1],
             lam=lru_lam[0, 1], reverse=True)
    lat = _lru_out_call(z, tokens, lru_w_out[0].astype(bf16), mods[1], batch, seq, tt).reshape(n_lat, d)
    out = _moe_layer(1, lat, gain(norm_ffn[1]), mods[1], n_lat, seq, ctx_row, router_w, router_b,
                     exp_w_gu, exp_b_gu, exp_w_down, exp_b_down, gain(final_norm), True)
    return out.reshape(batch, seq, d)
```

```python
import functools

import numpy as np
import jax
import jax.numpy as jnp
from jax import lax
from jax.experimental import pallas as pl
from jax.experimental.pallas import tpu as pltpu

f32 = jnp.float32
bf16 = jnp.bfloat16
i32 = jnp.int32
i16 = jnp.int16
HIGHEST = lax.Precision.HIGHEST

LANES = 128
SUBLANES = 8
VMEM_LIMIT = 56 * 1024 * 1024

GRID_W = 64
GRID_SHIFT = 6
POOL_WINDOWS = (2, 4, 8, 16)
N_POOL_GROUPS = 4
N_LRU_BLOCKS = 4
CONV_LEFT = 2
CONV_WIDTH = 4
LRU_C = 8.0
N_EXPERTS = 32
TOP_K = 4
SWIGLU_LIMIT = 7.0
SWIGLU_ALPHA = 1.702
NORM_EPS = 1e-6
GELU_C = 0.7978845608028654

EXPERT_ROWS = 512
MLP_ROWS = 256
BUILD_ROWS = 128
ROUTE_TILE = 512
LRU_TILE = 128
SCAN_CHUNK = 64
SCAN_UNROLL = 8


def _params(*sem):
    return pltpu.CompilerParams(dimension_semantics=sem, vmem_limit_bytes=VMEM_LIMIT)


def _pick_tile(pref, *sizes):
    t = pref
    while any(s % t for s in sizes):
        t //= 2
    assert t >= SUBLANES
    return t


def _norm_mod(x, gain, scale, shift):
    ms = jnp.mean(x * x, axis=-1, keepdims=True)
    y = x * lax.rsqrt(ms + NORM_EPS)
    return (y * gain) * (1.0 + scale) + shift


def _row_map(n_lat_tiles, tiles_per_batch, ctx_row):
    def row(i):
        return jnp.where(i < n_lat_tiles, i // tiles_per_batch, ctx_row)
    return row


def _mod_spec(row, chunk, width):
    return pl.BlockSpec((None, 1, width), lambda i, *_: (row(i), 0, chunk))


def _ada_kernel(c_ref, w_ref, b_ref, o_ref):
    c = c_ref[...]
    s = c * jax.nn.sigmoid(c)
    o_ref[...] = jnp.dot(s, w_ref[...], preferred_element_type=f32, precision=HIGHEST) + b_ref[...]


def _ada_call(cc, ada_w, ada_b):
    depth, d, d6 = ada_w.shape
    rows = cc.shape[0]
    bn = 1536
    return pl.pallas_call(
        _ada_kernel, grid=(depth, d6 // bn),
        in_specs=[pl.BlockSpec((rows, d), lambda l, j: (0, 0)),
                  pl.BlockSpec((None, d, bn), lambda l, j: (l, 0, j)),
                  pl.BlockSpec((None, 1, bn), lambda l, j: (l, 0, j))],
        out_specs=pl.BlockSpec((None, rows, bn), lambda l, j: (l, 0, j)),
        out_shape=jax.ShapeDtypeStruct((depth, rows, d6), f32),
        compiler_params=_params("arbitrary", "arbitrary"), name="ada_mod",
    )(cc, ada_w, ada_b.reshape(depth, 1, d6))


def _norm_kernel(x_ref, g_ref, sc_ref, sh_ref, o_ref):
    o_ref[...] = _norm_mod(x_ref[...], g_ref[...], sc_ref[...], sh_ref[...])


def _norm_call(tokens, gain, mod, sh_chunk, sc_chunk, row, tile):
    n, d = tokens.shape
    return pl.pallas_call(
        _norm_kernel, grid=(n // tile,),
        in_specs=[pl.BlockSpec((tile, d), lambda i: (i, 0)),
                  pl.BlockSpec((1, d), lambda i: (0, 0)),
                  _mod_spec(row, sc_chunk, d), _mod_spec(row, sh_chunk, d)],
        out_specs=pl.BlockSpec((tile, d), lambda i: (i, 0)),
        out_shape=jax.ShapeDtypeStruct((n, d), f32),
        compiler_params=_params("arbitrary"), name="norm_mod",
    )(tokens, gain, mod, mod)


def _shift(x, j, unit, pos, limit):
    n = x.shape[0]
    rolled = pltpu.roll(x, (j * unit) % n, 0)
    ok = (pos >= j) if j > 0 else (pos < limit + j)
    return jnp.where(ok, rolled, 0.0)


def _window_sum(x, w, unit, pos, limit):
    m = w // 2
    trail, lead, s = x, x, 1
    while s < m:
        trail = trail + _shift(trail, s, unit, pos, limit)
        lead = lead + _shift(lead, -s, unit, pos, limit)
        s *= 2
    return _shift(trail, 1, unit, pos, limit) + lead


def _window_count(w, pos, limit):
    m = w // 2
    return jnp.minimum(pos + m, limit) - jnp.maximum(pos - m, 0)


def _pool_grid(h, w):
    t = lax.broadcasted_iota(i32, h.shape, 0)
    col, row = t & (GRID_W - 1), t >> GRID_SHIFT
    rows = h.shape[0] // GRID_W
    total = _window_sum(_window_sum(h, w, 1, col, GRID_W), w, GRID_W, row, rows)
    count = (_window_count(w, row, rows) * _window_count(w, col, GRID_W)).astype(f32)
    return total / count


def _pool_seq(h, w):
    t = lax.broadcasted_iota(i32, h.shape, 0)
    n = h.shape[0]
    return _window_sum(h, w, 1, t, n) / _window_count(w, t, n).astype(f32)


def _pool_lat_kernel(h_ref, x_ref, w_ref, ps_ref, g1_ref, o_ref):
    grp = pl.program_id(1)
    for gi, w in enumerate(POOL_WINDOWS):
        @pl.when(grp == gi)
        def _(w=w):
            h = h_ref[...]
            d = _pool_grid(h, w) - h
            y = jnp.dot(d.astype(bf16), w_ref[...].astype(bf16), preferred_element_type=f32)
            o_ref[...] = x_ref[...] + g1_ref[...] * (y * ps_ref[...])


def _pool_lat_call(h, x, pool_w, pool_scale, mod, batch, seq):
    n, d = x.shape
    cg = d // N_POOL_GROUPS
    g1_chunk0 = 2 * N_POOL_GROUPS
    return pl.pallas_call(
        _pool_lat_kernel, grid=(batch, N_POOL_GROUPS),
        in_specs=[pl.BlockSpec((seq, cg), lambda b, g: (b, g)),
                  pl.BlockSpec((seq, cg), lambda b, g: (b, g)),
                  pl.BlockSpec((None, cg, cg), lambda b, g: (g, 0, 0)),
                  pl.BlockSpec((1, cg), lambda b, g: (0, g)),
                  pl.BlockSpec((None, 1, cg), lambda b, g: (b, 0, g1_chunk0 + g))],
        out_specs=pl.BlockSpec((seq, cg), lambda b, g: (b, g)),
        out_shape=jax.ShapeDtypeStruct((n, d), f32),
        compiler_params=_params("arbitrary", "arbitrary"), name="pool_lat",
    )(h, x, pool_w, pool_scale, mod)


def _pool_ctx_kernel(x_ref, gain_ref, sc_ref, sh_ref, w_ref, ps_ref, g1_ref, o_ref):
    x = x_ref[...]
    h = _norm_mod(x, gain_ref[...], sc_ref[...], sh_ref[...])
    cg = x.shape[1] // N_POOL_GROUPS
    for gi, w in enumerate(POOL_WINDOWS):
        sl = slice(gi * cg, (gi + 1) * cg)
        hg = h[:, sl]
        d = _pool_seq(hg, w) - hg
        y = jnp.dot(d.astype(bf16), w_ref[gi].astype(bf16), preferred_element_type=f32)
        o_ref[:, sl] = x[:, sl] + g1_ref[:, sl] * (y * ps_ref[:, sl])


def _pool_ctx_call(cx, gain, pool_w, pool_scale, mod, ctx_row, batch, ctx_len):
    n, d = cx.shape
    cg = d // N_POOL_GROUPS
    row = lambda i: ctx_row
    return pl.pallas_call(
        _pool_ctx_kernel, grid=(batch,),
        in_specs=[pl.BlockSpec((ctx_len, d), lambda b: (b, 0)),
                  pl.BlockSpec((1, d), lambda b: (0, 0)),
                  _mod_spec(row, 1, d), _mod_spec(row, 0, d),
                  pl.BlockSpec((N_POOL_GROUPS, cg, cg), lambda b: (0, 0, 0)),
                  pl.BlockSpec((1, d), lambda b: (0, 0)),
                  _mod_spec(row, 2, d)],
        out_specs=pl.BlockSpec((ctx_len, d), lambda b: (b, 0)),
        out_shape=jax.ShapeDtypeStruct((n, d), f32),
        compiler_params=_params("arbitrary"), name="pool_ctx",
    )(cx, gain, mod, mod, pool_w, pool_scale, mod)


def _route_kernel(x_ref, gain_ref, sc_ref, sh_ref, wr_ref, rb_ref, loc_ref, gate_ref, cnt_ref):
    tt = x_ref.shape[0]
    h = _norm_mod(x_ref[...], gain_ref[...], sc_ref[...], sh_ref[...])

    logits = lax.dot_general(wr_ref[...], h, (((1,), (1,)), ((), ())),
                             precision=HIGHEST, preferred_element_type=f32) + rb_ref[...]
    e_iota = lax.broadcasted_iota(i32, logits.shape, 0)
    vals, onehots = [], []
    work = logits
    for k in range(TOP_K):
        m = jnp.max(work, axis=0, keepdims=True)
        idx = jnp.min(jnp.where(work == m, e_iota, N_EXPERTS), axis=0, keepdims=True)
        hit = e_iota == idx
        vals.append(m)
        onehots.append(jnp.where(hit, 1.0, 0.0))
        work = jnp.where(hit, -jnp.inf, work)

    ex = [jnp.exp(v - vals[0]) for v in vals]
    den = ex[0] + ex[1] + ex[2] + ex[3]
    for k in range(TOP_K):
        gate_ref[k:k + 1, :] = ex[k] / den

    tri = jnp.where(lax.broadcasted_iota(i32, (tt, tt), 0) <= lax.broadcasted_iota(i32, (tt, tt), 1), 1.0, 0.0)
    planes = jnp.concatenate(onehots, axis=0).astype(bf16)
    incl = jnp.dot(planes, tri.astype(bf16), preferred_element_type=f32)
    totals = [incl[k * N_EXPERTS:(k + 1) * N_EXPERTS, tt - 1:tt] for k in range(TOP_K)]
    count = jnp.broadcast_to(totals[0] + totals[1] + totals[2] + totals[3], (N_EXPERTS, LANES))
    padded = (((count.astype(i32) + (SUBLANES - 1)) // SUBLANES) * SUBLANES).astype(f32)
    below = jnp.where(lax.broadcasted_iota(i32, (N_EXPERTS, N_EXPERTS), 1)
                      < lax.broadcasted_iota(i32, (N_EXPERTS, N_EXPERTS), 0), 1.0, 0.0)
    offset = jnp.dot(below.astype(bf16), padded.astype(bf16), preferred_element_type=f32)[:, 0:1]
    for k in range(TOP_K):
        inc_k = incl[k * N_EXPERTS:(k + 1) * N_EXPERTS]
        before = inc_k - onehots[k] + offset
        loc_ref[k:k + 1, :] = jnp.sum(onehots[k] * before, axis=0, keepdims=True).astype(i32)
        offset = offset + totals[k]
    cnt_ref[...] = count


def _route_call(tokens, gain, mod, row, w_router_t, b_router, tile):
    n, d = tokens.shape
    e = w_router_t.shape[0]
    return pl.pallas_call(
        _route_kernel, grid=(n // tile,),
        in_specs=[pl.BlockSpec((tile, d), lambda i: (i, 0)),
                  pl.BlockSpec((1, d), lambda i: (0, 0)),
                  _mod_spec(row, 4, d), _mod_spec(row, 3, d),
                  pl.BlockSpec((e, d), lambda i: (0, 0)),
                  pl.BlockSpec((e, 1), lambda i: (0, 0))],
        out_specs=[pl.BlockSpec((TOP_K, tile), lambda i: (0, i)),
                   pl.BlockSpec((TOP_K, tile), lambda i: (0, i)),
                   pl.BlockSpec((None, e, LANES), lambda i: (i, 0, 0))],
        out_shape=[jax.ShapeDtypeStruct((TOP_K, n), i32),
                   jax.ShapeDtypeStruct((TOP_K, n), f32),
                   jax.ShapeDtypeStruct((n // tile, e, LANES), f32)],
        compiler_params=_params("arbitrary"), name="moe_route",
    )(tokens, gain, mod, mod, w_router_t, b_router.reshape(e, 1))


def _copy_run(src_at, dst_at, n_pieces, n_bits, sem, wait=False):
    for k in reversed(range(n_bits)):
        size = SUBLANES << k
        done = (n_pieces >> (k + 1)) << (k + 1)

        @pl.when(((n_pieces >> k) & 1) == 1)
        def _(size=size, done=done):
            off = 0 if wait else pl.multiple_of(done * SUBLANES, SUBLANES)
            copy = pltpu.make_async_copy(src_at(off, size), dst_at(off, size), sem)
            copy.wait() if wait else copy.start()


def _dispatch_kernel(go_ref, lb_ref, p8_ref, np_ref, ts_ref, tn_ref, nu_ref,
                     x_ref, gain_ref, sc_ref, sh_ref, loc_ref, xs_hbm, xloc, zbuf, sel_s, sem, *, n_tiles, n_blocks):
    i = pl.program_id(0)
    tt = x_ref.shape[0]
    rl = xloc.shape[1]
    slot = i % 2

    def build(loc, dst):
        rows = [loc[k:k + 1, :].astype(i16) for k in range(TOP_K)]
        one, zero = jnp.ones((), bf16), jnp.zeros((), bf16)
        for r0 in range(0, rl, BUILD_ROWS):
            r_iota = lax.broadcasted_iota(i16, (BUILD_ROWS, tt), 0) + r0
            sel = None
            for k in range(TOP_K):
                m = jnp.where(r_iota == rows[k], one, zero)
                sel = m if sel is None else sel + m
            dst[r0:r0 + BUILD_ROWS, :] = sel

    build(loc_ref, sel_s)
    h = _norm_mod(x_ref[...], gain_ref[...], sc_ref[...], sh_ref[...]).astype(bf16)
    xloc[slot] = jnp.dot(sel_s[...], h, preferred_element_type=f32)

    seg_bits = (tt // SUBLANES).bit_length()
    tile_bits = (rl // SUBLANES).bit_length()
    buf_at = lambda off, size: xloc.at[0, pl.ds(off, size), :]
    hbm_at = lambda off, size: xs_hbm.at[pl.ds(off, size), :]

    @pl.when(i > 0)
    def _():
        _copy_run(buf_at, hbm_at, np_ref[jnp.maximum(i - 1, 0)], tile_bits, sem.at[1 - slot], wait=True)

    def per_expert(e, carry):
        idx = i * N_EXPERTS + e
        src0 = pl.multiple_of(lb_ref[idx], SUBLANES)
        dst0 = pl.multiple_of(go_ref[idx], SUBLANES)
        _copy_run(lambda off, size: xloc.at[slot, pl.ds(src0 + off, size), :],
                  lambda off, size: xs_hbm.at[pl.ds(dst0 + off, size), :],
                  p8_ref[idx] // SUBLANES, seg_bits, sem.at[slot])
        return carry
    lax.fori_loop(0, N_EXPERTS, per_expert, 0)

    @pl.when(i == n_tiles - 1)
    def _():
        _copy_run(buf_at, hbm_at, np_ref[i], tile_bits, sem.at[slot], wait=True)
        zbuf[...] = jnp.zeros(zbuf.shape, f32)
        tail_bits = (EXPERT_ROWS // SUBLANES - 1).bit_length()
        zero_at = lambda off, size: zbuf.at[pl.ds(0, size), :]

        def tail(e, carry):
            t0 = pl.multiple_of(ts_ref[e], SUBLANES)
            _copy_run(zero_at, lambda off, size: xs_hbm.at[pl.ds(t0 + off, size), :], tn_ref[e], tail_bits,
                      sem.at[slot])
            return carry
        lax.fori_loop(0, N_EXPERTS, tail, 0)

        def tail_wait(e, carry):
            _copy_run(zero_at, hbm_at, tn_ref[e], tail_bits, sem.at[slot], wait=True)
            return carry
        lax.fori_loop(0, N_EXPERTS, tail_wait, 0)

        def spare(b, carry):
            t = pl.multiple_of(b * EXPERT_ROWS, EXPERT_ROWS)
            pltpu.make_async_copy(zbuf, xs_hbm.at[pl.ds(t, EXPERT_ROWS), :], sem.at[slot]).start()
            return carry
        lax.fori_loop(nu_ref[0], n_blocks, spare, 0)

        def spare_wait(b, carry):
            pltpu.make_async_copy(zbuf, xs_hbm.at[pl.ds(0, EXPERT_ROWS), :], sem.at[slot]).wait()
            return carry
        lax.fori_loop(nu_ref[0], n_blocks, spare_wait, 0)


def _dispatch_call(tables, tokens, gain, mod, row, loc, n_rows, tile):
    n, d = tokens.shape
    n_tiles = n // tile
    rl = TOP_K * tile + N_EXPERTS * SUBLANES
    spec = lambda shape, fn: pl.BlockSpec(shape, lambda i, *_: fn(i))
    grid_spec = pltpu.PrefetchScalarGridSpec(
        num_scalar_prefetch=7, grid=(n_tiles,),
        in_specs=[spec((tile, d), lambda i: (i, 0)),
                  spec((1, d), lambda i: (0, 0)),
                  _mod_spec(row, 4, d), _mod_spec(row, 3, d),
                  spec((TOP_K, tile), lambda i: (0, i))],
        out_specs=pl.BlockSpec(memory_space=pl.ANY),
        scratch_shapes=[pltpu.VMEM((2, rl, d), f32), pltpu.VMEM((EXPERT_ROWS, d), f32),
                        pltpu.VMEM((rl, tile), bf16), pltpu.SemaphoreType.DMA((2,))])
    return pl.pallas_call(
        functools.partial(_dispatch_kernel, n_tiles=n_tiles, n_blocks=n_rows // EXPERT_ROWS), grid_spec=grid_spec,
        out_shape=jax.ShapeDtypeStruct((n_rows, d), f32),
        compiler_params=_params("arbitrary"), name="moe_dispatch",
    )(*tables, tokens, gain, mod, mod, loc)


def _gmm_kernel(be_ref, bv_ref, lu_ref, x_ref, wgu_ref, bgu_ref, wd_ref, bd_ref, o_ref, wgu_bf, wd_bf):
    j = pl.program_id(0)
    d_ff = wd_ref.shape[0]
    used = bv_ref[j] > 0

    @pl.when(used)
    def _():
        new_expert = jnp.logical_or(j == 0, be_ref[j] != be_ref[jnp.maximum(j - 1, 0)])

        @pl.when(new_expert)
        def _():
            wgu_bf[...] = wgu_ref[...].astype(bf16)
            wd_bf[...] = wd_ref[...].astype(bf16)

        for r0 in range(0, x_ref.shape[0], MLP_ROWS):
            rs = slice(r0, r0 + MLP_ROWS)
            x = x_ref[rs, :].astype(bf16)
            gu = jnp.dot(x, wgu_bf[...], preferred_element_type=f32) + bgu_ref[...]
            g = jnp.minimum(gu[:, :d_ff], SWIGLU_LIMIT)
            u = jnp.clip(gu[:, d_ff:], -SWIGLU_LIMIT, SWIGLU_LIMIT)
            act = (u + 1.0) * g * jax.nn.sigmoid(SWIGLU_ALPHA * g)
            o_ref[rs, :] = jnp.dot(act.astype(bf16), wd_bf[...], preferred_element_type=f32) + bd_ref[...]

    @pl.when(jnp.logical_not(used))
    def _():
        o_ref[...] = jnp.zeros(o_ref.shape, f32)


def _gmm_call(layer, blk_e, blk_valid, last_used, xs, w_gu, b_gu, w_down, b_down):
    depth, e, d, f2 = w_gu.shape
    d_ff = w_down.shape[2]
    rows = EXPERT_ROWS
    n_blocks = xs.shape[0] // rows
    grid_spec = pltpu.PrefetchScalarGridSpec(
        num_scalar_prefetch=3, grid=(n_blocks,),
        in_specs=[pl.BlockSpec((rows, d), lambda j, be, bv, lu: (jnp.minimum(j, lu[0]), 0)),
                  pl.BlockSpec((None, None, d, f2), lambda j, be, bv, lu: (layer, be[j], 0, 0)),
                  pl.BlockSpec((None, None, 1, f2), lambda j, be, bv, lu: (layer, be[j], 0, 0)),
                  pl.BlockSpec((None, None, d_ff, d), lambda j, be, bv, lu: (layer, be[j], 0, 0)),
                  pl.BlockSpec((None, None, 1, d), lambda j, be, bv, lu: (layer, be[j], 0, 0))],
        out_specs=pl.BlockSpec((rows, d), lambda j, be, bv, lu: (j, 0)),
        scratch_shapes=[pltpu.VMEM((d, f2), bf16), pltpu.VMEM((d_ff, d), bf16)])
    return pl.pallas_call(
        _gmm_kernel, grid_spec=grid_spec,
        out_shape=jax.ShapeDtypeStruct((n_blocks * rows, d), f32),
        compiler_params=_params("arbitrary"), name="moe_experts",
    )(blk_e, blk_valid, last_used, xs, w_gu, b_gu.reshape(depth, e, 1, f2), w_down, b_down.reshape(depth, e, 1, d))


def _combine_kernel(go_ref, lb_ref, p8_ref, np_ref,
                    ys_hbm, loc_ref, gate_ref, x_ref, g2_ref, fn_ref, o_ref, yloc, w_s, sem, *, n_tiles, final):
    i = pl.program_id(0)
    tt = x_ref.shape[0]
    rl = yloc.shape[1]
    slot = i % 2

    def build(loc, gate, dst):
        cols = [(loc[:, k:k + 1].astype(i16), gate[:, k:k + 1].astype(bf16)) for k in range(TOP_K)]
        zero = jnp.zeros((), bf16)
        for c0 in range(0, rl, BUILD_ROWS):
            r_iota = lax.broadcasted_iota(i16, (tt, BUILD_ROWS), 1) + c0
            weights = None
            for at, g in cols:
                m = jnp.where(r_iota == at, g, zero)
                weights = m if weights is None else weights + m
            dst[:, c0:c0 + BUILD_ROWS] = weights

    seg_bits = (tt // SUBLANES).bit_length()
    tile_bits = (rl // SUBLANES).bit_length()

    def fetch(tile, buf):
        def per_expert(e, carry):
            idx = tile * N_EXPERTS + e
            src0 = pl.multiple_of(go_ref[idx], SUBLANES)
            dst0 = pl.multiple_of(lb_ref[idx], SUBLANES)
            _copy_run(lambda off, size: ys_hbm.at[pl.ds(src0 + off, size), :],
                      lambda off, size: yloc.at[buf, pl.ds(dst0 + off, size), :],
                      p8_ref[idx] // SUBLANES, seg_bits, sem.at[buf])
            return carry
        lax.fori_loop(0, N_EXPERTS, per_expert, 0)

    @pl.when(i == 0)
    def _():
        yloc[...] = jnp.zeros(yloc.shape, f32)
        fetch(0, 0)

    @pl.when(i + 1 < n_tiles)
    def _():
        fetch(jnp.minimum(i + 1, n_tiles - 1), 1 - slot)

    _copy_run(lambda off, size: ys_hbm.at[pl.ds(off, size), :], lambda off, size: yloc.at[0, pl.ds(off, size), :],
              np_ref[i], tile_bits, sem.at[slot], wait=True)

    build(loc_ref, gate_ref, w_s)
    y = jnp.dot(w_s[...], yloc[slot].astype(bf16), preferred_element_type=f32)
    out = x_ref[...] + g2_ref[...] * y
    if final:
        ms = jnp.mean(out * out, axis=-1, keepdims=True)
        out = (out * lax.rsqrt(ms + NORM_EPS)) * fn_ref[...]
    o_ref[...] = out


def _combine_call(tables, ys, loc_t, gates_t, tokens, mod, row, final_gain, tile, final):
    n, d = tokens.shape
    n_tiles = n // tile
    rl = TOP_K * tile + N_EXPERTS * SUBLANES
    spec = lambda shape, fn: pl.BlockSpec(shape, lambda i, *_: fn(i))
    grid_spec = pltpu.PrefetchScalarGridSpec(
        num_scalar_prefetch=4, grid=(n_tiles,),
        in_specs=[pl.BlockSpec(memory_space=pl.ANY),
                  spec((tile, TOP_K), lambda i: (i, 0)),
                  spec((tile, TOP_K), lambda i: (i, 0)),
                  spec((tile, d), lambda i: (i, 0)),
                  _mod_spec(row, 5, d),
                  spec((1, d), lambda i: (0, 0))],
        out_specs=spec((tile, d), lambda i: (i, 0)),
        scratch_shapes=[pltpu.VMEM((2, rl, d), f32), pltpu.VMEM((tile, rl), bf16),
                        pltpu.SemaphoreType.DMA((2,))])
    return pl.pallas_call(
        functools.partial(_combine_kernel, n_tiles=n_tiles, final=final), grid_spec=grid_spec,
        out_shape=jax.ShapeDtypeStruct((n, d), f32),
        compiler_params=_params("arbitrary"), name="moe_combine",
    )(*tables, ys, loc_t, gates_t, tokens, mod, final_gain)


def _moe_layer(layer, tokens, gain, mod, n_lat, seq, ctx_row, w_router, b_router, w_gu, b_gu, w_down, b_down,
               final_gain, final):
    n, d = tokens.shape
    tile = _pick_tile(ROUTE_TILE, n_lat, n - n_lat if n > n_lat else n_lat, seq)
    n_tiles = n // tile
    row = _row_map(n_lat // tile, seq // tile, ctx_row)

    loc, gate, cnt = _route_call(tokens, gain, mod, row, w_router[layer].T, b_router[layer], tile)

    rows = EXPERT_ROWS
    count = cnt[:, :, 0].astype(i32)
    seg = (count + SUBLANES - 1) // SUBLANES * SUBLANES
    local_base = jnp.cumsum(seg, axis=1) - seg
    pieces = jnp.sum(seg, axis=1) // SUBLANES
    expert_rows = jnp.sum(seg, axis=0)
    region = (expert_rows + rows - 1) // rows * rows
    region_end = jnp.cumsum(region)
    region_start = region_end - region
    global_off = region_start[None, :] + jnp.cumsum(seg, axis=0) - seg
    n_blocks = (n * TOP_K + n_tiles * N_EXPERTS * (SUBLANES - 1) + N_EXPERTS * (rows - 1) + rows - 1) // rows
    blk_start = jnp.arange(n_blocks, dtype=i32) * rows
    experts = jnp.arange(N_EXPERTS, dtype=i32)
    blk_e = jnp.minimum(jnp.sum(region_end[None, :] <= blk_start[:, None], axis=1), N_EXPERTS - 1).astype(i32)
    content_end = region_start + expert_rows
    blk_end = jnp.sum(jnp.where(blk_e[:, None] == experts[None, :], content_end[None, :], 0), axis=1)
    blk_valid = (blk_start < blk_end).astype(i32)
    n_used = (region_end[-1:] // rows).astype(i32)
    last_used = jnp.maximum(n_used - 1, 0)
    tail_n = ((region - expert_rows) // SUBLANES).astype(i32)
    flat = lambda a: a.reshape(-1).astype(i32)
    tables = (flat(global_off), flat(local_base), flat(seg), pieces.astype(i32))

    xs = _dispatch_call(tables + (content_end.astype(i32), tail_n, n_used), tokens, gain, mod, row, loc,
                        n_blocks * rows, tile)
    ys = _gmm_call(layer, blk_e, blk_valid, last_used, xs, w_gu, b_gu, w_down, b_down)
    return _combine_call(tables, ys, loc.T, gate.T, tokens, mod, row, final_gain, tile, final)


def _time_tile_blocks(n_lat_t, n_ctx_t, n_lat_blocks, b):
    def index(i):
        lat_blk = b * n_lat_t + jnp.minimum(i, n_lat_t - 1)
        ctx_blk = n_lat_blocks + b * n_ctx_t + jnp.maximum(i - n_lat_t, 0)
        return (jnp.where(i < n_lat_t, lat_blk, ctx_blk), 0)
    return index


def _lru_in_kernel(*refs, batch, n_lat_t):
    x_refs = refs[:batch]
    gain_ref, sc_ref, sh_ref, scc_ref, shc_ref, w_ref, gg_ref, u_ref, tm_s = refs[batch:]
    tt = x_refs[0].shape[0]
    d_rnn = gg_ref.shape[1]
    is_ctx = pl.program_id(0) >= n_lat_t
    hs = []
    for b in range(batch):
        scale = jnp.where(is_ctx, scc_ref[...], sc_ref[b])
        shift = jnp.where(is_ctx, shc_ref[...], sh_ref[b])
        hs.append(_norm_mod(x_refs[b][...], gain_ref[...], scale, shift).astype(bf16))
    gu = jnp.dot(jnp.concatenate(hs, axis=0), w_ref[...], preferred_element_type=f32)
    gate = gu[:, :d_rnn]
    gelu = 0.5 * gate * (1.0 + jnp.tanh(GELU_C * (gate + 0.044715 * (gate * gate * gate))))
    for out_ref, val in ((gg_ref, gelu), (u_ref, gu[:, d_rnn:])):
        for b in range(batch):
            for s in range(d_rnn // LANES):
                tm_s[s, pl.ds(b, tt, stride=batch), :] = val[b * tt:(b + 1) * tt, s * LANES:(s + 1) * LANES]
        for s in range(d_rnn // LANES):
            out_ref[:, s * LANES:(s + 1) * LANES] = tm_s[s]


def _lru_in_call(tokens, gain, mod, ctx_row, w_in_bf, batch, seq, ctx_len, tt):
    n, d = tokens.shape
    d2 = w_in_bf.shape[1]
    d_rnn = d2 // 2
    n_lat_t, n_ctx_t = seq // tt, ctx_len // tt
    n_lat_blocks = batch * n_lat_t
    x_specs = [pl.BlockSpec((tt, d), _time_tile_blocks(n_lat_t, n_ctx_t, n_lat_blocks, b)) for b in range(batch)]
    mod_lat = lambda chunk: pl.BlockSpec((batch, 1, d), lambda i: (0, 0, chunk))
    mod_ctx = lambda chunk: pl.BlockSpec((None, 1, d), lambda i: (ctx_row, 0, chunk))
    out_spec = pl.BlockSpec((tt * batch, d_rnn), lambda i: (i, 0))
    out_shape = jax.ShapeDtypeStruct(((seq + ctx_len) * batch, d_rnn), f32)
    return pl.pallas_call(
        functools.partial(_lru_in_kernel, batch=batch, n_lat_t=n_lat_t), grid=(n_lat_t + n_ctx_t,),
        in_specs=x_specs + [pl.BlockSpec((1, d), lambda i: (0, 0)), mod_lat(1), mod_lat(0), mod_ctx(1), mod_ctx(0),
                            pl.BlockSpec((d, d2), lambda i: (0, 0))],
        out_specs=[out_spec, out_spec], out_shape=[out_shape, out_shape],
        scratch_shapes=[pltpu.VMEM((d_rnn // LANES, tt * batch, LANES), f32)],
        compiler_params=_params("arbitrary"), name="lru_in",
    )(*([tokens] * batch), gain, mod, mod, mod, mod, w_in_bf)


def _lru_scan_kernel(*refs, batch, n_ctx_c, n_lat_c, reverse):
    if reverse:
        (prev_ref, cur_ref, next_ref, cw_ref, cb_ref, wr_ref, br_ref, wi_ref, bi_ref, lam_ref, hf_ref, gg_ref,
         o_ref, a_s, b_s, h_s, z_s) = refs
    else:
        (prev_ref, cur_ref, next_ref, cw_ref, cb_ref, wr_ref, br_ref, wi_ref, bi_ref, lam_ref,
         o_ref, a_s, b_s, h_s) = refs
    s = pl.program_id(0)
    rows, d_rnn = cur_ref.shape
    steps = rows // batch
    wb = d_rnn // N_LRU_BLOCKS

    @pl.when(s == 0)
    def _():
        h_s[...] = jnp.zeros(h_s.shape, f32)

    c = jnp.where(s < n_ctx_c, s, s - n_ctx_c)
    seg_chunks = jnp.where(s < n_ctx_c, n_ctx_c, n_lat_c)
    c = (seg_chunks - 1 - c) if reverse else c
    has_prev = c > 0
    has_next = c < seg_chunks - 1
    ext = jnp.concatenate([jnp.where(has_prev, prev_ref[...], 0.0), cur_ref[...],
                           jnp.where(has_next, next_ref[...], 0.0)], axis=0)
    u = cb_ref[...]
    for k in range(CONV_WIDTH):
        u = u + ext[k * batch:k * batch + rows] * cw_ref[k:k + 1, :]

    lam = lam_ref[...]
    softplus_neg = jnp.maximum(-lam, 0.0) + jnp.log1p(jnp.exp(-jnp.abs(lam)))
    for n in range(N_LRU_BLOCKS):
        cols = slice(n * wb, (n + 1) * wb)
        un = u[:, cols]
        ub = un.astype(bf16)
        pre_r = jnp.dot(ub, wr_ref[n].astype(bf16), preferred_element_type=f32) + br_ref[:, cols]
        pre_i = jnp.dot(ub, wi_ref[n].astype(bf16), preferred_element_type=f32) + bi_ref[:, cols]
        r = 0.5 * (1.0 + jnp.tanh(0.5 * pre_r))
        gi = 0.5 * (1.0 + jnp.tanh(0.5 * pre_i))
        log_a = (-LRU_C * r) * softplus_neg[:, cols]
        a = jnp.exp(log_a)
        mult = jnp.sqrt(-jnp.tanh(log_a) * (a * a + 1.0))
        a_s[:, cols] = a
        b_s[:, cols] = mult * (gi * un)

    def step(q, h):
        for j in range(SCAN_UNROLL):
            t = q * SCAN_UNROLL + j
            t = (steps - 1 - t) if reverse else t
            row = pl.multiple_of(t * batch, batch)
            h = a_s[pl.ds(row, batch), :] * h + b_s[pl.ds(row, batch), :]
            if reverse:
                z_s[pl.ds(row, batch), :] = h
            else:
                o_ref[pl.ds(row, batch), :] = h
        return h
    h_s[...] = lax.fori_loop(0, steps // SCAN_UNROLL, step, h_s[...])
    if reverse:
        o_ref[...] = (gg_ref[...] * (hf_ref[...] + z_s[...])).astype(o_ref.dtype)


def _lru_scan_call(u, hf, gg, conv_w, conv_b, w_r, b_r, w_i, b_i, lam, batch, seq, ctx_len, chunk, reverse):
    n_rows, d_rnn = u.shape
    rows = chunk * batch
    n_lat_c, n_ctx_c = seq // chunk, ctx_len // chunk
    pre, post = CONV_LEFT * batch, (CONV_WIDTH - 1 - CONV_LEFT) * batch
    wb = d_rnn // N_LRU_BLOCKS

    def blk(s):
        ci = jnp.where(s < n_ctx_c, s, s - n_ctx_c)
        if reverse:
            ci = jnp.where(s < n_ctx_c, n_ctx_c - 1 - ci, n_lat_c - 1 - ci)
        return jnp.where(s < n_ctx_c, n_lat_c + ci, ci)

    last_post = n_rows // post - 1
    cur = pl.BlockSpec((rows, d_rnn), lambda s: (blk(s), 0))
    in_specs = [pl.BlockSpec((pre, d_rnn), lambda s: (jnp.maximum(blk(s) * (rows // pre) - 1, 0), 0)),
                cur,
                pl.BlockSpec((post, d_rnn), lambda s: (jnp.minimum((blk(s) + 1) * (rows // post), last_post), 0)),
                pl.BlockSpec((CONV_WIDTH, d_rnn), lambda s: (0, 0)),
                pl.BlockSpec((1, d_rnn), lambda s: (0, 0)),
                pl.BlockSpec((N_LRU_BLOCKS, wb, wb), lambda s: (0, 0, 0)),
                pl.BlockSpec((1, d_rnn), lambda s: (0, 0)),
                pl.BlockSpec((N_LRU_BLOCKS, wb, wb), lambda s: (0, 0, 0)),
                pl.BlockSpec((1, d_rnn), lambda s: (0, 0)),
                pl.BlockSpec((1, d_rnn), lambda s: (0, 0))]
    args = [u, u, u, conv_w, conv_b.reshape(1, d_rnn), w_r, b_r.reshape(1, d_rnn), w_i, b_i.reshape(1, d_rnn),
            lam.reshape(1, d_rnn)]
    scratch = [pltpu.VMEM((rows, d_rnn), f32), pltpu.VMEM((rows, d_rnn), f32), pltpu.VMEM((batch, d_rnn), f32)]
    if reverse:
        in_specs += [cur, cur]
        args += [hf, gg]
        scratch.append(pltpu.VMEM((rows, d_rnn), f32))
    return pl.pallas_call(
        functools.partial(_lru_scan_kernel, batch=batch, n_ctx_c=n_ctx_c, n_lat_c=n_lat_c, reverse=reverse),
        grid=(n_ctx_c + n_lat_c,), in_specs=in_specs, out_specs=cur,
        out_shape=jax.ShapeDtypeStruct((n_rows, d_rnn), bf16 if reverse else f32),
        scratch_shapes=scratch, compiler_params=_params("arbitrary"),
        name="lru_scan_rev" if reverse else "lru_scan_fwd",
    )(*args)


def _lru_out_kernel(*refs, batch):
    z_ref = refs[0]
    x_refs = refs[1:1 + batch]
    w_ref, g1_ref, o_ref, y_s = refs[1 + batch:]
    tt = x_refs[0].shape[0]
    y = jnp.dot(z_ref[...], w_ref[...], preferred_element_type=f32)
    n_planes = y.shape[1] // LANES
    for s in range(n_planes):
        y_s[s] = y[:, s * LANES:(s + 1) * LANES]
    for b in range(batch):
        y_b = jnp.concatenate([y_s[s, pl.ds(b, tt, stride=batch), :] for s in range(n_planes)], axis=-1)
        o_ref[b] = x_refs[b][...] + g1_ref[b] * y_b


def _lru_out_call(z, tokens, w_out_bf, mod, batch, seq, tt):
    d = tokens.shape[1]
    d_rnn = z.shape[1]
    n_t = seq // tt
    x_specs = [pl.BlockSpec((tt, d), lambda i, b=b: (b * n_t + i, 0)) for b in range(batch)]
    return pl.pallas_call(
        functools.partial(_lru_out_kernel, batch=batch), grid=(n_t,),
        in_specs=[pl.BlockSpec((tt * batch, d_rnn), lambda i: (i, 0))] + x_specs
                 + [pl.BlockSpec((d_rnn, d), lambda i: (0, 0)), pl.BlockSpec((batch, 1, d), lambda i: (0, 0, 2))],
        out_specs=pl.BlockSpec((batch, tt, d), lambda i: (0, i, 0)),
        out_shape=jax.ShapeDtypeStruct((batch, seq, d), f32),
        scratch_shapes=[pltpu.VMEM((d // LANES, tt * batch, LANES), f32)],
        compiler_params=_params("arbitrary"), name="lru_out",
    )(z, *([tokens] * batch), w_out_bf, mod)


def kernel(x, c, ctx, c_ctx, ada_w, ada_b, norm_mix, norm_ffn, pool_w, pool_scale, lru_w_in, lru_conv_w, lru_conv_b, lru_w_r, lru_b_r, lru_w_i, lru_b_i, lru_lam, lru_w_out, router_w, router_b, exp_w_gu, exp_b_gu, exp_w_down, exp_b_down, final_norm):
    batch, seq, d = x.shape
    ctx_len = ctx.shape[1]
    assert d == SUBLANES * LANES and seq % GRID_W == 0
    n_lat, n_ctx = batch * seq, batch * ctx_len
    ctx_row = batch

    mod_rows = -(-(batch + 1) // SUBLANES) * SUBLANES
    cc = jnp.zeros((mod_rows, d), f32).at[:batch].set(c).at[batch].set(c_ctx)
    mod_all = _ada_call(cc, ada_w, ada_b)
    mods = [mod_all[l].reshape(mod_rows, 1, 6 * d) for l in range(ada_w.shape[0])]

    lat = x.reshape(n_lat, d)
    cx = ctx.reshape(n_ctx, d)
    tile = _pick_tile(ROUTE_TILE, seq, n_ctx)
    row_lat = _row_map(n_lat // tile, seq // tile, ctx_row)
    gain = lambda g: g.reshape(1, d)

    hl = _norm_call(lat, gain(norm_mix[0]), mods[0], 0, 1, row_lat, tile)
    lat = _pool_lat_call(hl, lat, pool_w[0], gain(pool_scale[0]), mods[0], batch, seq)
    cx = _pool_ctx_call(cx, gain(norm_mix[0]), pool_w[0], gain(pool_scale[0]), mods[0], ctx_row, batch, ctx_len)
    tokens = jnp.concatenate([lat, cx], axis=0)
    tokens = _moe_layer(0, tokens, gain(norm_ffn[0]), mods[0], n_lat, seq, ctx_row, router_w, router_b,
                        exp_w_gu, exp_b_gu, exp_w_down, exp_b_down, gain(final_norm), False)

    assert batch == SUBLANES
    tt = _pick_tile(LRU_TILE, seq, ctx_len)
    chunk = _pick_tile(SCAN_CHUNK, seq, ctx_len)
    gg, u = _lru_in_call(tokens, gain(norm_mix[1]), mods[1], ctx_row, lru_w_in[0].astype(bf16), batch, seq,
                         ctx_len, tt)
    scan = functools.partial(_lru_scan_call, conv_w=lru_conv_w[0], conv_b=lru_conv_b[0], batch=batch, seq=seq,
                             ctx_len=ctx_len, chunk=chunk)
    hf = scan(u, None, None, w_r=lru_w_r[0, 0], b_r=lru_b_r[0, 0], w_i=lru_w_i[0, 0], b_i=lru_b_i[0, 0],
              lam=lru_lam[0, 0], reverse=False)
    z = scan(u, hf, gg, w_r=lru_w_r[0, 1], b_r=lru_b_r[0, 1], w_i=lru_w_i[0, 1], b_i=lru_b_i[0, 1],
             lam=lru_lam[0, 1], reverse=True)
    lat = _lru_out_call(z, tokens, lru_w_out[0].astype(bf16), mods[1], batch, seq, tt).reshape(n_lat, d)
    out = _moe_layer(1, lat, gain(norm_ffn[1]), mods[1], n_lat, seq, ctx_row, router_w, router_b,
                     exp_w_gu, exp_b_gu, exp_w_down, exp_b_down, gain(final_norm), True)
    return out.reshape(batch, seq, d)
```

```python
import functools

import numpy as np
import jax
import jax.numpy as jnp
from jax import lax
from jax.experimental import pallas as pl
from jax.experimental.pallas import tpu as pltpu

f32 = jnp.float32
bf16 = jnp.bfloat16
i32 = jnp.int32
i16 = jnp.int16
HIGHEST = lax.Precision.HIGHEST

LANES = 128
SUBLANES = 8
VMEM_LIMIT = 56 * 1024 * 1024

GRID_W = 64
GRID_SHIFT = 6
POOL_WINDOWS = (2, 4, 8, 16)
N_POOL_GROUPS = 4
N_LRU_BLOCKS = 4
CONV_LEFT = 2
CONV_WIDTH = 4
LRU_C = 8.0
N_EXPERTS = 32
TOP_K = 4
SWIGLU_LIMIT = 7.0
SWIGLU_ALPHA = 1.702
NORM_EPS = 1e-6
GELU_C = 0.7978845608028654

EXPERT_ROWS = 512
MLP_ROWS = 256
BUILD_ROWS = 128
ROUTE_TILE = 512
LRU_TILE = 128
SCAN_CHUNK = 64
SCAN_UNROLL = 8


def _params(*sem):
    return pltpu.CompilerParams(dimension_semantics=sem, vmem_limit_bytes=VMEM_LIMIT)


def _pick_tile(pref, *sizes):
    t = pref
    while any(s % t for s in sizes):
        t //= 2
    assert t >= SUBLANES
    return t


def _norm_mod(x, gain, scale, shift):
    ms = jnp.mean(x * x, axis=-1, keepdims=True)
    y = x * lax.rsqrt(ms + NORM_EPS)
    return (y * gain) * (1.0 + scale) + shift


def _row_map(n_lat_tiles, tiles_per_batch, ctx_row):
    def row(i):
        return jnp.where(i < n_lat_tiles, i // tiles_per_batch, ctx_row)
    return row


def _mod_spec(row, chunk, width):
    return pl.BlockSpec((None, 1, width), lambda i, *_: (row(i), 0, chunk))


def _token_specs(tile, d, n_lat_tiles):
    return [pl.BlockSpec((tile, d), lambda i, *_: (jnp.minimum(i, n_lat_tiles - 1), 0)),
            pl.BlockSpec((tile, d), lambda i, *_: (jnp.maximum(i - n_lat_tiles, 0), 0))]


def _token_tile(x_ref, xc_ref, n_lat_tiles):
    return jnp.where(pl.program_id(0) >= n_lat_tiles, xc_ref[...], x_ref[...])


def _ada_kernel(c_ref, w_ref, b_ref, o_ref):
    c = c_ref[...]
    s = c * jax.nn.sigmoid(c)
    o_ref[...] = jnp.dot(s, w_ref[...], preferred_element_type=f32, precision=HIGHEST) + b_ref[...]


def _ada_call(cc, ada_w, ada_b):
    depth, d, d6 = ada_w.shape
    rows = cc.shape[0]
    bn = 1536
    return pl.pallas_call(
        _ada_kernel, grid=(depth, d6 // bn),
        in_specs=[pl.BlockSpec((rows, d), lambda l, j: (0, 0)),
                  pl.BlockSpec((None, d, bn), lambda l, j: (l, 0, j)),
                  pl.BlockSpec((None, 1, bn), lambda l, j: (l, 0, j))],
        out_specs=pl.BlockSpec((None, rows, bn), lambda l, j: (l, 0, j)),
        out_shape=jax.ShapeDtypeStruct((depth, rows, d6), f32),
        compiler_params=_params("arbitrary", "arbitrary"), name="ada_mod",
    )(cc, ada_w, ada_b.reshape(depth, 1, d6))


def _norm_kernel(x_ref, g_ref, sc_ref, sh_ref, o_ref):
    o_ref[...] = _norm_mod(x_ref[...], g_ref[...], sc_ref[...], sh_ref[...])


def _norm_call(tokens, gain, mod, sh_chunk, sc_chunk, row, tile):
    n, d = tokens.shape
    return pl.pallas_call(
        _norm_kernel, grid=(n // tile,),
        in_specs=[pl.BlockSpec((tile, d), lambda i: (i, 0)),
                  pl.BlockSpec((1, d), lambda i: (0, 0)),
                  _mod_spec(row, sc_chunk, d), _mod_spec(row, sh_chunk, d)],
        out_specs=pl.BlockSpec((tile, d), lambda i: (i, 0)),
        out_shape=jax.ShapeDtypeStruct((n, d), f32),
        compiler_params=_params("arbitrary"), name="norm_mod",
    )(tokens, gain, mod, mod)


def _shift(x, j, unit, pos, limit):
    n = x.shape[0]
    rolled = pltpu.roll(x, (j * unit) % n, 0)
    ok = (pos >= j) if j > 0 else (pos < limit + j)
    return jnp.where(ok, rolled, 0.0)


def _window_sum(x, w, unit, pos, limit):
    m = w // 2
    trail, lead, s = x, x, 1
    while s < m:
        trail = trail + _shift(trail, s, unit, pos, limit)
        lead = lead + _shift(lead, -s, unit, pos, limit)
        s *= 2
    return _shift(trail, 1, unit, pos, limit) + lead


def _window_count(w, pos, limit):
    m = w // 2
    return jnp.minimum(pos + m, limit) - jnp.maximum(pos - m, 0)


def _pool_grid(h, w):
    t = lax.broadcasted_iota(i32, h.shape, 0)
    col, row = t & (GRID_W - 1), t >> GRID_SHIFT
    rows = h.shape[0] // GRID_W
    total = _window_sum(_window_sum(h, w, 1, col, GRID_W), w, GRID_W, row, rows)
    count = (_window_count(w, row, rows) * _window_count(w, col, GRID_W)).astype(f32)
    return total / count


def _pool_seq(h, w):
    t = lax.broadcasted_iota(i32, h.shape, 0)
    n = h.shape[0]
    return _window_sum(h, w, 1, t, n) / _window_count(w, t, n).astype(f32)


def _pool_lat_kernel(h_ref, x_ref, w_ref, ps_ref, g1_ref, o_ref):
    grp = pl.program_id(1)
    for gi, w in enumerate(POOL_WINDOWS):
        @pl.when(grp == gi)
        def _(w=w):
            h = h_ref[...]
            d = _pool_grid(h, w) - h
            y = jnp.dot(d.astype(bf16), w_ref[...].astype(bf16), preferred_element_type=f32)
            o_ref[...] = x_ref[...] + g1_ref[...] * (y * ps_ref[...])


def _pool_lat_call(h, x, pool_w, pool_scale, mod, batch, seq):
    n, d = x.shape
    cg = d // N_POOL_GROUPS
    g1_chunk0 = 2 * N_POOL_GROUPS
    return pl.pallas_call(
        _pool_lat_kernel, grid=(batch, N_POOL_GROUPS),
        in_specs=[pl.BlockSpec((seq, cg), lambda b, g: (b, g)),
                  pl.BlockSpec((seq, cg), lambda b, g: (b, g)),
                  pl.BlockSpec((None, cg, cg), lambda b, g: (g, 0, 0)),
                  pl.BlockSpec((1, cg), lambda b, g: (0, g)),
                  pl.BlockSpec((None, 1, cg), lambda b, g: (b, 0, g1_chunk0 + g))],
        out_specs=pl.BlockSpec((seq, cg), lambda b, g: (b, g)),
        out_shape=jax.ShapeDtypeStruct((n, d), f32),
        compiler_params=_params("arbitrary", "arbitrary"), name="pool_lat",
    )(h, x, pool_w, pool_scale, mod)


def _pool_ctx_kernel(x_ref, gain_ref, sc_ref, sh_ref, w_ref, ps_ref, g1_ref, o_ref):
    x = x_ref[...]
    h = _norm_mod(x, gain_ref[...], sc_ref[...], sh_ref[...])
    cg = x.shape[1] // N_POOL_GROUPS
    for gi, w in enumerate(POOL_WINDOWS):
        sl = slice(gi * cg, (gi + 1) * cg)
        hg = h[:, sl]
        d = _pool_seq(hg, w) - hg
        y = jnp.dot(d.astype(bf16), w_ref[gi].astype(bf16), preferred_element_type=f32)
        o_ref[:, sl] = x[:, sl] + g1_ref[:, sl] * (y * ps_ref[:, sl])


def _pool_ctx_call(cx, gain, pool_w, pool_scale, mod, ctx_row, batch, ctx_len):
    n, d = cx.shape
    cg = d // N_POOL_GROUPS
    row = lambda i: ctx_row
    return pl.pallas_call(
        _pool_ctx_kernel, grid=(batch,),
        in_specs=[pl.BlockSpec((ctx_len, d), lambda b: (b, 0)),
                  pl.BlockSpec((1, d), lambda b: (0, 0)),
                  _mod_spec(row, 1, d), _mod_spec(row, 0, d),
                  pl.BlockSpec((N_POOL_GROUPS, cg, cg), lambda b: (0, 0, 0)),
                  pl.BlockSpec((1, d), lambda b: (0, 0)),
                  _mod_spec(row, 2, d)],
        out_specs=pl.BlockSpec((ctx_len, d), lambda b: (b, 0)),
        out_shape=jax.ShapeDtypeStruct((n, d), f32),
        compiler_params=_params("arbitrary"), name="pool_ctx",
    )(cx, gain, mod, mod, pool_w, pool_scale, mod)


def _route_kernel(x_ref, xc_ref, gain_ref, sc_ref, sh_ref, wr_ref, rb_ref, loc_ref, gate_ref, cnt_ref, *, n_lat_tiles):
    tt = x_ref.shape[0]
    h = _norm_mod(_token_tile(x_ref, xc_ref, n_lat_tiles), gain_ref[...], sc_ref[...], sh_ref[...])

    def split(v):
        hi = v.astype(bf16)
        return hi, (v - hi.astype(f32)).astype(bf16)
    contract_last = (((1,), (1,)), ((), ()))
    w_hi, w_lo = split(wr_ref[...])
    h_hi, h_lo = split(h)
    by_hi = lax.dot_general(jnp.concatenate([w_hi, w_lo], axis=0), h_hi, contract_last, preferred_element_type=f32)
    by_lo = lax.dot_general(w_hi, h_lo, contract_last, preferred_element_type=f32)
    logits = by_hi[:N_EXPERTS] + (by_hi[N_EXPERTS:] + by_lo) + rb_ref[...]
    e_iota = lax.broadcasted_iota(i32, logits.shape, 0)
    vals, onehots = [], []
    work = logits
    for k in range(TOP_K):
        m = jnp.max(work, axis=0, keepdims=True)
        idx = jnp.min(jnp.where(work == m, e_iota, N_EXPERTS), axis=0, keepdims=True)
        hit = e_iota == idx
        vals.append(m)
        onehots.append(jnp.where(hit, 1.0, 0.0))
        work = jnp.where(hit, -jnp.inf, work)

    ex = [jnp.exp(v - vals[0]) for v in vals]
    den = ex[0] + ex[1] + ex[2] + ex[3]
    for k in range(TOP_K):
        gate_ref[k:k + 1, :] = ex[k] / den

    tri = jnp.where(lax.broadcasted_iota(i32, (tt, tt), 0) <= lax.broadcasted_iota(i32, (tt, tt), 1), 1.0, 0.0)
    planes = jnp.concatenate(onehots, axis=0).astype(bf16)
    incl = jnp.dot(planes, tri.astype(bf16), preferred_element_type=f32)
    totals = [incl[k * N_EXPERTS:(k + 1) * N_EXPERTS, tt - 1:tt] for k in range(TOP_K)]
    count = jnp.broadcast_to(totals[0] + totals[1] + totals[2] + totals[3], (N_EXPERTS, LANES))
    padded = (((count.astype(i32) + (SUBLANES - 1)) // SUBLANES) * SUBLANES).astype(f32)
    below = jnp.where(lax.broadcasted_iota(i32, (N_EXPERTS, N_EXPERTS), 1)
                      < lax.broadcasted_iota(i32, (N_EXPERTS, N_EXPERTS), 0), 1.0, 0.0)
    offset = jnp.dot(below.astype(bf16), padded.astype(bf16), preferred_element_type=f32)[:, 0:1]
    for k in range(TOP_K):
        inc_k = incl[k * N_EXPERTS:(k + 1) * N_EXPERTS]
        before = inc_k - onehots[k] + offset
        loc_ref[k:k + 1, :] = jnp.sum(onehots[k] * before, axis=0, keepdims=True).astype(i32)
        offset = offset + totals[k]
    cnt_ref[...] = count


def _route_call(lat, cx, n, n_lat_tiles, gain, mod, row, w_router_t, b_router, tile):
    d = lat.shape[1]
    e = w_router_t.shape[0]
    return pl.pallas_call(
        functools.partial(_route_kernel, n_lat_tiles=n_lat_tiles), grid=(n // tile,),
        in_specs=_token_specs(tile, d, n_lat_tiles) + [
                  pl.BlockSpec((1, d), lambda i: (0, 0)),
                  _mod_spec(row, 4, d), _mod_spec(row, 3, d),
                  pl.BlockSpec((e, d), lambda i: (0, 0)),
                  pl.BlockSpec((e, 1), lambda i: (0, 0))],
        out_specs=[pl.BlockSpec((TOP_K, tile), lambda i: (0, i)),
                   pl.BlockSpec((TOP_K, tile), lambda i: (0, i)),
                   pl.BlockSpec((None, e, LANES), lambda i: (i, 0, 0))],
        out_shape=[jax.ShapeDtypeStruct((TOP_K, n), i32),
                   jax.ShapeDtypeStruct((TOP_K, n), f32),
                   jax.ShapeDtypeStruct((n // tile, e, LANES), f32)],
        compiler_params=_params("arbitrary"), name="moe_route",
    )(lat, cx, gain, mod, mod, w_router_t, b_router.reshape(e, 1))


def _copy_run(src_at, dst_at, n_pieces, n_bits, sem, wait=False):
    for k in reversed(range(n_bits)):
        size = SUBLANES << k
        done = (n_pieces >> (k + 1)) << (k + 1)

        @pl.when(((n_pieces >> k) & 1) == 1)
        def _(size=size, done=done):
            off = 0 if wait else pl.multiple_of(done * SUBLANES, SUBLANES)
            copy = pltpu.make_async_copy(src_at(off, size), dst_at(off, size), sem)
            copy.wait() if wait else copy.start()


def _dispatch_kernel(go_ref, lb_ref, p8_ref, np_ref, ts_ref, tn_ref, nu_ref,
                     x_ref, xc_ref, gain_ref, sc_ref, sh_ref, loc_ref, xs_hbm, xloc, zbuf, sel_s, sem,
                     *, n_tiles, n_blocks, n_lat_tiles):
    i = pl.program_id(0)
    tt = x_ref.shape[0]
    rl = xloc.shape[1]
    slot = i % 2

    def build(loc, dst):
        rows = [loc[k:k + 1, :].astype(i16) for k in range(TOP_K)]
        one, zero = jnp.ones((), bf16), jnp.zeros((), bf16)
        for r0 in range(0, rl, BUILD_ROWS):
            r_iota = lax.broadcasted_iota(i16, (BUILD_ROWS, tt), 0) + r0
            sel = None
            for k in range(TOP_K):
                m = jnp.where(r_iota == rows[k], one, zero)
                sel = m if sel is None else sel + m
            dst[r0:r0 + BUILD_ROWS, :] = sel

    build(loc_ref, sel_s)
    h = _norm_mod(_token_tile(x_ref, xc_ref, n_lat_tiles), gain_ref[...], sc_ref[...], sh_ref[...]).astype(bf16)
    xloc[slot] = jnp.dot(sel_s[...], h, preferred_element_type=f32)

    seg_bits = (tt // SUBLANES).bit_length()
    tile_bits = (rl // SUBLANES).bit_length()
    buf_at = lambda off, size: xloc.at[0, pl.ds(off, size), :]
    hbm_at = lambda off, size: xs_hbm.at[pl.ds(off, size), :]

    @pl.when(i > 0)
    def _():
        _copy_run(buf_at, hbm_at, np_ref[jnp.maximum(i - 1, 0)], tile_bits, sem.at[1 - slot], wait=True)

    def per_expert(e, carry):
        idx = i * N_EXPERTS + e
        src0 = pl.multiple_of(lb_ref[idx], SUBLANES)
        dst0 = pl.multiple_of(go_ref[idx], SUBLANES)
        _copy_run(lambda off, size: xloc.at[slot, pl.ds(src0 + off, size), :],
                  lambda off, size: xs_hbm.at[pl.ds(dst0 + off, size), :],
                  p8_ref[idx] // SUBLANES, seg_bits, sem.at[slot])
        return carry
    lax.fori_loop(0, N_EXPERTS, per_expert, 0)

    @pl.when(i == n_tiles - 1)
    def _():
        _copy_run(buf_at, hbm_at, np_ref[i], tile_bits, sem.at[slot], wait=True)
        zbuf[...] = jnp.zeros(zbuf.shape, f32)
        tail_bits = (EXPERT_ROWS // SUBLANES - 1).bit_length()
        zero_at = lambda off, size: zbuf.at[pl.ds(0, size), :]

        def tail(e, carry):
            t0 = pl.multiple_of(ts_ref[e], SUBLANES)
            _copy_run(zero_at, lambda off, size: xs_hbm.at[pl.ds(t0 + off, size), :], tn_ref[e], tail_bits,
                      sem.at[slot])
            return carry
        lax.fori_loop(0, N_EXPERTS, tail, 0)

        def tail_wait(e, carry):
            _copy_run(zero_at, hbm_at, tn_ref[e], tail_bits, sem.at[slot], wait=True)
            return carry
        lax.fori_loop(0, N_EXPERTS, tail_wait, 0)

        def spare(b, carry):
            t = pl.multiple_of(b * EXPERT_ROWS, EXPERT_ROWS)
            pltpu.make_async_copy(zbuf, xs_hbm.at[pl.ds(t, EXPERT_ROWS), :], sem.at[slot]).start()
            return carry
        lax.fori_loop(nu_ref[0], n_blocks, spare, 0)

        def spare_wait(b, carry):
            pltpu.make_async_copy(zbuf, xs_hbm.at[pl.ds(0, EXPERT_ROWS), :], sem.at[slot]).wait()
            return carry
        lax.fori_loop(nu_ref[0], n_blocks, spare_wait, 0)


def _dispatch_call(tables, lat, cx, n, n_lat_tiles, gain, mod, row, loc, n_rows, tile):
    d = lat.shape[1]
    n_tiles = n // tile
    rl = TOP_K * tile + N_EXPERTS * SUBLANES
    spec = lambda shape, fn: pl.BlockSpec(shape, lambda i, *_: fn(i))
    grid_spec = pltpu.PrefetchScalarGridSpec(
        num_scalar_prefetch=7, grid=(n_tiles,),
        in_specs=_token_specs(tile, d, n_lat_tiles) + [
                  spec((1, d), lambda i: (0, 0)),
                  _mod_spec(row, 4, d), _mod_spec(row, 3, d),
                  spec((TOP_K, tile), lambda i: (0, i))],
        out_specs=pl.BlockSpec(memory_space=pl.ANY),
        scratch_shapes=[pltpu.VMEM((2, rl, d), f32), pltpu.VMEM((EXPERT_ROWS, d), f32),
                        pltpu.VMEM((rl, tile), bf16), pltpu.SemaphoreType.DMA((2,))])
    return pl.pallas_call(
        functools.partial(_dispatch_kernel, n_tiles=n_tiles, n_blocks=n_rows // EXPERT_ROWS, n_lat_tiles=n_lat_tiles),
        grid_spec=grid_spec,
        out_shape=jax.ShapeDtypeStruct((n_rows, d), f32),
        compiler_params=_params("arbitrary"), name="moe_dispatch",
    )(*tables, lat, cx, gain, mod, mod, loc)


def _gmm_kernel(be_ref, bv_ref, lu_ref, x_ref, wgu_ref, bgu_ref, wd_ref, bd_ref, o_ref, wgu_bf, wd_bf):
    j = pl.program_id(0)
    d_ff = wd_ref.shape[0]
    rows = x_ref.shape[0]
    valid = bv_ref[j]

    @pl.when(valid > 0)
    def _():
        new_expert = jnp.logical_or(j == 0, be_ref[j] != be_ref[jnp.maximum(j - 1, 0)])

        @pl.when(new_expert)
        def _():
            wgu_bf[...] = wgu_ref[...].astype(bf16)
            wd_bf[...] = wd_ref[...].astype(bf16)

    def mlp(r0):
        rs = slice(r0, r0 + MLP_ROWS)
        x = x_ref[rs, :].astype(bf16)
        gu = jnp.dot(x, wgu_bf[...], preferred_element_type=f32) + bgu_ref[...]
        g = jnp.minimum(gu[:, :d_ff], SWIGLU_LIMIT)
        u = jnp.clip(gu[:, d_ff:], -SWIGLU_LIMIT, SWIGLU_LIMIT)
        act = (u + 1.0) * g * jax.nn.sigmoid(SWIGLU_ALPHA * g)
        o_ref[rs, :] = jnp.dot(act.astype(bf16), wd_bf[...], preferred_element_type=f32) + bd_ref[...]

    n_groups = rows // MLP_ROWS
    for full in range(n_groups + 1):
        lo, hi = (full - 1) * MLP_ROWS, full * MLP_ROWS
        cond = (valid == 0) if full == 0 else jnp.logical_and(valid > lo, valid <= hi)
        if full == n_groups:
            cond = valid > lo

        @pl.when(cond)
        def _(full=full):
            for gidx in range(full):
                mlp(gidx * MLP_ROWS)
            if full < n_groups:
                o_ref[full * MLP_ROWS:, :] = jnp.zeros((rows - full * MLP_ROWS, o_ref.shape[1]), f32)


def _gmm_call(layer, blk_e, blk_valid, last_used, xs, w_gu, b_gu, w_down, b_down):
    depth, e, d, f2 = w_gu.shape
    d_ff = w_down.shape[2]
    rows = EXPERT_ROWS
    n_blocks = xs.shape[0] // rows
    grid_spec = pltpu.PrefetchScalarGridSpec(
        num_scalar_prefetch=3, grid=(n_blocks,),
        in_specs=[pl.BlockSpec((rows, d), lambda j, be, bv, lu: (jnp.minimum(j, lu[0]), 0)),
                  pl.BlockSpec((None, None, d, f2), lambda j, be, bv, lu: (layer, be[j], 0, 0)),
                  pl.BlockSpec((None, None, 1, f2), lambda j, be, bv, lu: (layer, be[j], 0, 0)),
                  pl.BlockSpec((None, None, d_ff, d), lambda j, be, bv, lu: (layer, be[j], 0, 0)),
                  pl.BlockSpec((None, None, 1, d), lambda j, be, bv, lu: (layer, be[j], 0, 0))],
        out_specs=pl.BlockSpec((rows, d), lambda j, be, bv, lu: (j, 0)),
        scratch_shapes=[pltpu.VMEM((d, f2), bf16), pltpu.VMEM((d_ff, d), bf16)])
    return pl.pallas_call(
        _gmm_kernel, grid_spec=grid_spec,
        out_shape=jax.ShapeDtypeStruct((n_blocks * rows, d), f32),
        compiler_params=_params("arbitrary"), name="moe_experts",
    )(blk_e, blk_valid, last_used, xs, w_gu, b_gu.reshape(depth, e, 1, f2), w_down, b_down.reshape(depth, e, 1, d))


def _combine_kernel(go_ref, lb_ref, p8_ref, np_ref,
                    ys_hbm, loc_ref, gate_ref, x_ref, xc_ref, g2_ref, fn_ref, o_ref, yloc, w_s, sem,
                    *, n_tiles, n_lat_tiles, final):
    i = pl.program_id(0)
    tt = x_ref.shape[0]
    rl = yloc.shape[1]
    slot = i % 2

    def build(loc, gate, dst):
        cols = [(loc[:, k:k + 1].astype(i16), gate[:, k:k + 1].astype(bf16)) for k in range(TOP_K)]
        zero = jnp.zeros((), bf16)
        for c0 in range(0, rl, BUILD_ROWS):
            r_iota = lax.broadcasted_iota(i16, (tt, BUILD_ROWS), 1) + c0
            weights = None
            for at, g in cols:
                m = jnp.where(r_iota == at, g, zero)
                weights = m if weights is None else weights + m
            dst[:, c0:c0 + BUILD_ROWS] = weights

    seg_bits = (tt // SUBLANES).bit_length()
    tile_bits = (rl // SUBLANES).bit_length()

    def fetch(tile, buf):
        def per_expert(e, carry):
            idx = tile * N_EXPERTS + e
            src0 = pl.multiple_of(go_ref[idx], SUBLANES)
            dst0 = pl.multiple_of(lb_ref[idx], SUBLANES)
            _copy_run(lambda off, size: ys_hbm.at[pl.ds(src0 + off, size), :],
                      lambda off, size: yloc.at[buf, pl.ds(dst0 + off, size), :],
                      p8_ref[idx] // SUBLANES, seg_bits, sem.at[buf])
            return carry
        lax.fori_loop(0, N_EXPERTS, per_expert, 0)

    @pl.when(i == 0)
    def _():
        yloc[...] = jnp.zeros(yloc.shape, f32)
        fetch(0, 0)

    @pl.when(i + 1 < n_tiles)
    def _():
        fetch(jnp.minimum(i + 1, n_tiles - 1), 1 - slot)

    _copy_run(lambda off, size: ys_hbm.at[pl.ds(off, size), :], lambda off, size: yloc.at[0, pl.ds(off, size), :],
              np_ref[i], tile_bits, sem.at[slot], wait=True)

    build(loc_ref, gate_ref, w_s)
    y = jnp.dot(w_s[...], yloc[slot].astype(bf16), preferred_element_type=f32)
    out = _token_tile(x_ref, xc_ref, n_lat_tiles) + g2_ref[...] * y
    if final:
        ms = jnp.mean(out * out, axis=-1, keepdims=True)
        out = (out * lax.rsqrt(ms + NORM_EPS)) * fn_ref[...]
    o_ref[...] = out


def _combine_call(tables, ys, loc_t, gates_t, lat, cx, n, n_lat_tiles, mod, row, final_gain, tile, final):
    d = lat.shape[1]
    n_tiles = n // tile
    rl = TOP_K * tile + N_EXPERTS * SUBLANES
    spec = lambda shape, fn: pl.BlockSpec(shape, lambda i, *_: fn(i))
    grid_spec = pltpu.PrefetchScalarGridSpec(
        num_scalar_prefetch=4, grid=(n_tiles,),
        in_specs=[pl.BlockSpec(memory_space=pl.ANY),
                  spec((tile, TOP_K), lambda i: (i, 0)),
                  spec((tile, TOP_K), lambda i: (i, 0))] + _token_specs(tile, d, n_lat_tiles) + [
                  _mod_spec(row, 5, d),
                  spec((1, d), lambda i: (0, 0))],
        out_specs=spec((tile, d), lambda i: (i, 0)),
        scratch_shapes=[pltpu.VMEM((2, rl, d), f32), pltpu.VMEM((tile, rl), bf16),
                        pltpu.SemaphoreType.DMA((2,))])
    return pl.pallas_call(
        functools.partial(_combine_kernel, n_tiles=n_tiles, n_lat_tiles=n_lat_tiles, final=final), grid_spec=grid_spec,
        out_shape=jax.ShapeDtypeStruct((n, d), f32),
        compiler_params=_params("arbitrary"), name="moe_combine",
    )(*tables, ys, loc_t, gates_t, lat, cx, mod, final_gain)


def _moe_layer(layer, lat, cx, gain, mod, seq, ctx_row, w_router, b_router, w_gu, b_gu, w_down, b_down,
               final_gain, final):
    n_lat, d = lat.shape
    n = n_lat + (0 if cx is None else cx.shape[0])
    tile = _pick_tile(ROUTE_TILE, n_lat, n - n_lat if n > n_lat else n_lat, seq)
    n_tiles, n_lat_tiles = n // tile, n_lat // tile
    cx = lat if cx is None else cx
    row = _row_map(n_lat_tiles, seq // tile, ctx_row)

    loc, gate, cnt = _route_call(lat, cx, n, n_lat_tiles, gain, mod, row, w_router[layer].T, b_router[layer], tile)

    rows = EXPERT_ROWS
    count = cnt[:, :, 0].astype(i32)
    seg = (count + SUBLANES - 1) // SUBLANES * SUBLANES
    local_base = jnp.cumsum(seg, axis=1) - seg
    pieces = jnp.sum(seg, axis=1) // SUBLANES
    expert_rows = jnp.sum(seg, axis=0)
    region = (expert_rows + rows - 1) // rows * rows
    region_end = jnp.cumsum(region)
    region_start = region_end - region
    global_off = region_start[None, :] + jnp.cumsum(seg, axis=0) - seg
    n_blocks = (n * TOP_K + n_tiles * N_EXPERTS * (SUBLANES - 1) + N_EXPERTS * (rows - 1) + rows - 1) // rows
    blk_start = jnp.arange(n_blocks, dtype=i32) * rows
    experts = jnp.arange(N_EXPERTS, dtype=i32)
    blk_e = jnp.minimum(jnp.sum(region_end[None, :] <= blk_start[:, None], axis=1), N_EXPERTS - 1).astype(i32)
    content_end = region_start + expert_rows
    blk_end = jnp.sum(jnp.where(blk_e[:, None] == experts[None, :], content_end[None, :], 0), axis=1)
    blk_valid = jnp.clip(blk_end - blk_start, 0, rows).astype(i32)
    n_used = (region_end[-1:] // rows).astype(i32)
    last_used = jnp.maximum(n_used - 1, 0)
    tail_n = ((region - expert_rows) // SUBLANES).astype(i32)
    flat = lambda a: a.reshape(-1).astype(i32)
    tables = (flat(global_off), flat(local_base), flat(seg), pieces.astype(i32))

    xs = _dispatch_call(tables + (content_end.astype(i32), tail_n, n_used), lat, cx, n, n_lat_tiles, gain, mod, row,
                        loc, n_blocks * rows, tile)
    ys = _gmm_call(layer, blk_e, blk_valid, last_used, xs, w_gu, b_gu, w_down, b_down)
    return _combine_call(tables, ys, loc.T, gate.T, lat, cx, n, n_lat_tiles, mod, row, final_gain, tile, final)


def _time_tile_blocks(n_lat_t, n_ctx_t, n_lat_blocks, b):
    def index(i):
        lat_blk = b * n_lat_t + jnp.minimum(i, n_lat_t - 1)
        ctx_blk = n_lat_blocks + b * n_ctx_t + jnp.maximum(i - n_lat_t, 0)
        return (jnp.where(i < n_lat_t, lat_blk, ctx_blk), 0)
    return index


def _lru_in_kernel(*refs, batch, n_lat_t):
    x_refs = refs[:batch]
    gain_ref, sc_ref, sh_ref, scc_ref, shc_ref, w_ref, gg_ref, u_ref, tm_s = refs[batch:]
    tt = x_refs[0].shape[0]
    d_rnn = gg_ref.shape[1]
    is_ctx = pl.program_id(0) >= n_lat_t
    hs = []
    for b in range(batch):
        scale = jnp.where(is_ctx, scc_ref[...], sc_ref[b])
        shift = jnp.where(is_ctx, shc_ref[...], sh_ref[b])
        hs.append(_norm_mod(x_refs[b][...], gain_ref[...], scale, shift).astype(bf16))
    gu = jnp.dot(jnp.concatenate(hs, axis=0), w_ref[...], preferred_element_type=f32)
    gate = gu[:, :d_rnn]
    gelu = 0.5 * gate * (1.0 + jnp.tanh(GELU_C * (gate + 0.044715 * (gate * gate * gate))))
    for out_ref, val in ((gg_ref, gelu), (u_ref, gu[:, d_rnn:])):
        for b in range(batch):
            for s in range(d_rnn // LANES):
                tm_s[s, pl.ds(b, tt, stride=batch), :] = val[b * tt:(b + 1) * tt, s * LANES:(s + 1) * LANES]
        for s in range(d_rnn // LANES):
            out_ref[:, s * LANES:(s + 1) * LANES] = tm_s[s]


def _lru_in_call(tokens, gain, mod, ctx_row, w_in_bf, batch, seq, ctx_len, tt):
    n, d = tokens.shape
    d2 = w_in_bf.shape[1]
    d_rnn = d2 // 2
    n_lat_t, n_ctx_t = seq // tt, ctx_len // tt
    n_lat_blocks = batch * n_lat_t
    x_specs = [pl.BlockSpec((tt, d), _time_tile_blocks(n_lat_t, n_ctx_t, n_lat_blocks, b)) for b in range(batch)]
    mod_lat = lambda chunk: pl.BlockSpec((batch, 1, d), lambda i: (0, 0, chunk))
    mod_ctx = lambda chunk: pl.BlockSpec((None, 1, d), lambda i: (ctx_row, 0, chunk))
    out_spec = pl.BlockSpec((tt * batch, d_rnn), lambda i: (i, 0))
    out_shape = jax.ShapeDtypeStruct(((seq + ctx_len) * batch, d_rnn), f32)
    return pl.pallas_call(
        functools.partial(_lru_in_kernel, batch=batch, n_lat_t=n_lat_t), grid=(n_lat_t + n_ctx_t,),
        in_specs=x_specs + [pl.BlockSpec((1, d), lambda i: (0, 0)), mod_lat(1), mod_lat(0), mod_ctx(1), mod_ctx(0),
                            pl.BlockSpec((d, d2), lambda i: (0, 0))],
        out_specs=[out_spec, out_spec], out_shape=[out_shape, out_shape],
        scratch_shapes=[pltpu.VMEM((d_rnn // LANES, tt * batch, LANES), f32)],
        compiler_params=_params("arbitrary"), name="lru_in",
    )(*([tokens] * batch), gain, mod, mod, mod, mod, w_in_bf)


def _lru_scan_kernel(*refs, batch, n_ctx_c, n_lat_c, reverse):
    if reverse:
        (prev_ref, cur_ref, next_ref, cw_ref, cb_ref, wr_ref, br_ref, wi_ref, bi_ref, lam_ref, hf_ref, gg_ref,
         o_ref, a_s, b_s, h_s, z_s) = refs
    else:
        (prev_ref, cur_ref, next_ref, cw_ref, cb_ref, wr_ref, br_ref, wi_ref, bi_ref, lam_ref,
         o_ref, a_s, b_s, h_s) = refs
    s = pl.program_id(0)
    rows, d_rnn = cur_ref.shape
    steps = rows // batch
    wb = d_rnn // N_LRU_BLOCKS

    @pl.when(s == 0)
    def _():
        h_s[...] = jnp.zeros(h_s.shape, f32)

    c = jnp.where(s < n_ctx_c, s, s - n_ctx_c)
    seg_chunks = jnp.where(s < n_ctx_c, n_ctx_c, n_lat_c)
    c = (seg_chunks - 1 - c) if reverse else c
    has_prev = c > 0
    has_next = c < seg_chunks - 1
    ext = jnp.concatenate([jnp.where(has_prev, prev_ref[...], 0.0), cur_ref[...],
                           jnp.where(has_next, next_ref[...], 0.0)], axis=0)
    u = cb_ref[...]
    for k in range(CONV_WIDTH):
        u = u + ext[k * batch:k * batch + rows] * cw_ref[k:k + 1, :]

    lam = lam_ref[...]
    softplus_neg = jnp.maximum(-lam, 0.0) + jnp.log1p(jnp.exp(-jnp.abs(lam)))
    decay = (-0.5 * LRU_C) * softplus_neg
    for n in range(N_LRU_BLOCKS):
        cols = slice(n * wb, (n + 1) * wb)
        un = u[:, cols]
        ub = un.astype(bf16)
        half_r = jnp.dot(ub, (0.5 * wr_ref[n]).astype(bf16), preferred_element_type=f32) + 0.5 * br_ref[:, cols]
        half_i = jnp.dot(ub, (0.5 * wi_ref[n]).astype(bf16), preferred_element_type=f32) + 0.5 * bi_ref[:, cols]
        log_a = decay[:, cols] * (1.0 + jnp.tanh(half_r))
        a = jnp.exp(log_a)
        mult = jnp.sqrt(-jnp.tanh(log_a) * (a * a + 1.0))
        a_s[:, cols] = a
        b_s[:, cols] = (mult * (0.5 * un)) * (1.0 + jnp.tanh(half_i))

    def step(q, h):
        for j in range(SCAN_UNROLL):
            t = q * SCAN_UNROLL + j
            t = (steps - 1 - t) if reverse else t
            row = pl.multiple_of(t * batch, batch)
            h = a_s[pl.ds(row, batch), :] * h + b_s[pl.ds(row, batch), :]
            if reverse:
                z_s[pl.ds(row, batch), :] = h
            else:
                o_ref[pl.ds(row, batch), :] = h
        return h
    h_s[...] = lax.fori_loop(0, steps // SCAN_UNROLL, step, h_s[...])
    if reverse:
        o_ref[...] = (gg_ref[...] * (hf_ref[...] + z_s[...])).astype(o_ref.dtype)


def _lru_scan_call(u, hf, gg, conv_w, conv_b, w_r, b_r, w_i, b_i, lam, batch, seq, ctx_len, chunk, reverse):
    n_rows, d_rnn = u.shape
    rows = chunk * batch
    n_lat_c, n_ctx_c = seq // chunk, ctx_len // chunk
    pre, post = CONV_LEFT * batch, (CONV_WIDTH - 1 - CONV_LEFT) * batch
    wb = d_rnn // N_LRU_BLOCKS

    def blk(s):
        ci = jnp.where(s < n_ctx_c, s, s - n_ctx_c)
        if reverse:
            ci = jnp.where(s < n_ctx_c, n_ctx_c - 1 - ci, n_lat_c - 1 - ci)
        return jnp.where(s < n_ctx_c, n_lat_c + ci, ci)

    last_post = n_rows // post - 1
    cur = pl.BlockSpec((rows, d_rnn), lambda s: (blk(s), 0))
    in_specs = [pl.BlockSpec((pre, d_rnn), lambda s: (jnp.maximum(blk(s) * (rows // pre) - 1, 0), 0)),
                cur,
                pl.BlockSpec((post, d_rnn), lambda s: (jnp.minimum((blk(s) + 1) * (rows // post), last_post), 0)),
                pl.BlockSpec((CONV_WIDTH, d_rnn), lambda s: (0, 0)),
                pl.BlockSpec((1, d_rnn), lambda s: (0, 0)),
                pl.BlockSpec((N_LRU_BLOCKS, wb, wb), lambda s: (0, 0, 0)),
                pl.BlockSpec((1, d_rnn), lambda s: (0, 0)),
                pl.BlockSpec((N_LRU_BLOCKS, wb, wb), lambda s: (0, 0, 0)),
                pl.BlockSpec((1, d_rnn), lambda s: (0, 0)),
                pl.BlockSpec((1, d_rnn), lambda s: (0, 0))]
    args = [u, u, u, conv_w, conv_b.reshape(1, d_rnn), w_r, b_r.reshape(1, d_rnn), w_i, b_i.reshape(1, d_rnn),
            lam.reshape(1, d_rnn)]
    scratch = [pltpu.VMEM((rows, d_rnn), f32), pltpu.VMEM((rows, d_rnn), f32), pltpu.VMEM((batch, d_rnn), f32)]
    if reverse:
        in_specs += [cur, cur]
        args += [hf, gg]
        scratch.append(pltpu.VMEM((rows, d_rnn), f32))
    return pl.pallas_call(
        functools.partial(_lru_scan_kernel, batch=batch, n_ctx_c=n_ctx_c, n_lat_c=n_lat_c, reverse=reverse),
        grid=(n_ctx_c + n_lat_c,), in_specs=in_specs, out_specs=cur,
        out_shape=jax.ShapeDtypeStruct((n_rows, d_rnn), bf16 if reverse else f32),
        scratch_shapes=scratch, compiler_params=_params("arbitrary"),
        name="lru_scan_rev" if reverse else "lru_scan_fwd",
    )(*args)


def _lru_out_kernel(*refs, batch):
    z_ref = refs[0]
    x_refs = refs[1:1 + batch]
    w_ref, g1_ref, o_ref, y_s = refs[1 + batch:]
    tt = x_refs[0].shape[0]
    y = jnp.dot(z_ref[...], w_ref[...], preferred_element_type=f32)
    n_planes = y.shape[1] // LANES
    for s in range(n_planes):
        y_s[s] = y[:, s * LANES:(s + 1) * LANES]
    for b in range(batch):
        y_b = jnp.concatenate([y_s[s, pl.ds(b, tt, stride=batch), :] for s in range(n_planes)], axis=-1)
        o_ref[b] = x_refs[b][...] + g1_ref[b] * y_b


def _lru_out_call(z, tokens, w_out_bf, mod, batch, seq, tt):
    d = tokens.shape[1]
    d_rnn = z.shape[1]
    n_t = seq // tt
    x_specs = [pl.BlockSpec((tt, d), lambda i, b=b: (b * n_t + i, 0)) for b in range(batch)]
    return pl.pallas_call(
        functools.partial(_lru_out_kernel, batch=batch), grid=(n_t,),
        in_specs=[pl.BlockSpec((tt * batch, d_rnn), lambda i: (i, 0))] + x_specs
                 + [pl.BlockSpec((d_rnn, d), lambda i: (0, 0)), pl.BlockSpec((batch, 1, d), lambda i: (0, 0, 2))],
        out_specs=pl.BlockSpec((batch, tt, d), lambda i: (0, i, 0)),
        out_shape=jax.ShapeDtypeStruct((batch, seq, d), f32),
        scratch_shapes=[pltpu.VMEM((d // LANES, tt * batch, LANES), f32)],
        compiler_params=_params("arbitrary"), name="lru_out",
    )(z, *([tokens] * batch), w_out_bf, mod)


def kernel(x, c, ctx, c_ctx, ada_w, ada_b, norm_mix, norm_ffn, pool_w, pool_scale, lru_w_in, lru_conv_w, lru_conv_b, lru_w_r, lru_b_r, lru_w_i, lru_b_i, lru_lam, lru_w_out, router_w, router_b, exp_w_gu, exp_b_gu, exp_w_down, exp_b_down, final_norm):
    batch, seq, d = x.shape
    ctx_len = ctx.shape[1]
    assert d == SUBLANES * LANES and seq % GRID_W == 0
    n_lat, n_ctx = batch * seq, batch * ctx_len
    ctx_row = batch

    mod_rows = -(-(batch + 1) // SUBLANES) * SUBLANES
    cc = jnp.zeros((mod_rows, d), f32).at[:batch].set(c).at[batch].set(c_ctx)
    mod_all = _ada_call(cc, ada_w, ada_b)
    mods = [mod_all[l].reshape(mod_rows, 1, 6 * d) for l in range(ada_w.shape[0])]

    lat = x.reshape(n_lat, d)
    cx = ctx.reshape(n_ctx, d)
    tile = _pick_tile(ROUTE_TILE, seq, n_ctx)
    row_lat = _row_map(n_lat // tile, seq // tile, ctx_row)
    gain = lambda g: g.reshape(1, d)

    hl = _norm_call(lat, gain(norm_mix[0]), mods[0], 0, 1, row_lat, tile)
    lat = _pool_lat_call(hl, lat, pool_w[0], gain(pool_scale[0]), mods[0], batch, seq)
    cx = _pool_ctx_call(cx, gain(norm_mix[0]), pool_w[0], gain(pool_scale[0]), mods[0], ctx_row, batch, ctx_len)
    tokens = _moe_layer(0, lat, cx, gain(norm_ffn[0]), mods[0], seq, ctx_row, router_w, router_b,
                        exp_w_gu, exp_b_gu, exp_w_down, exp_b_down, gain(final_norm), False)

    assert batch == SUBLANES
    tt = _pick_tile(LRU_TILE, seq, ctx_len)
    chunk = _pick_tile(SCAN_CHUNK, seq, ctx_len)
    gg, u = _lru_in_call(tokens, gain(norm_mix[1]), mods[1], ctx_row, lru_w_in[0].astype(bf16), batch, seq,
                         ctx_len, tt)
    scan = functools.partial(_lru_scan_call, conv_w=lru_conv_w[0], conv_b=lru_conv_b[0], batch=batch, seq=seq,
                             ctx_len=ctx_len, chunk=chunk)
    hf = scan(u, None, None, w_r=lru_w_r[0, 0], b_r=lru_b_r[0, 0], w_i=lru_w_i[0, 0], b_i=lru_b_i[0, 0],
              lam=lru_lam[0, 0], reverse=False)
    z = scan(u, hf, gg, w_r=lru_w_r[0, 1], b_r=lru_b_r[0, 1], w_i=lru_w_i[0, 1], b_i=lru_b_i[0, 1],
             lam=lru_lam[0, 1], reverse=True)
    lat = _lru_out_call(z, tokens, lru_w_out[0].astype(bf16), mods[1], batch, seq, tt).reshape(n_lat, d)
    out = _moe_layer(1, lat, None, gain(norm_ffn[1]), mods[1], seq, ctx_row, router_w, router_b,
                     exp_w_gu, exp_b_gu, exp_w_down, exp_b_down, gain(final_norm), True)
    return out.reshape(batch, seq, d)
```

```python
import functools

import numpy as np
import jax
import jax.numpy as jnp
from jax import lax
from jax.experimental import pallas as pl
from jax.experimental.pallas import tpu as pltpu

f32 = jnp.float32
bf16 = jnp.bfloat16
i32 = jnp.int32
i16 = jnp.int16
HIGHEST = lax.Precision.HIGHEST

LANES = 128
SUBLANES = 8
VMEM_LIMIT = 56 * 1024 * 1024

GRID_W = 64
GRID_SHIFT = 6
POOL_WINDOWS = (2, 4, 8, 16)
N_POOL_GROUPS = 4
N_LRU_BLOCKS = 4
CONV_LEFT = 2
CONV_WIDTH = 4
LRU_C = 8.0
N_EXPERTS = 32
TOP_K = 4
SWIGLU_LIMIT = 7.0
SWIGLU_ALPHA = 1.702
NORM_EPS = 1e-6
GELU_C = 0.7978845608028654

EXPERT_ROWS = 1024
MLP_ROWS = 256
BUILD_ROWS = 128
ROUTE_TILE = 512
LRU_TILE = 128
SCAN_CHUNK = 64
SCAN_UNROLL = 8


def _params(*sem):
    return pltpu.CompilerParams(dimension_semantics=sem, vmem_limit_bytes=VMEM_LIMIT)


def _pick_tile(pref, *sizes):
    t = pref
    while any(s % t for s in sizes):
        t //= 2
    assert t >= SUBLANES
    return t


def _norm_mod(x, gain, scale, shift):
    ms = jnp.mean(x * x, axis=-1, keepdims=True)
    y = x * lax.rsqrt(ms + NORM_EPS)
    return (y * gain) * (1.0 + scale) + shift


def _row_map(n_lat_tiles, tiles_per_batch, ctx_row):
    def row(i):
        return jnp.where(i < n_lat_tiles, i // tiles_per_batch, ctx_row)
    return row


def _mod_spec(row, chunk, width):
    return pl.BlockSpec((None, 1, width), lambda i, *_: (row(i), 0, chunk))


def _token_specs(tile, d, n_lat_tiles):
    return [pl.BlockSpec((tile, d), lambda i, *_: (jnp.minimum(i, n_lat_tiles - 1), 0)),
            pl.BlockSpec((tile, d), lambda i, *_: (jnp.maximum(i - n_lat_tiles, 0), 0))]


def _token_tile(x_ref, xc_ref, n_lat_tiles):
    return jnp.where(pl.program_id(0) >= n_lat_tiles, xc_ref[...], x_ref[...])


def _ada_kernel(c_ref, w_ref, b_ref, o_ref):
    c = c_ref[...]
    s = c * jax.nn.sigmoid(c)
    o_ref[...] = jnp.dot(s, w_ref[...], preferred_element_type=f32, precision=HIGHEST) + b_ref[...]


def _ada_call(cc, ada_w, ada_b):
    depth, d, d6 = ada_w.shape
    rows = cc.shape[0]
    bn = 1536
    return pl.pallas_call(
        _ada_kernel, grid=(depth, d6 // bn),
        in_specs=[pl.BlockSpec((rows, d), lambda l, j: (0, 0)),
                  pl.BlockSpec((None, d, bn), lambda l, j: (l, 0, j)),
                  pl.BlockSpec((None, 1, bn), lambda l, j: (l, 0, j))],
        out_specs=pl.BlockSpec((None, rows, bn), lambda l, j: (l, 0, j)),
        out_shape=jax.ShapeDtypeStruct((depth, rows, d6), f32),
        compiler_params=_params("arbitrary", "arbitrary"), name="ada_mod",
    )(cc, ada_w, ada_b.reshape(depth, 1, d6))


def _norm_kernel(x_ref, g_ref, sc_ref, sh_ref, o_ref):
    o_ref[...] = _norm_mod(x_ref[...], g_ref[...], sc_ref[...], sh_ref[...])


def _norm_call(tokens, gain, mod, sh_chunk, sc_chunk, row, tile):
    n, d = tokens.shape
    return pl.pallas_call(
        _norm_kernel, grid=(n // tile,),
        in_specs=[pl.BlockSpec((tile, d), lambda i: (i, 0)),
                  pl.BlockSpec((1, d), lambda i: (0, 0)),
                  _mod_spec(row, sc_chunk, d), _mod_spec(row, sh_chunk, d)],
        out_specs=pl.BlockSpec((tile, d), lambda i: (i, 0)),
        out_shape=jax.ShapeDtypeStruct((n, d), f32),
        compiler_params=_params("arbitrary"), name="norm_mod",
    )(tokens, gain, mod, mod)


def _shift(x, j, unit, pos, limit):
    n = x.shape[0]
    rolled = pltpu.roll(x, (j * unit) % n, 0)
    ok = (pos >= j) if j > 0 else (pos < limit + j)
    return jnp.where(ok, rolled, 0.0)


def _window_sum(x, w, unit, pos, limit):
    m = w // 2
    trail, lead, s = x, x, 1
    while s < m:
        trail = trail + _shift(trail, s, unit, pos, limit)
        lead = lead + _shift(lead, -s, unit, pos, limit)
        s *= 2
    return _shift(trail, 1, unit, pos, limit) + lead


def _window_count(w, pos, limit):
    m = w // 2
    return jnp.minimum(pos + m, limit) - jnp.maximum(pos - m, 0)


def _pool_grid(h, w):
    t = lax.broadcasted_iota(i32, h.shape, 0)
    col, row = t & (GRID_W - 1), t >> GRID_SHIFT
    rows = h.shape[0] // GRID_W
    total = _window_sum(_window_sum(h, w, 1, col, GRID_W), w, GRID_W, row, rows)
    count = (_window_count(w, row, rows) * _window_count(w, col, GRID_W)).astype(f32)
    return total / count


def _pool_seq(h, w):
    t = lax.broadcasted_iota(i32, h.shape, 0)
    n = h.shape[0]
    return _window_sum(h, w, 1, t, n) / _window_count(w, t, n).astype(f32)


def _pool_lat_kernel(h_ref, x_ref, w_ref, ps_ref, g1_ref, o_ref):
    grp = pl.program_id(1)
    for gi, w in enumerate(POOL_WINDOWS):
        @pl.when(grp == gi)
        def _(w=w):
            h = h_ref[...]
            d = _pool_grid(h, w) - h
            y = jnp.dot(d.astype(bf16), w_ref[...].astype(bf16), preferred_element_type=f32)
            o_ref[...] = x_ref[...] + g1_ref[...] * (y * ps_ref[...])


def _pool_lat_call(h, x, pool_w, pool_scale, mod, batch, seq):
    n, d = x.shape
    cg = d // N_POOL_GROUPS
    g1_chunk0 = 2 * N_POOL_GROUPS
    return pl.pallas_call(
        _pool_lat_kernel, grid=(batch, N_POOL_GROUPS),
        in_specs=[pl.BlockSpec((seq, cg), lambda b, g: (b, g)),
                  pl.BlockSpec((seq, cg), lambda b, g: (b, g)),
                  pl.BlockSpec((None, cg, cg), lambda b, g: (g, 0, 0)),
                  pl.BlockSpec((1, cg), lambda b, g: (0, g)),
                  pl.BlockSpec((None, 1, cg), lambda b, g: (b, 0, g1_chunk0 + g))],
        out_specs=pl.BlockSpec((seq, cg), lambda b, g: (b, g)),
        out_shape=jax.ShapeDtypeStruct((n, d), f32),
        compiler_params=_params("arbitrary", "arbitrary"), name="pool_lat",
    )(h, x, pool_w, pool_scale, mod)


def _pool_ctx_kernel(x_ref, gain_ref, sc_ref, sh_ref, w_ref, ps_ref, g1_ref, o_ref):
    x = x_ref[...]
    h = _norm_mod(x, gain_ref[...], sc_ref[...], sh_ref[...])
    cg = x.shape[1] // N_POOL_GROUPS
    for gi, w in enumerate(POOL_WINDOWS):
        sl = slice(gi * cg, (gi + 1) * cg)
        hg = h[:, sl]
        d = _pool_seq(hg, w) - hg
        y = jnp.dot(d.astype(bf16), w_ref[gi].astype(bf16), preferred_element_type=f32)
        o_ref[:, sl] = x[:, sl] + g1_ref[:, sl] * (y * ps_ref[:, sl])


def _pool_ctx_call(cx, gain, pool_w, pool_scale, mod, ctx_row, batch, ctx_len):
    n, d = cx.shape
    cg = d // N_POOL_GROUPS
    row = lambda i: ctx_row
    return pl.pallas_call(
        _pool_ctx_kernel, grid=(batch,),
        in_specs=[pl.BlockSpec((ctx_len, d), lambda b: (b, 0)),
                  pl.BlockSpec((1, d), lambda b: (0, 0)),
                  _mod_spec(row, 1, d), _mod_spec(row, 0, d),
                  pl.BlockSpec((N_POOL_GROUPS, cg, cg), lambda b: (0, 0, 0)),
                  pl.BlockSpec((1, d), lambda b: (0, 0)),
                  _mod_spec(row, 2, d)],
        out_specs=pl.BlockSpec((ctx_len, d), lambda b: (b, 0)),
        out_shape=jax.ShapeDtypeStruct((n, d), f32),
        compiler_params=_params("arbitrary"), name="pool_ctx",
    )(cx, gain, mod, mod, pool_w, pool_scale, mod)


def _route_kernel(x_ref, xc_ref, gain_ref, sc_ref, sh_ref, wr_ref, rb_ref, loc_ref, gate_ref, cnt_ref, *, n_lat_tiles):
    tt = x_ref.shape[0]
    h = _norm_mod(_token_tile(x_ref, xc_ref, n_lat_tiles), gain_ref[...], sc_ref[...], sh_ref[...])

    def split(v):
        hi = v.astype(bf16)
        return hi, (v - hi.astype(f32)).astype(bf16)
    contract_last = (((1,), (1,)), ((), ()))
    w_hi, w_lo = split(wr_ref[...])
    h_hi, h_lo = split(h)
    by_hi = lax.dot_general(jnp.concatenate([w_hi, w_lo], axis=0), h_hi, contract_last, preferred_element_type=f32)
    by_lo = lax.dot_general(w_hi, h_lo, contract_last, preferred_element_type=f32)
    logits = by_hi[:N_EXPERTS] + (by_hi[N_EXPERTS:] + by_lo) + rb_ref[...]
    e_iota = lax.broadcasted_iota(i32, logits.shape, 0)
    vals, onehots = [], []
    work = logits
    for k in range(TOP_K):
        m = jnp.max(work, axis=0, keepdims=True)
        idx = jnp.min(jnp.where(work == m, e_iota, N_EXPERTS), axis=0, keepdims=True)
        hit = e_iota == idx
        vals.append(m)
        onehots.append(jnp.where(hit, 1.0, 0.0))
        work = jnp.where(hit, -jnp.inf, work)

    ex = [jnp.exp(v - vals[0]) for v in vals]
    den = ex[0] + ex[1] + ex[2] + ex[3]
    for k in range(TOP_K):
        gate_ref[k:k + 1, :] = ex[k] / den

    tri = jnp.where(lax.broadcasted_iota(i32, (tt, tt), 0) <= lax.broadcasted_iota(i32, (tt, tt), 1), 1.0, 0.0)
    planes = jnp.concatenate(onehots, axis=0).astype(bf16)
    incl = jnp.dot(planes, tri.astype(bf16), preferred_element_type=f32)
    totals = [incl[k * N_EXPERTS:(k + 1) * N_EXPERTS, tt - 1:tt] for k in range(TOP_K)]
    count = jnp.broadcast_to(totals[0] + totals[1] + totals[2] + totals[3], (N_EXPERTS, LANES))
    padded = (((count.astype(i32) + (SUBLANES - 1)) // SUBLANES) * SUBLANES).astype(f32)
    below = jnp.where(lax.broadcasted_iota(i32, (N_EXPERTS, N_EXPERTS), 1)
                      < lax.broadcasted_iota(i32, (N_EXPERTS, N_EXPERTS), 0), 1.0, 0.0)
    offset = jnp.dot(below.astype(bf16), padded.astype(bf16), preferred_element_type=f32)[:, 0:1]
    for k in range(TOP_K):
        inc_k = incl[k * N_EXPERTS:(k + 1) * N_EXPERTS]
        before = inc_k - onehots[k] + offset
        loc_ref[k:k + 1, :] = jnp.sum(onehots[k] * before, axis=0, keepdims=True).astype(i32)
        offset = offset + totals[k]
    cnt_ref[...] = count


def _route_call(lat, cx, n, n_lat_tiles, gain, mod, row, w_router_t, b_router, tile):
    d = lat.shape[1]
    e = w_router_t.shape[0]
    return pl.pallas_call(
        functools.partial(_route_kernel, n_lat_tiles=n_lat_tiles), grid=(n // tile,),
        in_specs=_token_specs(tile, d, n_lat_tiles) + [
                  pl.BlockSpec((1, d), lambda i: (0, 0)),
                  _mod_spec(row, 4, d), _mod_spec(row, 3, d),
                  pl.BlockSpec((e, d), lambda i: (0, 0)),
                  pl.BlockSpec((e, 1), lambda i: (0, 0))],
        out_specs=[pl.BlockSpec((TOP_K, tile), lambda i: (0, i)),
                   pl.BlockSpec((TOP_K, tile), lambda i: (0, i)),
                   pl.BlockSpec((None, e, LANES), lambda i: (i, 0, 0))],
        out_shape=[jax.ShapeDtypeStruct((TOP_K, n), i32),
                   jax.ShapeDtypeStruct((TOP_K, n), f32),
                   jax.ShapeDtypeStruct((n // tile, e, LANES), f32)],
        compiler_params=_params("arbitrary"), name="moe_route",
    )(lat, cx, gain, mod, mod, w_router_t, b_router.reshape(e, 1))


def _copy_run(src_at, dst_at, n_pieces, n_bits, sem, wait=False):
    for k in reversed(range(n_bits)):
        size = SUBLANES << k
        done = (n_pieces >> (k + 1)) << (k + 1)

        @pl.when(((n_pieces >> k) & 1) == 1)
        def _(size=size, done=done):
            off = 0 if wait else pl.multiple_of(done * SUBLANES, SUBLANES)
            copy = pltpu.make_async_copy(src_at(off, size), dst_at(off, size), sem)
            copy.wait() if wait else copy.start()


def _dispatch_kernel(go_ref, lb_ref, p8_ref, np_ref, ts_ref, tn_ref, nu_ref,
                     x_ref, xc_ref, gain_ref, sc_ref, sh_ref, loc_ref, xs_hbm, xloc, zbuf, sel_s, sem,
                     *, n_tiles, n_blocks, n_lat_tiles):
    i = pl.program_id(0)
    tt = x_ref.shape[0]
    rl = xloc.shape[1]
    slot = i % 2

    def build(loc, dst):
        rows = [loc[k:k + 1, :].astype(i16) for k in range(TOP_K)]
        one, zero = jnp.ones((), bf16), jnp.zeros((), bf16)
        for r0 in range(0, rl, BUILD_ROWS):
            r_iota = lax.broadcasted_iota(i16, (BUILD_ROWS, tt), 0) + r0
            sel = None
            for k in range(TOP_K):
                m = jnp.where(r_iota == rows[k], one, zero)
                sel = m if sel is None else sel + m
            dst[r0:r0 + BUILD_ROWS, :] = sel

    build(loc_ref, sel_s)
    h = _norm_mod(_token_tile(x_ref, xc_ref, n_lat_tiles), gain_ref[...], sc_ref[...], sh_ref[...]).astype(bf16)
    xloc[slot] = jnp.dot(sel_s[...], h, preferred_element_type=f32)

    seg_bits = (tt // SUBLANES).bit_length()
    tile_bits = (rl // SUBLANES).bit_length()
    buf_at = lambda off, size: xloc.at[0, pl.ds(off, size), :]
    hbm_at = lambda off, size: xs_hbm.at[pl.ds(off, size), :]

    @pl.when(i > 0)
    def _():
        _copy_run(buf_at, hbm_at, np_ref[jnp.maximum(i - 1, 0)], tile_bits, sem.at[1 - slot], wait=True)

    def per_expert(e, carry):
        idx = i * N_EXPERTS + e
        src0 = pl.multiple_of(lb_ref[idx], SUBLANES)
        dst0 = pl.multiple_of(go_ref[idx], SUBLANES)
        _copy_run(lambda off, size: xloc.at[slot, pl.ds(src0 + off, size), :],
                  lambda off, size: xs_hbm.at[pl.ds(dst0 + off, size), :],
                  p8_ref[idx] // SUBLANES, seg_bits, sem.at[slot])
        return carry
    lax.fori_loop(0, N_EXPERTS, per_expert, 0)

    @pl.when(i == n_tiles - 1)
    def _():
        _copy_run(buf_at, hbm_at, np_ref[i], tile_bits, sem.at[slot], wait=True)
        zbuf[...] = jnp.zeros(zbuf.shape, f32)
        tail_bits = (EXPERT_ROWS // SUBLANES - 1).bit_length()
        zero_at = lambda off, size: zbuf.at[pl.ds(0, size), :]

        def tail(e, carry):
            t0 = pl.multiple_of(ts_ref[e], SUBLANES)
            _copy_run(zero_at, lambda off, size: xs_hbm.at[pl.ds(t0 + off, size), :], tn_ref[e], tail_bits,
                      sem.at[slot])
            return carry
        lax.fori_loop(0, N_EXPERTS, tail, 0)

        def tail_wait(e, carry):
            _copy_run(zero_at, hbm_at, tn_ref[e], tail_bits, sem.at[slot], wait=True)
            return carry
        lax.fori_loop(0, N_EXPERTS, tail_wait, 0)

        def spare(b, carry):
            t = pl.multiple_of(b * EXPERT_ROWS, EXPERT_ROWS)
            pltpu.make_async_copy(zbuf, xs_hbm.at[pl.ds(t, EXPERT_ROWS), :], sem.at[slot]).start()
            return carry
        lax.fori_loop(nu_ref[0], n_blocks, spare, 0)

        def spare_wait(b, carry):
            pltpu.make_async_copy(zbuf, xs_hbm.at[pl.ds(0, EXPERT_ROWS), :], sem.at[slot]).wait()
            return carry
        lax.fori_loop(nu_ref[0], n_blocks, spare_wait, 0)


def _dispatch_call(tables, lat, cx, n, n_lat_tiles, gain, mod, row, loc, n_rows, tile):
    d = lat.shape[1]
    n_tiles = n // tile
    rl = TOP_K * tile + N_EXPERTS * SUBLANES
    spec = lambda shape, fn: pl.BlockSpec(shape, lambda i, *_: fn(i))
    grid_spec = pltpu.PrefetchScalarGridSpec(
        num_scalar_prefetch=7, grid=(n_tiles,),
        in_specs=_token_specs(tile, d, n_lat_tiles) + [
                  spec((1, d), lambda i: (0, 0)),
                  _mod_spec(row, 4, d), _mod_spec(row, 3, d),
                  spec((TOP_K, tile), lambda i: (0, i))],
        out_specs=pl.BlockSpec(memory_space=pl.ANY),
        scratch_shapes=[pltpu.VMEM((2, rl, d), f32), pltpu.VMEM((EXPERT_ROWS, d), f32),
                        pltpu.VMEM((rl, tile), bf16), pltpu.SemaphoreType.DMA((2,))])
    return pl.pallas_call(
        functools.partial(_dispatch_kernel, n_tiles=n_tiles, n_blocks=n_rows // EXPERT_ROWS, n_lat_tiles=n_lat_tiles),
        grid_spec=grid_spec,
        out_shape=jax.ShapeDtypeStruct((n_rows, d), f32),
        compiler_params=_params("arbitrary"), name="moe_dispatch",
    )(*tables, lat, cx, gain, mod, mod, loc)


def _gmm_kernel(be_ref, bv_ref, lu_ref, x_ref, wgu_ref, bgu_ref, wd_ref, bd_ref, o_ref, wgu_bf, wd_bf):
    j = pl.program_id(0)
    d_ff = wd_ref.shape[0]
    rows = x_ref.shape[0]
    valid = bv_ref[j]

    @pl.when(valid > 0)
    def _():
        new_expert = jnp.logical_or(j == 0, be_ref[j] != be_ref[jnp.maximum(j - 1, 0)])

        @pl.when(new_expert)
        def _():
            wgu_bf[...] = wgu_ref[...].astype(bf16)
            wd_bf[...] = wd_ref[...].astype(bf16)

    def mlp(r0):
        rs = slice(r0, r0 + MLP_ROWS)
        x = x_ref[rs, :].astype(bf16)
        gu = jnp.dot(x, wgu_bf[...], preferred_element_type=f32) + bgu_ref[...]
        g = jnp.minimum(gu[:, :d_ff], SWIGLU_LIMIT)
        u = jnp.clip(gu[:, d_ff:], -SWIGLU_LIMIT, SWIGLU_LIMIT)
        act = (u + 1.0) * g * jax.nn.sigmoid(SWIGLU_ALPHA * g)
        o_ref[rs, :] = jnp.dot(act.astype(bf16), wd_bf[...], preferred_element_type=f32) + bd_ref[...]

    n_groups = rows // MLP_ROWS
    for full in range(n_groups + 1):
        lo, hi = (full - 1) * MLP_ROWS, full * MLP_ROWS
        cond = (valid == 0) if full == 0 else jnp.logical_and(valid > lo, valid <= hi)
        if full == n_groups:
            cond = valid > lo

        @pl.when(cond)
        def _(full=full):
            for gidx in range(full):
                mlp(gidx * MLP_ROWS)
            if full < n_groups:
                o_ref[full * MLP_ROWS:, :] = jnp.zeros((rows - full * MLP_ROWS, o_ref.shape[1]), f32)


def _gmm_call(layer, blk_e, blk_valid, last_used, xs, w_gu, b_gu, w_down, b_down):
    depth, e, d, f2 = w_gu.shape
    d_ff = w_down.shape[2]
    rows = EXPERT_ROWS
    n_blocks = xs.shape[0] // rows
    grid_spec = pltpu.PrefetchScalarGridSpec(
        num_scalar_prefetch=3, grid=(n_blocks,),
        in_specs=[pl.BlockSpec((rows, d), lambda j, be, bv, lu: (jnp.minimum(j, lu[0]), 0)),
                  pl.BlockSpec((None, None, d, f2), lambda j, be, bv, lu: (layer, be[j], 0, 0)),
                  pl.BlockSpec((None, None, 1, f2), lambda j, be, bv, lu: (layer, be[j], 0, 0)),
                  pl.BlockSpec((None, None, d_ff, d), lambda j, be, bv, lu: (layer, be[j], 0, 0)),
                  pl.BlockSpec((None, None, 1, d), lambda j, be, bv, lu: (layer, be[j], 0, 0))],
        out_specs=pl.BlockSpec((rows, d), lambda j, be, bv, lu: (j, 0)),
        scratch_shapes=[pltpu.VMEM((d, f2), bf16), pltpu.VMEM((d_ff, d), bf16)])
    return pl.pallas_call(
        _gmm_kernel, grid_spec=grid_spec,
        out_shape=jax.ShapeDtypeStruct((n_blocks * rows, d), f32),
        compiler_params=_params("arbitrary"), name="moe_experts",
    )(blk_e, blk_valid, last_used, xs, w_gu, b_gu.reshape(depth, e, 1, f2), w_down, b_down.reshape(depth, e, 1, d))


def _combine_kernel(go_ref, lb_ref, p8_ref, np_ref,
                    ys_hbm, loc_ref, gate_ref, x_ref, xc_ref, g2_ref, fn_ref, o_ref, yloc, w_s, sem,
                    *, n_tiles, n_lat_tiles, final):
    i = pl.program_id(0)
    tt = x_ref.shape[0]
    rl = yloc.shape[1]
    slot = i % 2

    def build(loc, gate, dst):
        cols = [(loc[:, k:k + 1].astype(i16), gate[:, k:k + 1].astype(bf16)) for k in range(TOP_K)]
        zero = jnp.zeros((), bf16)
        for c0 in range(0, rl, BUILD_ROWS):
            r_iota = lax.broadcasted_iota(i16, (tt, BUILD_ROWS), 1) + c0
            weights = None
            for at, g in cols:
                m = jnp.where(r_iota == at, g, zero)
                weights = m if weights is None else weights + m
            dst[:, c0:c0 + BUILD_ROWS] = weights

    seg_bits = (tt // SUBLANES).bit_length()
    tile_bits = (rl // SUBLANES).bit_length()

    def fetch(tile, buf):
        def per_expert(e, carry):
            idx = tile * N_EXPERTS + e
            src0 = pl.multiple_of(go_ref[idx], SUBLANES)
            dst0 = pl.multiple_of(lb_ref[idx], SUBLANES)
            _copy_run(lambda off, size: ys_hbm.at[pl.ds(src0 + off, size), :],
                      lambda off, size: yloc.at[buf, pl.ds(dst0 + off, size), :],
                      p8_ref[idx] // SUBLANES, seg_bits, sem.at[buf])
            return carry
        lax.fori_loop(0, N_EXPERTS, per_expert, 0)

    @pl.when(i == 0)
    def _():
        yloc[...] = jnp.zeros(yloc.shape, f32)
        fetch(0, 0)

    @pl.when(i + 1 < n_tiles)
    def _():
        fetch(jnp.minimum(i + 1, n_tiles - 1), 1 - slot)

    _copy_run(lambda off, size: ys_hbm.at[pl.ds(off, size), :], lambda off, size: yloc.at[0, pl.ds(off, size), :],
              np_ref[i], tile_bits, sem.at[slot], wait=True)

    build(loc_ref, gate_ref, w_s)
    y = jnp.dot(w_s[...], yloc[slot].astype(bf16), preferred_element_type=f32)
    out = _token_tile(x_ref, xc_ref, n_lat_tiles) + g2_ref[...] * y
    if final:
        ms = jnp.mean(out * out, axis=-1, keepdims=True)
        out = (out * lax.rsqrt(ms + NORM_EPS)) * fn_ref[...]
    o_ref[...] = out


def _combine_call(tables, ys, loc_t, gates_t, lat, cx, n, n_lat_tiles, mod, row, final_gain, tile, final):
    d = lat.shape[1]
    n_tiles = n // tile
    rl = TOP_K * tile + N_EXPERTS * SUBLANES
    spec = lambda shape, fn: pl.BlockSpec(shape, lambda i, *_: fn(i))
    grid_spec = pltpu.PrefetchScalarGridSpec(
        num_scalar_prefetch=4, grid=(n_tiles,),
        in_specs=[pl.BlockSpec(memory_space=pl.ANY),
                  spec((tile, TOP_K), lambda i: (i, 0)),
                  spec((tile, TOP_K), lambda i: (i, 0))] + _token_specs(tile, d, n_lat_tiles) + [
                  _mod_spec(row, 5, d),
                  spec((1, d), lambda i: (0, 0))],
        out_specs=spec((tile, d), lambda i: (i, 0)),
        scratch_shapes=[pltpu.VMEM((2, rl, d), f32), pltpu.VMEM((tile, rl), bf16),
                        pltpu.SemaphoreType.DMA((2,))])
    return pl.pallas_call(
        functools.partial(_combine_kernel, n_tiles=n_tiles, n_lat_tiles=n_lat_tiles, final=final), grid_spec=grid_spec,
        out_shape=jax.ShapeDtypeStruct((n, d), f32),
        compiler_params=_params("arbitrary"), name="moe_combine",
    )(*tables, ys, loc_t, gates_t, lat, cx, mod, final_gain)


def _moe_layer(layer, lat, cx, gain, mod, seq, ctx_row, w_router, b_router, w_gu, b_gu, w_down, b_down,
               final_gain, final):
    n_lat, d = lat.shape
    n = n_lat + (0 if cx is None else cx.shape[0])
    tile = _pick_tile(ROUTE_TILE, n_lat, n - n_lat if n > n_lat else n_lat, seq)
    n_tiles, n_lat_tiles = n // tile, n_lat // tile
    cx = lat if cx is None else cx
    row = _row_map(n_lat_tiles, seq // tile, ctx_row)

    loc, gate, cnt = _route_call(lat, cx, n, n_lat_tiles, gain, mod, row, w_router[layer].T, b_router[layer], tile)

    rows = EXPERT_ROWS
    count = cnt[:, :, 0].astype(i32)
    seg = (count + SUBLANES - 1) // SUBLANES * SUBLANES
    local_base = jnp.cumsum(seg, axis=1) - seg
    pieces = jnp.sum(seg, axis=1) // SUBLANES
    expert_rows = jnp.sum(seg, axis=0)
    region = (expert_rows + rows - 1) // rows * rows
    region_end = jnp.cumsum(region)
    region_start = region_end - region
    global_off = region_start[None, :] + jnp.cumsum(seg, axis=0) - seg
    n_blocks = (n * TOP_K + n_tiles * N_EXPERTS * (SUBLANES - 1) + N_EXPERTS * (rows - 1) + rows - 1) // rows
    blk_start = jnp.arange(n_blocks, dtype=i32) * rows
    experts = jnp.arange(N_EXPERTS, dtype=i32)
    blk_e = jnp.minimum(jnp.sum(region_end[None, :] <= blk_start[:, None], axis=1), N_EXPERTS - 1).astype(i32)
    content_end = region_start + expert_rows
    blk_end = jnp.sum(jnp.where(blk_e[:, None] == experts[None, :], content_end[None, :], 0), axis=1)
    blk_valid = jnp.clip(blk_end - blk_start, 0, rows).astype(i32)
    n_used = (region_end[-1:] // rows).astype(i32)
    last_used = jnp.maximum(n_used - 1, 0)
    tail_n = ((region - expert_rows) // SUBLANES).astype(i32)
    flat = lambda a: a.reshape(-1).astype(i32)
    tables = (flat(global_off), flat(local_base), flat(seg), pieces.astype(i32))

    xs = _dispatch_call(tables + (content_end.astype(i32), tail_n, n_used), lat, cx, n, n_lat_tiles, gain, mod, row,
                        loc, n_blocks * rows, tile)
    ys = _gmm_call(layer, blk_e, blk_valid, last_used, xs, w_gu, b_gu, w_down, b_down)
    return _combine_call(tables, ys, loc.T, gate.T, lat, cx, n, n_lat_tiles, mod, row, final_gain, tile, final)


def _time_tile_blocks(n_lat_t, n_ctx_t, n_lat_blocks, b):
    def index(i):
        lat_blk = b * n_lat_t + jnp.minimum(i, n_lat_t - 1)
        ctx_blk = n_lat_blocks + b * n_ctx_t + jnp.maximum(i - n_lat_t, 0)
        return (jnp.where(i < n_lat_t, lat_blk, ctx_blk), 0)
    return index


def _lru_in_kernel(*refs, batch, n_lat_t):
    x_refs = refs[:batch]
    gain_ref, sc_ref, sh_ref, scc_ref, shc_ref, w_ref, gg_ref, u_ref, tm_s = refs[batch:]
    tt = x_refs[0].shape[0]
    d_rnn = gg_ref.shape[1]
    is_ctx = pl.program_id(0) >= n_lat_t
    hs = []
    for b in range(batch):
        scale = jnp.where(is_ctx, scc_ref[...], sc_ref[b])
        shift = jnp.where(is_ctx, shc_ref[...], sh_ref[b])
        hs.append(_norm_mod(x_refs[b][...], gain_ref[...], scale, shift).astype(bf16))
    gu = jnp.dot(jnp.concatenate(hs, axis=0), w_ref[...], preferred_element_type=f32)
    gate = gu[:, :d_rnn]
    gelu = 0.5 * gate * (1.0 + jnp.tanh(GELU_C * (gate + 0.044715 * (gate * gate * gate))))
    for out_ref, val in ((gg_ref, gelu), (u_ref, gu[:, d_rnn:])):
        for b in range(batch):
            for s in range(d_rnn // LANES):
                tm_s[s, pl.ds(b, tt, stride=batch), :] = val[b * tt:(b + 1) * tt, s * LANES:(s + 1) * LANES]
        for s in range(d_rnn // LANES):
            out_ref[:, s * LANES:(s + 1) * LANES] = tm_s[s]


def _lru_in_call(tokens, gain, mod, ctx_row, w_in_bf, batch, seq, ctx_len, tt):
    n, d = tokens.shape
    d2 = w_in_bf.shape[1]
    d_rnn = d2 // 2
    n_lat_t, n_ctx_t = seq // tt, ctx_len // tt
    n_lat_blocks = batch * n_lat_t
    x_specs = [pl.BlockSpec((tt, d), _time_tile_blocks(n_lat_t, n_ctx_t, n_lat_blocks, b)) for b in range(batch)]
    mod_lat = lambda chunk: pl.BlockSpec((batch, 1, d), lambda i: (0, 0, chunk))
    mod_ctx = lambda chunk: pl.BlockSpec((None, 1, d), lambda i: (ctx_row, 0, chunk))
    out_spec = pl.BlockSpec((tt * batch, d_rnn), lambda i: (i, 0))
    out_shape = jax.ShapeDtypeStruct(((seq + ctx_len) * batch, d_rnn), f32)
    return pl.pallas_call(
        functools.partial(_lru_in_kernel, batch=batch, n_lat_t=n_lat_t), grid=(n_lat_t + n_ctx_t,),
        in_specs=x_specs + [pl.BlockSpec((1, d), lambda i: (0, 0)), mod_lat(1), mod_lat(0), mod_ctx(1), mod_ctx(0),
                            pl.BlockSpec((d, d2), lambda i: (0, 0))],
        out_specs=[out_spec, out_spec], out_shape=[out_shape, out_shape],
        scratch_shapes=[pltpu.VMEM((d_rnn // LANES, tt * batch, LANES), f32)],
        compiler_params=_params("arbitrary"), name="lru_in",
    )(*([tokens] * batch), gain, mod, mod, mod, mod, w_in_bf)


def _lru_scan_kernel(*refs, batch, n_ctx_c, n_lat_c, reverse):
    if reverse:
        (prev_ref, cur_ref, next_ref, cw_ref, cb_ref, wr_ref, br_ref, wi_ref, bi_ref, lam_ref, hf_ref, gg_ref,
         o_ref, a_s, b_s, h_s, z_s) = refs
    else:
        (prev_ref, cur_ref, next_ref, cw_ref, cb_ref, wr_ref, br_ref, wi_ref, bi_ref, lam_ref,
         o_ref, a_s, b_s, h_s) = refs
    s = pl.program_id(0)
    rows, d_rnn = cur_ref.shape
    steps = rows // batch
    wb = d_rnn // N_LRU_BLOCKS

    @pl.when(s == 0)
    def _():
        h_s[...] = jnp.zeros(h_s.shape, f32)

    c = jnp.where(s < n_ctx_c, s, s - n_ctx_c)
    seg_chunks = jnp.where(s < n_ctx_c, n_ctx_c, n_lat_c)
    c = (seg_chunks - 1 - c) if reverse else c
    has_prev = c > 0
    has_next = c < seg_chunks - 1
    ext = jnp.concatenate([jnp.where(has_prev, prev_ref[...], 0.0), cur_ref[...],
                           jnp.where(has_next, next_ref[...], 0.0)], axis=0)
    u = cb_ref[...]
    for k in range(CONV_WIDTH):
        u = u + ext[k * batch:k * batch + rows] * cw_ref[k:k + 1, :]

    lam = lam_ref[...]
    softplus_neg = jnp.maximum(-lam, 0.0) + jnp.log1p(jnp.exp(-jnp.abs(lam)))
    decay = (-0.5 * LRU_C) * softplus_neg
    for n in range(N_LRU_BLOCKS):
        cols = slice(n * wb, (n + 1) * wb)
        un = u[:, cols]
        ub = un.astype(bf16)
        half_r = jnp.dot(ub, (0.5 * wr_ref[n]).astype(bf16), preferred_element_type=f32) + 0.5 * br_ref[:, cols]
        half_i = jnp.dot(ub, (0.5 * wi_ref[n]).astype(bf16), preferred_element_type=f32) + 0.5 * bi_ref[:, cols]
        log_a = decay[:, cols] * (1.0 + jnp.tanh(half_r))
        a = jnp.exp(log_a)
        mult = jnp.sqrt(-jnp.tanh(log_a) * (a * a + 1.0))
        a_s[:, cols] = a
        b_s[:, cols] = (mult * (0.5 * un)) * (1.0 + jnp.tanh(half_i))

    def step(q, h):
        for j in range(SCAN_UNROLL):
            t = q * SCAN_UNROLL + j
            t = (steps - 1 - t) if reverse else t
            row = pl.multiple_of(t * batch, batch)
            h = a_s[pl.ds(row, batch), :] * h + b_s[pl.ds(row, batch), :]
            if reverse:
                z_s[pl.ds(row, batch), :] = h
            else:
                o_ref[pl.ds(row, batch), :] = h
        return h
    h_s[...] = lax.fori_loop(0, steps // SCAN_UNROLL, step, h_s[...])
    if reverse:
        o_ref[...] = (gg_ref[...] * (hf_ref[...] + z_s[...])).astype(o_ref.dtype)


def _lru_scan_call(u, hf, gg, conv_w, conv_b, w_r, b_r, w_i, b_i, lam, batch, seq, ctx_len, chunk, reverse):
    n_rows, d_rnn = u.shape
    rows = chunk * batch
    n_lat_c, n_ctx_c = seq // chunk, ctx_len // chunk
    pre, post = CONV_LEFT * batch, (CONV_WIDTH - 1 - CONV_LEFT) * batch
    wb = d_rnn // N_LRU_BLOCKS

    def blk(s):
        ci = jnp.where(s < n_ctx_c, s, s - n_ctx_c)
        if reverse:
            ci = jnp.where(s < n_ctx_c, n_ctx_c - 1 - ci, n_lat_c - 1 - ci)
        return jnp.where(s < n_ctx_c, n_lat_c + ci, ci)

    last_post = n_rows // post - 1
    cur = pl.BlockSpec((rows, d_rnn), lambda s: (blk(s), 0))
    in_specs = [pl.BlockSpec((pre, d_rnn), lambda s: (jnp.maximum(blk(s) * (rows // pre) - 1, 0), 0)),
                cur,
                pl.BlockSpec((post, d_rnn), lambda s: (jnp.minimum((blk(s) + 1) * (rows // post), last_post), 0)),
                pl.BlockSpec((CONV_WIDTH, d_rnn), lambda s: (0, 0)),
                pl.BlockSpec((1, d_rnn), lambda s: (0, 0)),
                pl.BlockSpec((N_LRU_BLOCKS, wb, wb), lambda s: (0, 0, 0)),
                pl.BlockSpec((1, d_rnn), lambda s: (0, 0)),
                pl.BlockSpec((N_LRU_BLOCKS, wb, wb), lambda s: (0, 0, 0)),
                pl.BlockSpec((1, d_rnn), lambda s: (0, 0)),
                pl.BlockSpec((1, d_rnn), lambda s: (0, 0))]
    args = [u, u, u, conv_w, conv_b.reshape(1, d_rnn), w_r, b_r.reshape(1, d_rnn), w_i, b_i.reshape(1, d_rnn),
            lam.reshape(1, d_rnn)]
    scratch = [pltpu.VMEM((rows, d_rnn), f32), pltpu.VMEM((rows, d_rnn), f32), pltpu.VMEM((batch, d_rnn), f32)]
    if reverse:
        in_specs += [cur, cur]
        args += [hf, gg]
        scratch.append(pltpu.VMEM((rows, d_rnn), f32))
    return pl.pallas_call(
        functools.partial(_lru_scan_kernel, batch=batch, n_ctx_c=n_ctx_c, n_lat_c=n_lat_c, reverse=reverse),
        grid=(n_ctx_c + n_lat_c,), in_specs=in_specs, out_specs=cur,
        out_shape=jax.ShapeDtypeStruct((n_rows, d_rnn), bf16 if reverse else f32),
        scratch_shapes=scratch, compiler_params=_params("arbitrary"),
        name="lru_scan_rev" if reverse else "lru_scan_fwd",
    )(*args)


def _lru_out_kernel(*refs, batch):
    z_ref = refs[0]
    x_refs = refs[1:1 + batch]
    w_ref, g1_ref, o_ref, y_s = refs[1 + batch:]
    tt = x_refs[0].shape[0]
    y = jnp.dot(z_ref[...], w_ref[...], preferred_element_type=f32)
    n_planes = y.shape[1] // LANES
    for s in range(n_planes):
        y_s[s] = y[:, s * LANES:(s + 1) * LANES]
    for b in range(batch):
        y_b = jnp.concatenate([y_s[s, pl.ds(b, tt, stride=batch), :] for s in range(n_planes)], axis=-1)
        o_ref[b] = x_refs[b][...] + g1_ref[b] * y_b


def _lru_out_call(z, tokens, w_out_bf, mod, batch, seq, tt):
    d = tokens.shape[1]
    d_rnn = z.shape[1]
    n_t = seq // tt
    x_specs = [pl.BlockSpec((tt, d), lambda i, b=b: (b * n_t + i, 0)) for b in range(batch)]
    return pl.pallas_call(
        functools.partial(_lru_out_kernel, batch=batch), grid=(n_t,),
        in_specs=[pl.BlockSpec((tt * batch, d_rnn), lambda i: (i, 0))] + x_specs
                 + [pl.BlockSpec((d_rnn, d), lambda i: (0, 0)), pl.BlockSpec((batch, 1, d), lambda i: (0, 0, 2))],
        out_specs=pl.BlockSpec((batch, tt, d), lambda i: (0, i, 0)),
        out_shape=jax.ShapeDtypeStruct((batch, seq, d), f32),
        scratch_shapes=[pltpu.VMEM((d // LANES, tt * batch, LANES), f32)],
        compiler_params=_params("arbitrary"), name="lru_out",
    )(z, *([tokens] * batch), w_out_bf, mod)


def kernel(x, c, ctx, c_ctx, ada_w, ada_b, norm_mix, norm_ffn, pool_w, pool_scale, lru_w_in, lru_conv_w, lru_conv_b, lru_w_r, lru_b_r, lru_w_i, lru_b_i, lru_lam, lru_w_out, router_w, router_b, exp_w_gu, exp_b_gu, exp_w_down, exp_b_down, final_norm):
    batch, seq, d = x.shape
    ctx_len = ctx.shape[1]
    assert d == SUBLANES * LANES and seq % GRID_W == 0
    n_lat, n_ctx = batch * seq, batch * ctx_len
    ctx_row = batch

    mod_rows = -(-(batch + 1) // SUBLANES) * SUBLANES
    cc = jnp.zeros((mod_rows, d), f32).at[:batch].set(c).at[batch].set(c_ctx)
    mod_all = _ada_call(cc, ada_w, ada_b)
    mods = [mod_all[l].reshape(mod_rows, 1, 6 * d) for l in range(ada_w.shape[0])]

    lat = x.reshape(n_lat, d)
    cx = ctx.reshape(n_ctx, d)
    tile = _pick_tile(ROUTE_TILE, seq, n_ctx)
    row_lat = _row_map(n_lat // tile, seq // tile, ctx_row)
    gain = lambda g: g.reshape(1, d)

    hl = _norm_call(lat, gain(norm_mix[0]), mods[0], 0, 1, row_lat, tile)
    lat = _pool_lat_call(hl, lat, pool_w[0], gain(pool_scale[0]), mods[0], batch, seq)
    cx = _pool_ctx_call(cx, gain(norm_mix[0]), pool_w[0], gain(pool_scale[0]), mods[0], ctx_row, batch, ctx_len)
    tokens = _moe_layer(0, lat, cx, gain(norm_ffn[0]), mods[0], seq, ctx_row, router_w, router_b,
                        exp_w_gu, exp_b_gu, exp_w_down, exp_b_down, gain(final_norm), False)

    assert batch == SUBLANES
    tt = _pick_tile(LRU_TILE, seq, ctx_len)
    chunk = _pick_tile(SCAN_CHUNK, seq, ctx_len)
    gg, u = _lru_in_call(tokens, gain(norm_mix[1]), mods[1], ctx_row, lru_w_in[0].astype(bf16), batch, seq,
                         ctx_len, tt)
    scan = functools.partial(_lru_scan_call, conv_w=lru_conv_w[0], conv_b=lru_conv_b[0], batch=batch, seq=seq,
                             ctx_len=ctx_len, chunk=chunk)
    hf = scan(u, None, None, w_r=lru_w_r[0, 0], b_r=lru_b_r[0, 0], w_i=lru_w_i[0, 0], b_i=lru_b_i[0, 0],
              lam=lru_lam[0, 0], reverse=False)
    z = scan(u, hf, gg, w_r=lru_w_r[0, 1], b_r=lru_b_r[0, 1], w_i=lru_w_i[0, 1], b_i=lru_b_i[0, 1],
             lam=lru_lam[0, 1], reverse=True)
    lat = _lru_out_call(z, tokens, lru_w_out[0].astype(bf16), mods[1], batch, seq, tt).reshape(n_lat, d)
    out = _moe_layer(1, lat, None, gain(norm_ffn[1]), mods[1], seq, ctx_row, router_w, router_b,
                     exp_w_gu, exp_b_gu, exp_w_down, exp_b_down, gain(final_norm), True)
    return out.reshape(batch, seq, d)
```

```python
import functools

import numpy as np
import jax
import jax.numpy as jnp
from jax import lax
from jax.experimental import pallas as pl
from jax.experimental.pallas import tpu as pltpu

f32 = jnp.float32
bf16 = jnp.bfloat16
i32 = jnp.int32
i16 = jnp.int16
HIGHEST = lax.Precision.HIGHEST

LANES = 128
SUBLANES = 8
VMEM_LIMIT = 56 * 1024 * 1024

GRID_W = 64
GRID_SHIFT = 6
POOL_WINDOWS = (2, 4, 8, 16)
N_POOL_GROUPS = 4
POOL_BLOCK = 256
N_LRU_BLOCKS = 4
CONV_LEFT = 2
CONV_WIDTH = 4
LRU_C = 8.0
N_EXPERTS = 32
TOP_K = 4
SWIGLU_LIMIT = 7.0
SWIGLU_ALPHA = 1.702
NORM_EPS = 1e-6
GELU_C = 0.7978845608028654

EXPERT_ROWS = 1024
MLP_ROWS = 256
BUILD_ROWS = 128
ROUTE_TILE = 512
LRU_TILE = 128
SCAN_CHUNK = 64
SCAN_UNROLL = 8


def _params(*sem):
    return pltpu.CompilerParams(dimension_semantics=sem, vmem_limit_bytes=VMEM_LIMIT)


def _pick_tile(pref, *sizes):
    t = pref
    while any(s % t for s in sizes):
        t //= 2
    assert t >= SUBLANES
    return t


def _norm_mod(x, gain, scale, shift):
    ms = jnp.mean(x * x, axis=-1, keepdims=True)
    y = x * lax.rsqrt(ms + NORM_EPS)
    return (y * gain) * (1.0 + scale) + shift


def _row_map(n_lat_tiles, tiles_per_batch, ctx_row):
    def row(i):
        return jnp.where(i < n_lat_tiles, i // tiles_per_batch, ctx_row)
    return row


def _mod_spec(row, chunk, width):
    return pl.BlockSpec((None, 1, width), lambda i, *_: (row(i), 0, chunk))


def _token_specs(tile, d, n_lat_tiles):
    return [pl.BlockSpec((tile, d), lambda i, *_: (jnp.minimum(i, n_lat_tiles - 1), 0)),
            pl.BlockSpec((tile, d), lambda i, *_: (jnp.maximum(i - n_lat_tiles, 0), 0))]


def _token_tile(x_ref, xc_ref, n_lat_tiles):
    return jnp.where(pl.program_id(0) >= n_lat_tiles, xc_ref[...], x_ref[...])


def _ada_kernel(c_ref, w_ref, b_ref, o_ref):
    c = c_ref[...]
    s = c * jax.nn.sigmoid(c)
    o_ref[...] = jnp.dot(s, w_ref[...], preferred_element_type=f32, precision=HIGHEST) + b_ref[...]


def _ada_call(cc, ada_w, ada_b):
    depth, d, d6 = ada_w.shape
    rows = cc.shape[0]
    bn = 1536
    return pl.pallas_call(
        _ada_kernel, grid=(depth, d6 // bn),
        in_specs=[pl.BlockSpec((rows, d), lambda l, j: (0, 0)),
                  pl.BlockSpec((None, d, bn), lambda l, j: (l, 0, j)),
                  pl.BlockSpec((None, 1, bn), lambda l, j: (l, 0, j))],
        out_specs=pl.BlockSpec((None, rows, bn), lambda l, j: (l, 0, j)),
        out_shape=jax.ShapeDtypeStruct((depth, rows, d6), f32),
        compiler_params=_params("arbitrary", "arbitrary"), name="ada_mod",
    )(cc, ada_w, ada_b.reshape(depth, 1, d6))


def _rms_kernel(x_ref, o_ref):
    x = x_ref[...]
    o_ref[...] = lax.rsqrt(jnp.mean(x * x, axis=-1, keepdims=True) + NORM_EPS)


def _rms_call(tokens, tile):
    n, d = tokens.shape
    return pl.pallas_call(
        _rms_kernel, grid=(n // tile,),
        in_specs=[pl.BlockSpec((tile, d), lambda i: (i, 0))],
        out_specs=pl.BlockSpec((tile, 1), lambda i: (i, 0)),
        out_shape=jax.ShapeDtypeStruct((n, 1), f32),
        compiler_params=_params("arbitrary"), name="row_rms",
    )(tokens)


def _shift(x, j, unit, pos, limit):
    n = x.shape[0]
    rolled = pltpu.roll(x, (j * unit) % n, 0)
    ok = (pos >= j) if j > 0 else (pos < limit + j)
    return jnp.where(ok, rolled, 0.0)


def _window_sum(x, w, unit, pos, limit):
    m = w // 2
    trail, lead, s = x, x, 1
    while s < m:
        trail = trail + _shift(trail, s, unit, pos, limit)
        lead = lead + _shift(lead, -s, unit, pos, limit)
        s *= 2
    return _shift(trail, 1, unit, pos, limit) + lead


def _window_count(w, pos, limit):
    m = w // 2
    return jnp.minimum(pos + m, limit) - jnp.maximum(pos - m, 0)


def _col_window_sum(h, w):
    blk = POOL_BLOCK
    i = lax.broadcasted_iota(i32, (blk, blk), 0)
    j = lax.broadcasted_iota(i32, (blk, blk), 1)
    off = (j & (GRID_W - 1)) - (i & (GRID_W - 1))
    same_row = (i >> GRID_SHIFT) == (j >> GRID_SHIFT)
    band = jnp.where(jnp.logical_and(same_row, jnp.logical_and(off >= -(w // 2), off < w // 2)), 1.0, 0.0)
    band = band.astype(bf16)
    out = []
    for r0 in range(0, h.shape[0], blk):
        hb = h[r0:r0 + blk]
        hi = hb.astype(bf16)
        lo = (hb - hi.astype(f32)).astype(bf16)
        out.append(jnp.dot(band, hi, preferred_element_type=f32) + jnp.dot(band, lo, preferred_element_type=f32))
    return jnp.concatenate(out, axis=0)


def _pool_grid(h, w):
    t = lax.broadcasted_iota(i32, h.shape, 0)
    col, row = t & (GRID_W - 1), t >> GRID_SHIFT
    rows = h.shape[0] // GRID_W
    total = _window_sum(_col_window_sum(h, w), w, GRID_W, row, rows)
    count = (_window_count(w, row, rows) * _window_count(w, col, GRID_W)).astype(f32)
    return total / count


def _pool_seq(h, w):
    t = lax.broadcasted_iota(i32, h.shape, 0)
    n = h.shape[0]
    return _window_sum(h, w, 1, t, n) / _window_count(w, t, n).astype(f32)


def _pool_lat_kernel(x_ref, r_ref, gain_ref, sc_ref, sh_ref, w_ref, ps_ref, g1_ref, o_ref):
    grp = pl.program_id(1)
    for gi, w in enumerate(POOL_WINDOWS):
        @pl.when(grp == gi)
        def _(w=w):
            x = x_ref[...]
            h = ((x * r_ref[...]) * gain_ref[...]) * (1.0 + sc_ref[...]) + sh_ref[...]
            d = _pool_grid(h, w) - h
            y = jnp.dot(d.astype(bf16), w_ref[...].astype(bf16), preferred_element_type=f32)
            o_ref[...] = x + g1_ref[...] * (y * ps_ref[...])


def _pool_lat_call(x, rms, gain, pool_w, pool_scale, mod, batch, seq):
    n, d = x.shape
    cg = d // N_POOL_GROUPS
    mod_chunk = lambda c: pl.BlockSpec((None, 1, cg), lambda b, g: (b, 0, c * N_POOL_GROUPS + g))
    return pl.pallas_call(
        _pool_lat_kernel, grid=(batch, N_POOL_GROUPS),
        in_specs=[pl.BlockSpec((seq, cg), lambda b, g: (b, g)),
                  pl.BlockSpec((seq, 1), lambda b, g: (b, 0)),
                  pl.BlockSpec((1, cg), lambda b, g: (0, g)),
                  mod_chunk(1), mod_chunk(0),
                  pl.BlockSpec((None, cg, cg), lambda b, g: (g, 0, 0)),
                  pl.BlockSpec((1, cg), lambda b, g: (0, g)),
                  mod_chunk(2)],
        out_specs=pl.BlockSpec((seq, cg), lambda b, g: (b, g)),
        out_shape=jax.ShapeDtypeStruct((n, d), f32),
        compiler_params=_params("arbitrary", "arbitrary"), name="pool_lat",
    )(x, rms, gain, mod, mod, pool_w, pool_scale, mod)


def _pool_ctx_kernel(x_ref, gain_ref, sc_ref, sh_ref, w_ref, ps_ref, g1_ref, o_ref):
    x = x_ref[...]
    h = _norm_mod(x, gain_ref[...], sc_ref[...], sh_ref[...])
    cg = x.shape[1] // N_POOL_GROUPS
    for gi, w in enumerate(POOL_WINDOWS):
        sl = slice(gi * cg, (gi + 1) * cg)
        hg = h[:, sl]
        d = _pool_seq(hg, w) - hg
        y = jnp.dot(d.astype(bf16), w_ref[gi].astype(bf16), preferred_element_type=f32)
        o_ref[:, sl] = x[:, sl] + g1_ref[:, sl] * (y * ps_ref[:, sl])


def _pool_ctx_call(cx, gain, pool_w, pool_scale, mod, ctx_row, batch, ctx_len):
    n, d = cx.shape
    cg = d // N_POOL_GROUPS
    row = lambda i: ctx_row
    return pl.pallas_call(
        _pool_ctx_kernel, grid=(batch,),
        in_specs=[pl.BlockSpec((ctx_len, d), lambda b: (b, 0)),
                  pl.BlockSpec((1, d), lambda b: (0, 0)),
                  _mod_spec(row, 1, d), _mod_spec(row, 0, d),
                  pl.BlockSpec((N_POOL_GROUPS, cg, cg), lambda b: (0, 0, 0)),
                  pl.BlockSpec((1, d), lambda b: (0, 0)),
                  _mod_spec(row, 2, d)],
        out_specs=pl.BlockSpec((ctx_len, d), lambda b: (b, 0)),
        out_shape=jax.ShapeDtypeStruct((n, d), f32),
        compiler_params=_params("arbitrary"), name="pool_ctx",
    )(cx, gain, mod, mod, pool_w, pool_scale, mod)


def _route_kernel(x_ref, xc_ref, gain_ref, sc_ref, sh_ref, wr_ref, rb_ref, loc_ref, gate_ref, cnt_ref, *, n_lat_tiles):
    tt = x_ref.shape[0]
    h = _norm_mod(_token_tile(x_ref, xc_ref, n_lat_tiles), gain_ref[...], sc_ref[...], sh_ref[...])

    def split(v):
        hi = v.astype(bf16)
        return hi, (v - hi.astype(f32)).astype(bf16)
    contract_last = (((1,), (1,)), ((), ()))
    w_hi, w_lo = split(wr_ref[...])
    h_hi, h_lo = split(h)
    by_hi = lax.dot_general(jnp.concatenate([w_hi, w_lo], axis=0), h_hi, contract_last, preferred_element_type=f32)
    by_lo = lax.dot_general(w_hi, h_lo, contract_last, preferred_element_type=f32)
    logits = by_hi[:N_EXPERTS] + (by_hi[N_EXPERTS:] + by_lo) + rb_ref[...]
    e_iota = lax.broadcasted_iota(i32, logits.shape, 0)
    vals, onehots = [], []
    work = logits
    for k in range(TOP_K):
        m = jnp.max(work, axis=0, keepdims=True)
        idx = jnp.min(jnp.where(work == m, e_iota, N_EXPERTS), axis=0, keepdims=True)
        hit = e_iota == idx
        vals.append(m)
        onehots.append(jnp.where(hit, 1.0, 0.0))
        work = jnp.where(hit, -jnp.inf, work)

    ex = [jnp.exp(v - vals[0]) for v in vals]
    den = ex[0] + ex[1] + ex[2] + ex[3]
    for k in range(TOP_K):
        gate_ref[k:k + 1, :] = ex[k] / den

    tri = jnp.where(lax.broadcasted_iota(i32, (tt, tt), 0) <= lax.broadcasted_iota(i32, (tt, tt), 1), 1.0, 0.0)
    planes = jnp.concatenate(onehots, axis=0).astype(bf16)
    incl = jnp.dot(planes, tri.astype(bf16), preferred_element_type=f32)
    totals = [incl[k * N_EXPERTS:(k + 1) * N_EXPERTS, tt - 1:tt] for k in range(TOP_K)]
    count = jnp.broadcast_to(totals[0] + totals[1] + totals[2] + totals[3], (N_EXPERTS, LANES))
    padded = (((count.astype(i32) + (SUBLANES - 1)) // SUBLANES) * SUBLANES).astype(f32)
    below = jnp.where(lax.broadcasted_iota(i32, (N_EXPERTS, N_EXPERTS), 1)
                      < lax.broadcasted_iota(i32, (N_EXPERTS, N_EXPERTS), 0), 1.0, 0.0)
    offset = jnp.dot(below.astype(bf16), padded.astype(bf16), preferred_element_type=f32)[:, 0:1]
    for k in range(TOP_K):
        inc_k = incl[k * N_EXPERTS:(k + 1) * N_EXPERTS]
        before = inc_k - onehots[k] + offset
        loc_ref[k:k + 1, :] = jnp.sum(onehots[k] * before, axis=0, keepdims=True).astype(i32)
        offset = offset + totals[k]
    cnt_ref[...] = count


def _route_call(lat, cx, n, n_lat_tiles, gain, mod, row, w_router_t, b_router, tile):
    d = lat.shape[1]
    e = w_router_t.shape[0]
    return pl.pallas_call(
        functools.partial(_route_kernel, n_lat_tiles=n_lat_tiles), grid=(n // tile,),
        in_specs=_token_specs(tile, d, n_lat_tiles) + [
                  pl.BlockSpec((1, d), lambda i: (0, 0)),
                  _mod_spec(row, 4, d), _mod_spec(row, 3, d),
                  pl.BlockSpec((e, d), lambda i: (0, 0)),
                  pl.BlockSpec((e, 1), lambda i: (0, 0))],
        out_specs=[pl.BlockSpec((TOP_K, tile), lambda i: (0, i)),
                   pl.BlockSpec((TOP_K, tile), lambda i: (0, i)),
                   pl.BlockSpec((None, e, LANES), lambda i: (i, 0, 0))],
        out_shape=[jax.ShapeDtypeStruct((TOP_K, n), i32),
                   jax.ShapeDtypeStruct((TOP_K, n), f32),
                   jax.ShapeDtypeStruct((n // tile, e, LANES), f32)],
        compiler_params=_params("arbitrary"), name="moe_route",
    )(lat, cx, gain, mod, mod, w_router_t, b_router.reshape(e, 1))


def _copy_run(src_at, dst_at, n_pieces, n_bits, sem, wait=False):
    for k in reversed(range(n_bits)):
        size = SUBLANES << k
        done = (n_pieces >> (k + 1)) << (k + 1)

        @pl.when(((n_pieces >> k) & 1) == 1)
        def _(size=size, done=done):
            off = 0 if wait else pl.multiple_of(done * SUBLANES, SUBLANES)
            copy = pltpu.make_async_copy(src_at(off, size), dst_at(off, size), sem)
            copy.wait() if wait else copy.start()


def _dispatch_kernel(go_ref, lb_ref, p8_ref, np_ref, ts_ref, tn_ref, nu_ref,
                     x_ref, xc_ref, gain_ref, sc_ref, sh_ref, loc_ref, xs_hbm, xloc, zbuf, sel_s, sem,
                     *, n_tiles, n_blocks, n_lat_tiles):
    i = pl.program_id(0)
    tt = x_ref.shape[0]
    rl = xloc.shape[1]
    slot = i % 2

    def build(loc, dst):
        rows = [loc[k:k + 1, :].astype(i16) for k in range(TOP_K)]
        one, zero = jnp.ones((), bf16), jnp.zeros((), bf16)
        for r0 in range(0, rl, BUILD_ROWS):
            r_iota = lax.broadcasted_iota(i16, (BUILD_ROWS, tt), 0) + r0
            sel = None
            for k in range(TOP_K):
                m = jnp.where(r_iota == rows[k], one, zero)
                sel = m if sel is None else sel + m
            dst[r0:r0 + BUILD_ROWS, :] = sel

    fill_sem = sem.at[2]
    tail_bits = (EXPERT_ROWS // SUBLANES - 1).bit_length()
    zero_at = lambda off, size: zbuf.at[pl.ds(0, size), :]

    @pl.when(i == 0)
    def _():
        zbuf[...] = jnp.zeros(zbuf.shape, f32)

        def tail(e, carry):
            t0 = pl.multiple_of(ts_ref[e], SUBLANES)
            _copy_run(zero_at, lambda off, size: xs_hbm.at[pl.ds(t0 + off, size), :], tn_ref[e], tail_bits,
                      fill_sem)
            return carry
        lax.fori_loop(0, N_EXPERTS, tail, 0)

        def spare(b, carry):
            t = pl.multiple_of(b * EXPERT_ROWS, EXPERT_ROWS)
            pltpu.make_async_copy(zbuf, xs_hbm.at[pl.ds(t, EXPERT_ROWS), :], fill_sem).start()
            return carry
        lax.fori_loop(nu_ref[0], n_blocks, spare, 0)

    build(loc_ref, sel_s)
    h = _norm_mod(_token_tile(x_ref, xc_ref, n_lat_tiles), gain_ref[...], sc_ref[...], sh_ref[...]).astype(bf16)
    xloc[slot] = jnp.dot(sel_s[...], h, preferred_element_type=f32)

    seg_bits = (tt // SUBLANES).bit_length()
    tile_bits = (rl // SUBLANES).bit_length()
    buf_at = lambda off, size: xloc.at[0, pl.ds(off, size), :]
    hbm_at = lambda off, size: xs_hbm.at[pl.ds(off, size), :]

    @pl.when(i > 0)
    def _():
        _copy_run(buf_at, hbm_at, np_ref[jnp.maximum(i - 1, 0)], tile_bits, sem.at[1 - slot], wait=True)

    def per_expert(e, carry):
        idx = i * N_EXPERTS + e
        src0 = pl.multiple_of(lb_ref[idx], SUBLANES)
        dst0 = pl.multiple_of(go_ref[idx], SUBLANES)
        _copy_run(lambda off, size: xloc.at[slot, pl.ds(src0 + off, size), :],
                  lambda off, size: xs_hbm.at[pl.ds(dst0 + off, size), :],
                  p8_ref[idx] // SUBLANES, seg_bits, sem.at[slot])
        return carry
    lax.fori_loop(0, N_EXPERTS, per_expert, 0)

    @pl.when(i == n_tiles - 1)
    def _():
        _copy_run(buf_at, hbm_at, np_ref[i], tile_bits, sem.at[slot], wait=True)

        def tail_wait(e, carry):
            _copy_run(zero_at, hbm_at, tn_ref[e], tail_bits, fill_sem, wait=True)
            return carry
        lax.fori_loop(0, N_EXPERTS, tail_wait, 0)

        def spare_wait(b, carry):
            pltpu.make_async_copy(zbuf, xs_hbm.at[pl.ds(0, EXPERT_ROWS), :], fill_sem).wait()
            return carry
        lax.fori_loop(nu_ref[0], n_blocks, spare_wait, 0)


def _dispatch_call(tables, lat, cx, n, n_lat_tiles, gain, mod, row, loc, n_rows, tile):
    d = lat.shape[1]
    n_tiles = n // tile
    rl = TOP_K * tile + N_EXPERTS * SUBLANES
    spec = lambda shape, fn: pl.BlockSpec(shape, lambda i, *_: fn(i))
    grid_spec = pltpu.PrefetchScalarGridSpec(
        num_scalar_prefetch=7, grid=(n_tiles,),
        in_specs=_token_specs(tile, d, n_lat_tiles) + [
                  spec((1, d), lambda i: (0, 0)),
                  _mod_spec(row, 4, d), _mod_spec(row, 3, d),
                  spec((TOP_K, tile), lambda i: (0, i))],
        out_specs=pl.BlockSpec(memory_space=pl.ANY),
        scratch_shapes=[pltpu.VMEM((2, rl, d), f32), pltpu.VMEM((EXPERT_ROWS, d), f32),
                        pltpu.VMEM((rl, tile), bf16), pltpu.SemaphoreType.DMA((3,))])
    return pl.pallas_call(
        functools.partial(_dispatch_kernel, n_tiles=n_tiles, n_blocks=n_rows // EXPERT_ROWS, n_lat_tiles=n_lat_tiles),
        grid_spec=grid_spec,
        out_shape=jax.ShapeDtypeStruct((n_rows, d), f32),
        compiler_params=_params("arbitrary"), name="moe_dispatch",
    )(*tables, lat, cx, gain, mod, mod, loc)


def _gmm_kernel(be_ref, bv_ref, lu_ref, x_ref, wgu_ref, bgu_ref, wd_ref, bd_ref, o_ref, wgu_bf, wd_bf):
    j = pl.program_id(0)
    d_ff = wd_ref.shape[0]
    rows = x_ref.shape[0]
    valid = bv_ref[j]

    @pl.when(valid > 0)
    def _():
        new_expert = jnp.logical_or(j == 0, be_ref[j] != be_ref[jnp.maximum(j - 1, 0)])

        @pl.when(new_expert)
        def _():
            wgu_bf[...] = wgu_ref[...].astype(bf16)
            wd_bf[...] = wd_ref[...].astype(bf16)

    def mlp(r0):
        rs = slice(r0, r0 + MLP_ROWS)
        x = x_ref[rs, :].astype(bf16)
        gu = jnp.dot(x, wgu_bf[...], preferred_element_type=f32) + bgu_ref[...]
        g = jnp.minimum(gu[:, :d_ff], SWIGLU_LIMIT)
        u = jnp.clip(gu[:, d_ff:], -SWIGLU_LIMIT, SWIGLU_LIMIT)
        act = (u + 1.0) * g * jax.nn.sigmoid(SWIGLU_ALPHA * g)
        o_ref[rs, :] = jnp.dot(act.astype(bf16), wd_bf[...], preferred_element_type=f32) + bd_ref[...]

    n_groups = rows // MLP_ROWS
    for full in range(n_groups + 1):
        lo, hi = (full - 1) * MLP_ROWS, full * MLP_ROWS
        cond = (valid == 0) if full == 0 else jnp.logical_and(valid > lo, valid <= hi)
        if full == n_groups:
            cond = valid > lo

        @pl.when(cond)
        def _(full=full):
            for gidx in range(full):
                mlp(gidx * MLP_ROWS)
            if full < n_groups:
                o_ref[full * MLP_ROWS:, :] = jnp.zeros((rows - full * MLP_ROWS, o_ref.shape[1]), f32)


def _gmm_call(layer, blk_e, blk_valid, last_used, xs, w_gu, b_gu, w_down, b_down):
    depth, e, d, f2 = w_gu.shape
    d_ff = w_down.shape[2]
    rows = EXPERT_ROWS
    n_blocks = xs.shape[0] // rows
    grid_spec = pltpu.PrefetchScalarGridSpec(
        num_scalar_prefetch=3, grid=(n_blocks,),
        in_specs=[pl.BlockSpec((rows, d), lambda j, be, bv, lu: (jnp.minimum(j, lu[0]), 0)),
                  pl.BlockSpec((None, None, d, f2), lambda j, be, bv, lu: (layer, be[j], 0, 0)),
                  pl.BlockSpec((None, None, 1, f2), lambda j, be, bv, lu: (layer, be[j], 0, 0)),
                  pl.BlockSpec((None, None, d_ff, d), lambda j, be, bv, lu: (layer, be[j], 0, 0)),
                  pl.BlockSpec((None, None, 1, d), lambda j, be, bv, lu: (layer, be[j], 0, 0))],
        out_specs=pl.BlockSpec((rows, d), lambda j, be, bv, lu: (j, 0)),
        scratch_shapes=[pltpu.VMEM((d, f2), bf16), pltpu.VMEM((d_ff, d), bf16)])
    return pl.pallas_call(
        _gmm_kernel, grid_spec=grid_spec,
        out_shape=jax.ShapeDtypeStruct((n_blocks * rows, d), f32),
        compiler_params=_params("arbitrary"), name="moe_experts",
    )(blk_e, blk_valid, last_used, xs, w_gu, b_gu.reshape(depth, e, 1, f2), w_down, b_down.reshape(depth, e, 1, d))


def _combine_kernel(go_ref, lb_ref, p8_ref, np_ref,
                    ys_hbm, loc_ref, gate_ref, x_ref, xc_ref, g2_ref, fn_ref, o_ref, yloc, w_s, sem,
                    *, n_tiles, n_lat_tiles, final):
    i = pl.program_id(0)
    tt = x_ref.shape[0]
    rl = yloc.shape[1]
    slot = i % 2

    def build(loc, gate, dst):
        cols = [(loc[:, k:k + 1].astype(i16), gate[:, k:k + 1].astype(bf16)) for k in range(TOP_K)]
        zero = jnp.zeros((), bf16)
        for c0 in range(0, rl, BUILD_ROWS):
            r_iota = lax.broadcasted_iota(i16, (tt, BUILD_ROWS), 1) + c0
            weights = None
            for at, g in cols:
                m = jnp.where(r_iota == at, g, zero)
                weights = m if weights is None else weights + m
            dst[:, c0:c0 + BUILD_ROWS] = weights

    seg_bits = (tt // SUBLANES).bit_length()
    tile_bits = (rl // SUBLANES).bit_length()

    def fetch(tile, buf):
        def per_expert(e, carry):
            idx = tile * N_EXPERTS + e
            src0 = pl.multiple_of(go_ref[idx], SUBLANES)
            dst0 = pl.multiple_of(lb_ref[idx], SUBLANES)
            _copy_run(lambda off, size: ys_hbm.at[pl.ds(src0 + off, size), :],
                      lambda off, size: yloc.at[buf, pl.ds(dst0 + off, size), :],
                      p8_ref[idx] // SUBLANES, seg_bits, sem.at[buf])
            return carry
        lax.fori_loop(0, N_EXPERTS, per_expert, 0)

    @pl.when(i == 0)
    def _():
        yloc[...] = jnp.zeros(yloc.shape, f32)
        fetch(0, 0)

    @pl.when(i + 1 < n_tiles)
    def _():
        fetch(jnp.minimum(i + 1, n_tiles - 1), 1 - slot)

    _copy_run(lambda off, size: ys_hbm.at[pl.ds(off, size), :], lambda off, size: yloc.at[0, pl.ds(off, size), :],
              np_ref[i], tile_bits, sem.at[slot], wait=True)

    build(loc_ref, gate_ref, w_s)
    y = jnp.dot(w_s[...], yloc[slot].astype(bf16), preferred_element_type=f32)
    out = _token_tile(x_ref, xc_ref, n_lat_tiles) + g2_ref[...] * y
    if final:
        ms = jnp.mean(out * out, axis=-1, keepdims=True)
        out = (out * lax.rsqrt(ms + NORM_EPS)) * fn_ref[...]
    o_ref[...] = out


def _combine_call(tables, ys, loc_t, gates_t, lat, cx, n, n_lat_tiles, mod, row, final_gain, tile, final):
    d = lat.shape[1]
    n_tiles = n // tile
    rl = TOP_K * tile + N_EXPERTS * SUBLANES
    spec = lambda shape, fn: pl.BlockSpec(shape, lambda i, *_: fn(i))
    grid_spec = pltpu.PrefetchScalarGridSpec(
        num_scalar_prefetch=4, grid=(n_tiles,),
        in_specs=[pl.BlockSpec(memory_space=pl.ANY),
                  spec((tile, TOP_K), lambda i: (i, 0)),
                  spec((tile, TOP_K), lambda i: (i, 0))] + _token_specs(tile, d, n_lat_tiles) + [
                  _mod_spec(row, 5, d),
                  spec((1, d), lambda i: (0, 0))],
        out_specs=spec((tile, d), lambda i: (i, 0)),
        scratch_shapes=[pltpu.VMEM((2, rl, d), f32), pltpu.VMEM((tile, rl), bf16),
                        pltpu.SemaphoreType.DMA((2,))])
    return pl.pallas_call(
        functools.partial(_combine_kernel, n_tiles=n_tiles, n_lat_tiles=n_lat_tiles, final=final), grid_spec=grid_spec,
        out_shape=jax.ShapeDtypeStruct((n, d), f32),
        compiler_params=_params("arbitrary"), name="moe_combine",
    )(*tables, ys, loc_t, gates_t, lat, cx, mod, final_gain)


def _moe_layer(layer, lat, cx, gain, mod, seq, ctx_row, w_router, b_router, w_gu, b_gu, w_down, b_down,
               final_gain, final):
    n_lat, d = lat.shape
    n = n_lat + (0 if cx is None else cx.shape[0])
    tile = _pick_tile(ROUTE_TILE, n_lat, n - n_lat if n > n_lat else n_lat, seq)
    n_tiles, n_lat_tiles = n // tile, n_lat // tile
    cx = lat if cx is None else cx
    row = _row_map(n_lat_tiles, seq // tile, ctx_row)

    loc, gate, cnt = _route_call(lat, cx, n, n_lat_tiles, gain, mod, row, w_router[layer].T, b_router[layer], tile)

    rows = EXPERT_ROWS
    count = cnt[:, :, 0].astype(i32)
    seg = (count + SUBLANES - 1) // SUBLANES * SUBLANES
    local_base = jnp.cumsum(seg, axis=1) - seg
    pieces = jnp.sum(seg, axis=1) // SUBLANES
    expert_rows = jnp.sum(seg, axis=0)
    region = (expert_rows + rows - 1) // rows * rows
    region_end = jnp.cumsum(region)
    region_start = region_end - region
    global_off = region_start[None, :] + jnp.cumsum(seg, axis=0) - seg
    n_blocks = (n * TOP_K + n_tiles * N_EXPERTS * (SUBLANES - 1) + N_EXPERTS * (rows - 1) + rows - 1) // rows
    blk_start = jnp.arange(n_blocks, dtype=i32) * rows
    experts = jnp.arange(N_EXPERTS, dtype=i32)
    blk_e = jnp.minimum(jnp.sum(region_end[None, :] <= blk_start[:, None], axis=1), N_EXPERTS - 1).astype(i32)
    content_end = region_start + expert_rows
    blk_end = jnp.sum(jnp.where(blk_e[:, None] == experts[None, :], content_end[None, :], 0), axis=1)
    blk_valid = jnp.clip(blk_end - blk_start, 0, rows).astype(i32)
    n_used = (region_end[-1:] // rows).astype(i32)
    last_used = jnp.maximum(n_used - 1, 0)
    tail_n = ((region - expert_rows) // SUBLANES).astype(i32)
    flat = lambda a: a.reshape(-1).astype(i32)
    tables = (flat(global_off), flat(local_base), flat(seg), pieces.astype(i32))

    xs = _dispatch_call(tables + (content_end.astype(i32), tail_n, n_used), lat, cx, n, n_lat_tiles, gain, mod, row,
                        loc, n_blocks * rows, tile)
    ys = _gmm_call(layer, blk_e, blk_valid, last_used, xs, w_gu, b_gu, w_down, b_down)
    return _combine_call(tables, ys, loc.T, gate.T, lat, cx, n, n_lat_tiles, mod, row, final_gain, tile, final)


def _time_tile_blocks(n_lat_t, n_ctx_t, n_lat_blocks, b):
    def index(i):
        lat_blk = b * n_lat_t + jnp.minimum(i, n_lat_t - 1)
        ctx_blk = n_lat_blocks + b * n_ctx_t + jnp.maximum(i - n_lat_t, 0)
        return (jnp.where(i < n_lat_t, lat_blk, ctx_blk), 0)
    return index


def _lru_in_kernel(*refs, batch, n_lat_t):
    x_refs = refs[:batch]
    gain_ref, sc_ref, sh_ref, scc_ref, shc_ref, w_ref, gg_ref, u_ref, tm_s = refs[batch:]
    tt = x_refs[0].shape[0]
    d_rnn = gg_ref.shape[1]
    is_ctx = pl.program_id(0) >= n_lat_t
    hs = []
    for b in range(batch):
        scale = jnp.where(is_ctx, scc_ref[...], sc_ref[b])
        shift = jnp.where(is_ctx, shc_ref[...], sh_ref[b])
        hs.append(_norm_mod(x_refs[b][...], gain_ref[...], scale, shift).astype(bf16))
    gu = jnp.dot(jnp.concatenate(hs, axis=0), w_ref[...], preferred_element_type=f32)
    gate = gu[:, :d_rnn]
    gelu = 0.5 * gate * (1.0 + jnp.tanh(GELU_C * (gate + 0.044715 * (gate * gate * gate))))
    for out_ref, val in ((gg_ref, gelu), (u_ref, gu[:, d_rnn:])):
        for b in range(batch):
            for s in range(d_rnn // LANES):
                tm_s[s, pl.ds(b, tt, stride=batch), :] = val[b * tt:(b + 1) * tt, s * LANES:(s + 1) * LANES]
        for s in range(d_rnn // LANES):
            out_ref[:, s * LANES:(s + 1) * LANES] = tm_s[s]


def _lru_in_call(tokens, gain, mod, ctx_row, w_in_bf, batch, seq, ctx_len, tt):
    n, d = tokens.shape
    d2 = w_in_bf.shape[1]
    d_rnn = d2 // 2
    n_lat_t, n_ctx_t = seq // tt, ctx_len // tt
    n_lat_blocks = batch * n_lat_t
    x_specs = [pl.BlockSpec((tt, d), _time_tile_blocks(n_lat_t, n_ctx_t, n_lat_blocks, b)) for b in range(batch)]
    mod_lat = lambda chunk: pl.BlockSpec((batch, 1, d), lambda i: (0, 0, chunk))
    mod_ctx = lambda chunk: pl.BlockSpec((None, 1, d), lambda i: (ctx_row, 0, chunk))
    out_spec = pl.BlockSpec((tt * batch, d_rnn), lambda i: (i, 0))
    out_shape = jax.ShapeDtypeStruct(((seq + ctx_len) * batch, d_rnn), f32)
    return pl.pallas_call(
        functools.partial(_lru_in_kernel, batch=batch, n_lat_t=n_lat_t), grid=(n_lat_t + n_ctx_t,),
        in_specs=x_specs + [pl.BlockSpec((1, d), lambda i: (0, 0)), mod_lat(1), mod_lat(0), mod_ctx(1), mod_ctx(0),
                            pl.BlockSpec((d, d2), lambda i: (0, 0))],
        out_specs=[out_spec, out_spec], out_shape=[out_shape, out_shape],
        scratch_shapes=[pltpu.VMEM((d_rnn // LANES, tt * batch, LANES), f32)],
        compiler_params=_params("arbitrary"), name="lru_in",
    )(*([tokens] * batch), gain, mod, mod, mod, mod, w_in_bf)


def _lru_scan_kernel(*refs, batch, n_ctx_c, n_lat_c, reverse):
    if reverse:
        (prev_ref, cur_ref, next_ref, cw_ref, cb_ref, wr_ref, br_ref, wi_ref, bi_ref, lam_ref, hf_ref, gg_ref,
         o_ref, a_s, b_s, h_s, z_s) = refs
    else:
        (prev_ref, cur_ref, next_ref, cw_ref, cb_ref, wr_ref, br_ref, wi_ref, bi_ref, lam_ref,
         o_ref, a_s, b_s, h_s) = refs
    s = pl.program_id(0)
    rows, d_rnn = cur_ref.shape
    steps = rows // batch
    wb = d_rnn // N_LRU_BLOCKS

    @pl.when(s == 0)
    def _():
        h_s[...] = jnp.zeros(h_s.shape, f32)

    c = jnp.where(s < n_ctx_c, s, s - n_ctx_c)
    seg_chunks = jnp.where(s < n_ctx_c, n_ctx_c, n_lat_c)
    c = (seg_chunks - 1 - c) if reverse else c
    has_prev = c > 0
    has_next = c < seg_chunks - 1
    ext = jnp.concatenate([jnp.where(has_prev, prev_ref[...], 0.0), cur_ref[...],
                           jnp.where(has_next, next_ref[...], 0.0)], axis=0)
    u = cb_ref[...]
    for k in range(CONV_WIDTH):
        u = u + ext[k * batch:k * batch + rows] * cw_ref[k:k + 1, :]

    lam = lam_ref[...]
    softplus_neg = jnp.maximum(-lam, 0.0) + jnp.log1p(jnp.exp(-jnp.abs(lam)))
    decay = (-0.5 * LRU_C) * softplus_neg
    for n in range(N_LRU_BLOCKS):
        cols = slice(n * wb, (n + 1) * wb)
        un = u[:, cols]
        ub = un.astype(bf16)
        half_r = jnp.dot(ub, (0.5 * wr_ref[n]).astype(bf16), preferred_element_type=f32) + 0.5 * br_ref[:, cols]
        half_i = jnp.dot(ub, (0.5 * wi_ref[n]).astype(bf16), preferred_element_type=f32) + 0.5 * bi_ref[:, cols]
        log_a = decay[:, cols] * (1.0 + jnp.tanh(half_r))
        a = jnp.exp(log_a)
        mult = jnp.sqrt(-jnp.tanh(log_a) * (a * a + 1.0))
        a_s[:, cols] = a
        b_s[:, cols] = (mult * (0.5 * un)) * (1.0 + jnp.tanh(half_i))

    def step(q, h):
        for j in range(SCAN_UNROLL):
            t = q * SCAN_UNROLL + j
            t = (steps - 1 - t) if reverse else t
            row = pl.multiple_of(t * batch, batch)
            h = a_s[pl.ds(row, batch), :] * h + b_s[pl.ds(row, batch), :]
            if reverse:
                z_s[pl.ds(row, batch), :] = h
            else:
                o_ref[pl.ds(row, batch), :] = h
        return h
    h_s[...] = lax.fori_loop(0, steps // SCAN_UNROLL, step, h_s[...])
    if reverse:
        o_ref[...] = (gg_ref[...] * (hf_ref[...] + z_s[...])).astype(o_ref.dtype)


def _lru_scan_call(u, hf, gg, conv_w, conv_b, w_r, b_r, w_i, b_i, lam, batch, seq, ctx_len, chunk, reverse):
    n_rows, d_rnn = u.shape
    rows = chunk * batch
    n_lat_c, n_ctx_c = seq // chunk, ctx_len // chunk
    pre, post = CONV_LEFT * batch, (CONV_WIDTH - 1 - CONV_LEFT) * batch
    wb = d_rnn // N_LRU_BLOCKS

    def blk(s):
        ci = jnp.where(s < n_ctx_c, s, s - n_ctx_c)
        if reverse:
            ci = jnp.where(s < n_ctx_c, n_ctx_c - 1 - ci, n_lat_c - 1 - ci)
        return jnp.where(s < n_ctx_c, n_lat_c + ci, ci)

    last_post = n_rows // post - 1
    cur = pl.BlockSpec((rows, d_rnn), lambda s: (blk(s), 0))
    in_specs = [pl.BlockSpec((pre, d_rnn), lambda s: (jnp.maximum(blk(s) * (rows // pre) - 1, 0), 0)),
                cur,
                pl.BlockSpec((post, d_rnn), lambda s: (jnp.minimum((blk(s) + 1) * (rows // post), last_post), 0)),
                pl.BlockSpec((CONV_WIDTH, d_rnn), lambda s: (0, 0)),
                pl.BlockSpec((1, d_rnn), lambda s: (0, 0)),
                pl.BlockSpec((N_LRU_BLOCKS, wb, wb), lambda s: (0, 0, 0)),
                pl.BlockSpec((1, d_rnn), lambda s: (0, 0)),
                pl.BlockSpec((N_LRU_BLOCKS, wb, wb), lambda s: (0, 0, 0)),
                pl.BlockSpec((1, d_rnn), lambda s: (0, 0)),
                pl.BlockSpec((1, d_rnn), lambda s: (0, 0))]
    args = [u, u, u, conv_w, conv_b.reshape(1, d_rnn), w_r, b_r.reshape(1, d_rnn), w_i, b_i.reshape(1, d_rnn),
            lam.reshape(1, d_rnn)]
    scratch = [pltpu.VMEM((rows, d_rnn), f32), pltpu.VMEM((rows, d_rnn), f32), pltpu.VMEM((batch, d_rnn), f32)]
    if reverse:
        in_specs += [cur, cur]
        args += [hf, gg]
        scratch.append(pltpu.VMEM((rows, d_rnn), f32))
    return pl.pallas_call(
        functools.partial(_lru_scan_kernel, batch=batch, n_ctx_c=n_ctx_c, n_lat_c=n_lat_c, reverse=reverse),
        grid=(n_ctx_c + n_lat_c,), in_specs=in_specs, out_specs=cur,
        out_shape=jax.ShapeDtypeStruct((n_rows, d_rnn), bf16 if reverse else f32),
        scratch_shapes=scratch, compiler_params=_params("arbitrary"),
        name="lru_scan_rev" if reverse else "lru_scan_fwd",
    )(*args)


def _lru_out_kernel(*refs, batch):
    z_ref = refs[0]
    x_refs = refs[1:1 + batch]
    w_ref, g1_ref, o_ref, y_s = refs[1 + batch:]
    tt = x_refs[0].shape[0]
    y = jnp.dot(z_ref[...], w_ref[...], preferred_element_type=f32)
    n_planes = y.shape[1] // LANES
    for s in range(n_planes):
        y_s[s] = y[:, s * LANES:(s + 1) * LANES]
    for b in range(batch):
        y_b = jnp.concatenate([y_s[s, pl.ds(b, tt, stride=batch), :] for s in range(n_planes)], axis=-1)
        o_ref[b] = x_refs[b][...] + g1_ref[b] * y_b


def _lru_out_call(z, tokens, w_out_bf, mod, batch, seq, tt):
    d = tokens.shape[1]
    d_rnn = z.shape[1]
    n_t = seq // tt
    x_specs = [pl.BlockSpec((tt, d), lambda i, b=b: (b * n_t + i, 0)) for b in range(batch)]
    return pl.pallas_call(
        functools.partial(_lru_out_kernel, batch=batch), grid=(n_t,),
        in_specs=[pl.BlockSpec((tt * batch, d_rnn), lambda i: (i, 0))] + x_specs
                 + [pl.BlockSpec((d_rnn, d), lambda i: (0, 0)), pl.BlockSpec((batch, 1, d), lambda i: (0, 0, 2))],
        out_specs=pl.BlockSpec((batch, tt, d), lambda i: (0, i, 0)),
        out_shape=jax.ShapeDtypeStruct((batch, seq, d), f32),
        scratch_shapes=[pltpu.VMEM((d // LANES, tt * batch, LANES), f32)],
        compiler_params=_params("arbitrary"), name="lru_out",
    )(z, *([tokens] * batch), w_out_bf, mod)


def kernel(x, c, ctx, c_ctx, ada_w, ada_b, norm_mix, norm_ffn, pool_w, pool_scale, lru_w_in, lru_conv_w, lru_conv_b, lru_w_r, lru_b_r, lru_w_i, lru_b_i, lru_lam, lru_w_out, router_w, router_b, exp_w_gu, exp_b_gu, exp_w_down, exp_b_down, final_norm):
    batch, seq, d = x.shape
    ctx_len = ctx.shape[1]
    assert d == SUBLANES * LANES and seq % GRID_W == 0
    n_lat, n_ctx = batch * seq, batch * ctx_len
    ctx_row = batch

    mod_rows = -(-(batch + 1) // SUBLANES) * SUBLANES
    cc = jnp.zeros((mod_rows, d), f32).at[:batch].set(c).at[batch].set(c_ctx)
    mod_all = _ada_call(cc, ada_w, ada_b)
    mods = [mod_all[l].reshape(mod_rows, 1, 6 * d) for l in range(ada_w.shape[0])]

    lat = x.reshape(n_lat, d)
    cx = ctx.reshape(n_ctx, d)
    tile = _pick_tile(ROUTE_TILE, seq, n_ctx)
    gain = lambda g: g.reshape(1, d)

    lat = _pool_lat_call(lat, _rms_call(lat, tile), gain(norm_mix[0]), pool_w[0], gain(pool_scale[0]), mods[0],
                         batch, seq)
    cx = _pool_ctx_call(cx, gain(norm_mix[0]), pool_w[0], gain(pool_scale[0]), mods[0], ctx_row, batch, ctx_len)
    tokens = _moe_layer(0, lat, cx, gain(norm_ffn[0]), mods[0], seq, ctx_row, router_w, router_b,
                        exp_w_gu, exp_b_gu, exp_w_down, exp_b_down, gain(final_norm), False)

    assert batch == SUBLANES
    tt = _pick_tile(LRU_TILE, seq, ctx_len)
    chunk = _pick_tile(SCAN_CHUNK, seq, ctx_len)
    gg, u = _lru_in_call(tokens, gain(norm_mix[1]), mods[1], ctx_row, lru_w_in[0].astype(bf16), batch, seq,
                         ctx_len, tt)
    scan = functools.partial(_lru_scan_call, conv_w=lru_conv_w[0], conv_b=lru_conv_b[0], batch=batch, seq=seq,
                             ctx_len=ctx_len, chunk=chunk)
    hf = scan(u, None, None, w_r=lru_w_r[0, 0], b_r=lru_b_r[0, 0], w_i=lru_w_i[0, 0], b_i=lru_b_i[0, 0],
              lam=lru_lam[0, 0], reverse=False)
    z = scan(u, hf, gg, w_r=lru_w_r[0, 1], b_r=lru_b_r[0, 1], w_i=lru_w_i[0, 1], b_i=lru_b_i[0, 1],
             lam=lru_lam[0, 1], reverse=True)
    lat = _lru_out_call(z, tokens, lru_w_out[0].astype(bf16), mods[1], batch, seq, tt).reshape(n_lat, d)
    out = _moe_layer(1, lat, None, gain(norm_ffn[1]), mods[1], seq, ctx_row, router_w, router_b,
                     exp_w_gu, exp_b_gu, exp_w_down, exp_b_down, gain(final_norm), True)
    return out.reshape(batch, seq, d)
```

```python
import functools

import numpy as np
import jax
import jax.numpy as jnp
from jax import lax
from jax.experimental import pallas as pl
from jax.experimental.pallas import tpu as pltpu

f32 = jnp.float32
bf16 = jnp.bfloat16
i32 = jnp.int32
i16 = jnp.int16
HIGHEST = lax.Precision.HIGHEST

LANES = 128
SUBLANES = 8
VMEM_LIMIT = 56 * 1024 * 1024

GRID_W = 64
GRID_SHIFT = 6
POOL_WINDOWS = (2, 4, 8, 16)
N_POOL_GROUPS = 4
POOL_BLOCK = 256
N_LRU_BLOCKS = 4
CONV_LEFT = 2
CONV_WIDTH = 4
LRU_C = 8.0
N_EXPERTS = 32
TOP_K = 4
SWIGLU_LIMIT = 7.0
SWIGLU_ALPHA = 1.702
NORM_EPS = 1e-6
GELU_C = 0.7978845608028654

EXPERT_ROWS = 1024
MLP_ROWS = 256
BUILD_ROWS = 128
ROUTE_TILE = 512
LRU_TILE = 128
SCAN_CHUNK = 64
SCAN_UNROLL = 8


def _params(*sem):
    return pltpu.CompilerParams(dimension_semantics=sem, vmem_limit_bytes=VMEM_LIMIT)


def _pick_tile(pref, *sizes):
    t = pref
    while any(s % t for s in sizes):
        t //= 2
    assert t >= SUBLANES
    return t


def _norm_mod(x, gain, scale, shift):
    ms = jnp.mean(x * x, axis=-1, keepdims=True)
    y = x * lax.rsqrt(ms + NORM_EPS)
    return (y * gain) * (1.0 + scale) + shift


def _row_map(n_lat_tiles, tiles_per_batch, ctx_row):
    def row(i):
        return jnp.where(i < n_lat_tiles, i // tiles_per_batch, ctx_row)
    return row


def _mod_spec(row, chunk, width):
    return pl.BlockSpec((None, 1, width), lambda i, *_: (row(i), 0, chunk))


def _token_specs(tile, d, n_lat_tiles):
    return [pl.BlockSpec((tile, d), lambda i, *_: (jnp.minimum(i, n_lat_tiles - 1), 0)),
            pl.BlockSpec((tile, d), lambda i, *_: (jnp.maximum(i - n_lat_tiles, 0), 0))]


def _token_tile(x_ref, xc_ref, n_lat_tiles):
    return jnp.where(pl.program_id(0) >= n_lat_tiles, xc_ref[...], x_ref[...])


def _ada_kernel(c_ref, w_ref, b_ref, o_ref):
    c = c_ref[...]
    s = c * jax.nn.sigmoid(c)
    o_ref[...] = jnp.dot(s, w_ref[...], preferred_element_type=f32, precision=HIGHEST) + b_ref[...]


def _ada_call(cc, ada_w, ada_b):
    depth, d, d6 = ada_w.shape
    rows = cc.shape[0]
    bn = 1536
    return pl.pallas_call(
        _ada_kernel, grid=(depth, d6 // bn),
        in_specs=[pl.BlockSpec((rows, d), lambda l, j: (0, 0)),
                  pl.BlockSpec((None, d, bn), lambda l, j: (l, 0, j)),
                  pl.BlockSpec((None, 1, bn), lambda l, j: (l, 0, j))],
        out_specs=pl.BlockSpec((None, rows, bn), lambda l, j: (l, 0, j)),
        out_shape=jax.ShapeDtypeStruct((depth, rows, d6), f32),
        compiler_params=_params("arbitrary", "arbitrary"), name="ada_mod",
    )(cc, ada_w, ada_b.reshape(depth, 1, d6))


def _rms_kernel(x_ref, o_ref):
    x = x_ref[...]
    o_ref[...] = lax.rsqrt(jnp.mean(x * x, axis=-1, keepdims=True) + NORM_EPS)


def _rms_call(tokens, tile):
    n, d = tokens.shape
    return pl.pallas_call(
        _rms_kernel, grid=(n // tile,),
        in_specs=[pl.BlockSpec((tile, d), lambda i: (i, 0))],
        out_specs=pl.BlockSpec((tile, 1), lambda i: (i, 0)),
        out_shape=jax.ShapeDtypeStruct((n, 1), f32),
        compiler_params=_params("arbitrary"), name="row_rms",
    )(tokens)


def _shift(x, j, unit, pos, limit):
    n = x.shape[0]
    rolled = pltpu.roll(x, (j * unit) % n, 0)
    ok = (pos >= j) if j > 0 else (pos < limit + j)
    return jnp.where(ok, rolled, 0.0)


def _window_sum(x, w, unit, pos, limit):
    m = w // 2
    trail, lead, s = x, x, 1
    while s < m:
        trail = trail + _shift(trail, s, unit, pos, limit)
        lead = lead + _shift(lead, -s, unit, pos, limit)
        s *= 2
    return _shift(trail, 1, unit, pos, limit) + lead


def _window_count(w, pos, limit):
    m = w // 2
    return jnp.minimum(pos + m, limit) - jnp.maximum(pos - m, 0)


def _col_window_sum(h, w):
    blk = POOL_BLOCK
    i = lax.broadcasted_iota(i32, (blk, blk), 0)
    j = lax.broadcasted_iota(i32, (blk, blk), 1)
    off = (j & (GRID_W - 1)) - (i & (GRID_W - 1))
    same_row = (i >> GRID_SHIFT) == (j >> GRID_SHIFT)
    band = jnp.where(jnp.logical_and(same_row, jnp.logical_and(off >= -(w // 2), off < w // 2)), 1.0, 0.0)
    band = band.astype(bf16)
    out = []
    for r0 in range(0, h.shape[0], blk):
        hb = h[r0:r0 + blk]
        hi = hb.astype(bf16)
        lo = (hb - hi.astype(f32)).astype(bf16)
        out.append(jnp.dot(band, hi, preferred_element_type=f32) + jnp.dot(band, lo, preferred_element_type=f32))
    return jnp.concatenate(out, axis=0)


def _pool_grid(h, w):
    t = lax.broadcasted_iota(i32, h.shape, 0)
    col, row = t & (GRID_W - 1), t >> GRID_SHIFT
    rows = h.shape[0] // GRID_W
    total = _window_sum(_col_window_sum(h, w), w, GRID_W, row, rows)
    count = (_window_count(w, row, rows) * _window_count(w, col, GRID_W)).astype(f32)
    return total / count


def _pool_seq(h, w):
    t = lax.broadcasted_iota(i32, h.shape, 0)
    n = h.shape[0]
    return _window_sum(h, w, 1, t, n) / _window_count(w, t, n).astype(f32)


def _pool_lat_kernel(x_ref, r_ref, gain_ref, sc_ref, sh_ref, w_ref, ps_ref, g1_ref, o_ref):
    grp = pl.program_id(1)
    for gi, w in enumerate(POOL_WINDOWS):
        @pl.when(grp == gi)
        def _(w=w):
            x = x_ref[...]
            h = ((x * r_ref[...]) * gain_ref[...]) * (1.0 + sc_ref[...]) + sh_ref[...]
            d = _pool_grid(h, w) - h
            y = jnp.dot(d.astype(bf16), w_ref[...].astype(bf16), preferred_element_type=f32)
            o_ref[...] = x + g1_ref[...] * (y * ps_ref[...])


def _pool_lat_call(x, rms, gain, pool_w, pool_scale, mod, batch, seq):
    n, d = x.shape
    cg = d // N_POOL_GROUPS
    mod_chunk = lambda c: pl.BlockSpec((None, 1, cg), lambda b, g: (b, 0, c * N_POOL_GROUPS + g))
    return pl.pallas_call(
        _pool_lat_kernel, grid=(batch, N_POOL_GROUPS),
        in_specs=[pl.BlockSpec((seq, cg), lambda b, g: (b, g)),
                  pl.BlockSpec((seq, 1), lambda b, g: (b, 0)),
                  pl.BlockSpec((1, cg), lambda b, g: (0, g)),
                  mod_chunk(1), mod_chunk(0),
                  pl.BlockSpec((None, cg, cg), lambda b, g: (g, 0, 0)),
                  pl.BlockSpec((1, cg), lambda b, g: (0, g)),
                  mod_chunk(2)],
        out_specs=pl.BlockSpec((seq, cg), lambda b, g: (b, g)),
        out_shape=jax.ShapeDtypeStruct((n, d), f32),
        compiler_params=_params("arbitrary", "arbitrary"), name="pool_lat",
    )(x, rms, gain, mod, mod, pool_w, pool_scale, mod)


def _pool_ctx_kernel(x_ref, gain_ref, sc_ref, sh_ref, w_ref, ps_ref, g1_ref, o_ref):
    x = x_ref[...]
    h = _norm_mod(x, gain_ref[...], sc_ref[...], sh_ref[...])
    cg = x.shape[1] // N_POOL_GROUPS
    for gi, w in enumerate(POOL_WINDOWS):
        sl = slice(gi * cg, (gi + 1) * cg)
        hg = h[:, sl]
        d = _pool_seq(hg, w) - hg
        y = jnp.dot(d.astype(bf16), w_ref[gi].astype(bf16), preferred_element_type=f32)
        o_ref[:, sl] = x[:, sl] + g1_ref[:, sl] * (y * ps_ref[:, sl])


def _pool_ctx_call(cx, gain, pool_w, pool_scale, mod, ctx_row, batch, ctx_len):
    n, d = cx.shape
    cg = d // N_POOL_GROUPS
    row = lambda i: ctx_row
    return pl.pallas_call(
        _pool_ctx_kernel, grid=(batch,),
        in_specs=[pl.BlockSpec((ctx_len, d), lambda b: (b, 0)),
                  pl.BlockSpec((1, d), lambda b: (0, 0)),
                  _mod_spec(row, 1, d), _mod_spec(row, 0, d),
                  pl.BlockSpec((N_POOL_GROUPS, cg, cg), lambda b: (0, 0, 0)),
                  pl.BlockSpec((1, d), lambda b: (0, 0)),
                  _mod_spec(row, 2, d)],
        out_specs=pl.BlockSpec((ctx_len, d), lambda b: (b, 0)),
        out_shape=jax.ShapeDtypeStruct((n, d), f32),
        compiler_params=_params("arbitrary"), name="pool_ctx",
    )(cx, gain, mod, mod, pool_w, pool_scale, mod)


def _route_kernel(x_ref, xc_ref, gain_ref, sc_ref, sh_ref, wr_ref, rb_ref, loc_ref, gate_ref, cnt_ref, *, n_lat_tiles):
    tt = x_ref.shape[0]
    h = _norm_mod(_token_tile(x_ref, xc_ref, n_lat_tiles), gain_ref[...], sc_ref[...], sh_ref[...])

    def split(v):
        hi = v.astype(bf16)
        return hi, (v - hi.astype(f32)).astype(bf16)
    contract_last = (((1,), (1,)), ((), ()))
    w_hi, w_lo = split(wr_ref[...])
    h_hi, h_lo = split(h)
    by_hi = lax.dot_general(jnp.concatenate([w_hi, w_lo], axis=0), h_hi, contract_last, preferred_element_type=f32)
    by_lo = lax.dot_general(w_hi, h_lo, contract_last, preferred_element_type=f32)
    logits = by_hi[:N_EXPERTS] + (by_hi[N_EXPERTS:] + by_lo) + rb_ref[...]
    e_iota = lax.broadcasted_iota(i32, logits.shape, 0)
    vals, onehots = [], []
    work = logits
    for k in range(TOP_K):
        m = jnp.max(work, axis=0, keepdims=True)
        idx = jnp.min(jnp.where(work == m, e_iota, N_EXPERTS), axis=0, keepdims=True)
        hit = e_iota == idx
        vals.append(m)
        onehots.append(jnp.where(hit, 1.0, 0.0))
        work = jnp.where(hit, -jnp.inf, work)

    ex = [jnp.exp(v - vals[0]) for v in vals]
    den = ex[0] + ex[1] + ex[2] + ex[3]
    for k in range(TOP_K):
        gate_ref[k:k + 1, :] = ex[k] / den

    tri = jnp.where(lax.broadcasted_iota(i32, (tt, tt), 0) <= lax.broadcasted_iota(i32, (tt, tt), 1), 1.0, 0.0)
    planes = jnp.concatenate(onehots, axis=0).astype(bf16)
    incl = jnp.dot(planes, tri.astype(bf16), preferred_element_type=f32)
    totals = [incl[k * N_EXPERTS:(k + 1) * N_EXPERTS, tt - 1:tt] for k in range(TOP_K)]
    count = jnp.broadcast_to(totals[0] + totals[1] + totals[2] + totals[3], (N_EXPERTS, LANES))
    padded = (((count.astype(i32) + (SUBLANES - 1)) // SUBLANES) * SUBLANES).astype(f32)
    below = jnp.where(lax.broadcasted_iota(i32, (N_EXPERTS, N_EXPERTS), 1)
                      < lax.broadcasted_iota(i32, (N_EXPERTS, N_EXPERTS), 0), 1.0, 0.0)
    offset = jnp.dot(below.astype(bf16), padded.astype(bf16), preferred_element_type=f32)[:, 0:1]
    for k in range(TOP_K):
        inc_k = incl[k * N_EXPERTS:(k + 1) * N_EXPERTS]
        before = inc_k - onehots[k] + offset
        loc_ref[k:k + 1, :] = jnp.sum(onehots[k] * before, axis=0, keepdims=True).astype(i32)
        offset = offset + totals[k]
    cnt_ref[...] = count


def _route_call(lat, cx, n, n_lat_tiles, gain, mod, row, w_router_t, b_router, tile):
    d = lat.shape[1]
    e = w_router_t.shape[0]
    return pl.pallas_call(
        functools.partial(_route_kernel, n_lat_tiles=n_lat_tiles), grid=(n // tile,),
        in_specs=_token_specs(tile, d, n_lat_tiles) + [
                  pl.BlockSpec((1, d), lambda i: (0, 0)),
                  _mod_spec(row, 4, d), _mod_spec(row, 3, d),
                  pl.BlockSpec((e, d), lambda i: (0, 0)),
                  pl.BlockSpec((e, 1), lambda i: (0, 0))],
        out_specs=[pl.BlockSpec((TOP_K, tile), lambda i: (0, i)),
                   pl.BlockSpec((TOP_K, tile), lambda i: (0, i)),
                   pl.BlockSpec((None, e, LANES), lambda i: (i, 0, 0))],
        out_shape=[jax.ShapeDtypeStruct((TOP_K, n), i32),
                   jax.ShapeDtypeStruct((TOP_K, n), f32),
                   jax.ShapeDtypeStruct((n // tile, e, LANES), f32)],
        compiler_params=_params("arbitrary"), name="moe_route",
    )(lat, cx, gain, mod, mod, w_router_t, b_router.reshape(e, 1))


def _copy_run(src_at, dst_at, n_pieces, n_bits, sem, wait=False):
    for k in reversed(range(n_bits)):
        size = SUBLANES << k
        done = (n_pieces >> (k + 1)) << (k + 1)

        @pl.when(((n_pieces >> k) & 1) == 1)
        def _(size=size, done=done):
            off = 0 if wait else pl.multiple_of(done * SUBLANES, SUBLANES)
            copy = pltpu.make_async_copy(src_at(off, size), dst_at(off, size), sem)
            copy.wait() if wait else copy.start()


def _dispatch_kernel(go_ref, lb_ref, p8_ref, np_ref, ts_ref, tn_ref, nu_ref,
                     x_ref, xc_ref, gain_ref, sc_ref, sh_ref, loc_ref, xs_hbm, xloc, zbuf, sel_s, sem,
                     *, n_tiles, n_blocks, n_lat_tiles):
    i = pl.program_id(0)
    tt = x_ref.shape[0]
    rl = xloc.shape[1]
    slot = i % 2

    def build(loc, dst):
        rows = [loc[k:k + 1, :].astype(i16) for k in range(TOP_K)]
        one, zero = jnp.ones((), bf16), jnp.zeros((), bf16)
        for r0 in range(0, rl, BUILD_ROWS):
            r_iota = lax.broadcasted_iota(i16, (BUILD_ROWS, tt), 0) + r0
            sel = None
            for k in range(TOP_K):
                m = jnp.where(r_iota == rows[k], one, zero)
                sel = m if sel is None else sel + m
            dst[r0:r0 + BUILD_ROWS, :] = sel

    fill_sem = sem.at[2]
    tail_bits = (EXPERT_ROWS // SUBLANES - 1).bit_length()
    zero_at = lambda off, size: zbuf.at[pl.ds(0, size), :]

    @pl.when(i == 0)
    def _():
        zbuf[...] = jnp.zeros(zbuf.shape, f32)

        def tail(e, carry):
            t0 = pl.multiple_of(ts_ref[e], SUBLANES)
            _copy_run(zero_at, lambda off, size: xs_hbm.at[pl.ds(t0 + off, size), :], tn_ref[e], tail_bits,
                      fill_sem)
            return carry
        lax.fori_loop(0, N_EXPERTS, tail, 0)

        def spare(b, carry):
            t = pl.multiple_of(b * EXPERT_ROWS, EXPERT_ROWS)
            pltpu.make_async_copy(zbuf, xs_hbm.at[pl.ds(t, EXPERT_ROWS), :], fill_sem).start()
            return carry
        lax.fori_loop(nu_ref[0], n_blocks, spare, 0)

    seg_bits = (tt // SUBLANES).bit_length()
    tile_bits = (rl // SUBLANES).bit_length()
    buf_at = lambda off, size: xloc.at[0, pl.ds(off, size), :]
    hbm_at = lambda off, size: xs_hbm.at[pl.ds(off, size), :]

    def segment_copies(tile, e, buf, live):
        idx = tile * N_EXPERTS + e
        src0 = pl.multiple_of(lb_ref[idx], SUBLANES)
        dst0 = pl.multiple_of(go_ref[idx], SUBLANES)
        _copy_run(lambda off, size: xloc.at[buf, pl.ds(src0 + off, size), :],
                  lambda off, size: xs_hbm.at[pl.ds(dst0 + off, size), :],
                  jnp.where(live, p8_ref[idx] // SUBLANES, 0), seg_bits, sem.at[buf])

    @pl.when(i >= 2)
    def _():
        _copy_run(buf_at, hbm_at, np_ref[jnp.maximum(i - 2, 0)], tile_bits, sem.at[slot], wait=True)

    for parity in range(2):
        @pl.when(slot == parity)
        def _(parity=parity):
            for e in range(N_EXPERTS):
                segment_copies(jnp.maximum(i - 1, 0), e, 1 - parity, i >= 1)
            build(loc_ref, sel_s)
            h = _norm_mod(_token_tile(x_ref, xc_ref, n_lat_tiles), gain_ref[...], sc_ref[...], sh_ref[...])
            xloc[parity] = jnp.dot(sel_s[...], h.astype(bf16), preferred_element_type=f32)

    @pl.when(i == n_tiles - 1)
    def _():
        def per_expert(e, carry):
            segment_copies(i, e, slot, True)
            return carry
        lax.fori_loop(0, N_EXPERTS, per_expert, 0)

        @pl.when(i >= 1)
        def _():
            _copy_run(buf_at, hbm_at, np_ref[jnp.maximum(i - 1, 0)], tile_bits, sem.at[1 - slot], wait=True)
        _copy_run(buf_at, hbm_at, np_ref[i], tile_bits, sem.at[slot], wait=True)

        def tail_wait(e, carry):
            _copy_run(zero_at, hbm_at, tn_ref[e], tail_bits, fill_sem, wait=True)
            return carry
        lax.fori_loop(0, N_EXPERTS, tail_wait, 0)

        def spare_wait(b, carry):
            pltpu.make_async_copy(zbuf, xs_hbm.at[pl.ds(0, EXPERT_ROWS), :], fill_sem).wait()
            return carry
        lax.fori_loop(nu_ref[0], n_blocks, spare_wait, 0)


def _dispatch_call(tables, lat, cx, n, n_lat_tiles, gain, mod, row, loc, n_rows, tile):
    d = lat.shape[1]
    n_tiles = n // tile
    rl = TOP_K * tile + N_EXPERTS * SUBLANES
    spec = lambda shape, fn: pl.BlockSpec(shape, lambda i, *_: fn(i))
    grid_spec = pltpu.PrefetchScalarGridSpec(
        num_scalar_prefetch=7, grid=(n_tiles,),
        in_specs=_token_specs(tile, d, n_lat_tiles) + [
                  spec((1, d), lambda i: (0, 0)),
                  _mod_spec(row, 4, d), _mod_spec(row, 3, d),
                  spec((TOP_K, tile), lambda i: (0, i))],
        out_specs=pl.BlockSpec(memory_space=pl.ANY),
        scratch_shapes=[pltpu.VMEM((2, rl, d), f32), pltpu.VMEM((EXPERT_ROWS, d), f32),
                        pltpu.VMEM((rl, tile), bf16), pltpu.SemaphoreType.DMA((3,))])
    return pl.pallas_call(
        functools.partial(_dispatch_kernel, n_tiles=n_tiles, n_blocks=n_rows // EXPERT_ROWS, n_lat_tiles=n_lat_tiles),
        grid_spec=grid_spec,
        out_shape=jax.ShapeDtypeStruct((n_rows, d), f32),
        compiler_params=_params("arbitrary"), name="moe_dispatch",
    )(*tables, lat, cx, gain, mod, mod, loc)


def _gmm_kernel(be_ref, bv_ref, lu_ref, x_ref, wgu_ref, bgu_ref, wd_ref, bd_ref, o_ref, wgu_bf, wd_bf):
    j = pl.program_id(0)
    d_ff = wd_ref.shape[0]
    rows = x_ref.shape[0]
    valid = bv_ref[j]

    @pl.when(valid > 0)
    def _():
        new_expert = jnp.logical_or(j == 0, be_ref[j] != be_ref[jnp.maximum(j - 1, 0)])

        @pl.when(new_expert)
        def _():
            wgu_bf[...] = wgu_ref[...].astype(bf16)
            wd_bf[...] = wd_ref[...].astype(bf16)

    def mlp(r0):
        rs = slice(r0, r0 + MLP_ROWS)
        x = x_ref[rs, :].astype(bf16)
        gu = jnp.dot(x, wgu_bf[...], preferred_element_type=f32) + bgu_ref[...]
        g = jnp.minimum(gu[:, :d_ff], SWIGLU_LIMIT)
        u = jnp.clip(gu[:, d_ff:], -SWIGLU_LIMIT, SWIGLU_LIMIT)
        act = (u + 1.0) * g * jax.nn.sigmoid(SWIGLU_ALPHA * g)
        o_ref[rs, :] = jnp.dot(act.astype(bf16), wd_bf[...], preferred_element_type=f32) + bd_ref[...]

    n_groups = rows // MLP_ROWS
    for full in range(n_groups + 1):
        lo, hi = (full - 1) * MLP_ROWS, full * MLP_ROWS
        cond = (valid == 0) if full == 0 else jnp.logical_and(valid > lo, valid <= hi)
        if full == n_groups:
            cond = valid > lo

        @pl.when(cond)
        def _(full=full):
            for gidx in range(full):
                mlp(gidx * MLP_ROWS)
            if full < n_groups:
                o_ref[full * MLP_ROWS:, :] = jnp.zeros((rows - full * MLP_ROWS, o_ref.shape[1]), f32)


def _gmm_call(layer, blk_e, blk_valid, last_used, xs, w_gu, b_gu, w_down, b_down):
    depth, e, d, f2 = w_gu.shape
    d_ff = w_down.shape[2]
    rows = EXPERT_ROWS
    n_blocks = xs.shape[0] // rows
    grid_spec = pltpu.PrefetchScalarGridSpec(
        num_scalar_prefetch=3, grid=(n_blocks,),
        in_specs=[pl.BlockSpec((rows, d), lambda j, be, bv, lu: (jnp.minimum(j, lu[0]), 0)),
                  pl.BlockSpec((None, None, d, f2), lambda j, be, bv, lu: (layer, be[j], 0, 0)),
                  pl.BlockSpec((None, None, 1, f2), lambda j, be, bv, lu: (layer, be[j], 0, 0)),
                  pl.BlockSpec((None, None, d_ff, d), lambda j, be, bv, lu: (layer, be[j], 0, 0)),
                  pl.BlockSpec((None, None, 1, d), lambda j, be, bv, lu: (layer, be[j], 0, 0))],
        out_specs=pl.BlockSpec((rows, d), lambda j, be, bv, lu: (j, 0)),
        scratch_shapes=[pltpu.VMEM((d, f2), bf16), pltpu.VMEM((d_ff, d), bf16)])
    return pl.pallas_call(
        _gmm_kernel, grid_spec=grid_spec,
        out_shape=jax.ShapeDtypeStruct((n_blocks * rows, d), f32),
        compiler_params=_params("arbitrary"), name="moe_experts",
    )(blk_e, blk_valid, last_used, xs, w_gu, b_gu.reshape(depth, e, 1, f2), w_down, b_down.reshape(depth, e, 1, d))


def _combine_kernel(go_ref, lb_ref, p8_ref, np_ref,
                    ys_hbm, loc_ref, gate_ref, x_ref, xc_ref, g2_ref, fn_ref, o_ref, yloc, w_s, sem,
                    *, n_tiles, n_lat_tiles, final):
    i = pl.program_id(0)
    tt = x_ref.shape[0]
    rl = yloc.shape[1]
    slot = i % 2

    def build(loc, gate, dst):
        cols = [(loc[:, k:k + 1].astype(i16), gate[:, k:k + 1].astype(bf16)) for k in range(TOP_K)]
        zero = jnp.zeros((), bf16)
        for c0 in range(0, rl, BUILD_ROWS):
            r_iota = lax.broadcasted_iota(i16, (tt, BUILD_ROWS), 1) + c0
            weights = None
            for at, g in cols:
                m = jnp.where(r_iota == at, g, zero)
                weights = m if weights is None else weights + m
            dst[:, c0:c0 + BUILD_ROWS] = weights

    seg_bits = (tt // SUBLANES).bit_length()
    tile_bits = (rl // SUBLANES).bit_length()

    def segment_fetch(tile, e, buf, live):
        idx = tile * N_EXPERTS + e
        src0 = pl.multiple_of(go_ref[idx], SUBLANES)
        dst0 = pl.multiple_of(lb_ref[idx], SUBLANES)
        _copy_run(lambda off, size: ys_hbm.at[pl.ds(src0 + off, size), :],
                  lambda off, size: yloc.at[buf, pl.ds(dst0 + off, size), :],
                  jnp.where(live, p8_ref[idx] // SUBLANES, 0), seg_bits, sem.at[buf])

    @pl.when(i == 0)
    def _():
        yloc[...] = jnp.zeros(yloc.shape, f32)

        def per_expert(e, carry):
            segment_fetch(0, e, 0, True)
            return carry
        lax.fori_loop(0, N_EXPERTS, per_expert, 0)

    _copy_run(lambda off, size: ys_hbm.at[pl.ds(off, size), :], lambda off, size: yloc.at[0, pl.ds(off, size), :],
              np_ref[i], tile_bits, sem.at[slot], wait=True)

    for parity in range(2):
        @pl.when(slot == parity)
        def _(parity=parity):
            for e in range(N_EXPERTS):
                segment_fetch(jnp.minimum(i + 1, n_tiles - 1), e, 1 - parity, i + 1 < n_tiles)
            build(loc_ref, gate_ref, w_s)
            y = jnp.dot(w_s[...], yloc[parity].astype(bf16), preferred_element_type=f32)
            out = _token_tile(x_ref, xc_ref, n_lat_tiles) + g2_ref[...] * y
            if final:
                ms = jnp.mean(out * out, axis=-1, keepdims=True)
                out = (out * lax.rsqrt(ms + NORM_EPS)) * fn_ref[...]
            o_ref[...] = out


def _combine_call(tables, ys, loc_t, gates_t, lat, cx, n, n_lat_tiles, mod, row, final_gain, tile, final):
    d = lat.shape[1]
    n_tiles = n // tile
    rl = TOP_K * tile + N_EXPERTS * SUBLANES
    spec = lambda shape, fn: pl.BlockSpec(shape, lambda i, *_: fn(i))
    grid_spec = pltpu.PrefetchScalarGridSpec(
        num_scalar_prefetch=4, grid=(n_tiles,),
        in_specs=[pl.BlockSpec(memory_space=pl.ANY),
                  spec((tile, TOP_K), lambda i: (i, 0)),
                  spec((tile, TOP_K), lambda i: (i, 0))] + _token_specs(tile, d, n_lat_tiles) + [
                  _mod_spec(row, 5, d),
                  spec((1, d), lambda i: (0, 0))],
        out_specs=spec((tile, d), lambda i: (i, 0)),
        scratch_shapes=[pltpu.VMEM((2, rl, d), f32), pltpu.VMEM((tile, rl), bf16),
                        pltpu.SemaphoreType.DMA((2,))])
    return pl.pallas_call(
        functools.partial(_combine_kernel, n_tiles=n_tiles, n_lat_tiles=n_lat_tiles, final=final), grid_spec=grid_spec,
        out_shape=jax.ShapeDtypeStruct((n, d), f32),
        compiler_params=_params("arbitrary"), name="moe_combine",
    )(*tables, ys, loc_t, gates_t, lat, cx, mod, final_gain)


def _moe_layer(layer, lat, cx, gain, mod, seq, ctx_row, w_router, b_router, w_gu, b_gu, w_down, b_down,
               final_gain, final):
    n_lat, d = lat.shape
    n = n_lat + (0 if cx is None else cx.shape[0])
    tile = _pick_tile(ROUTE_TILE, n_lat, n - n_lat if n > n_lat else n_lat, seq)
    n_tiles, n_lat_tiles = n // tile, n_lat // tile
    cx = lat if cx is None else cx
    row = _row_map(n_lat_tiles, seq // tile, ctx_row)

    loc, gate, cnt = _route_call(lat, cx, n, n_lat_tiles, gain, mod, row, w_router[layer].T, b_router[layer], tile)

    rows = EXPERT_ROWS
    count = cnt[:, :, 0].astype(i32)
    seg = (count + SUBLANES - 1) // SUBLANES * SUBLANES
    local_base = jnp.cumsum(seg, axis=1) - seg
    pieces = jnp.sum(seg, axis=1) // SUBLANES
    expert_rows = jnp.sum(seg, axis=0)
    region = (expert_rows + rows - 1) // rows * rows
    region_end = jnp.cumsum(region)
    region_start = region_end - region
    global_off = region_start[None, :] + jnp.cumsum(seg, axis=0) - seg
    n_blocks = (n * TOP_K + n_tiles * N_EXPERTS * (SUBLANES - 1) + N_EXPERTS * (rows - 1) + rows - 1) // rows
    blk_start = jnp.arange(n_blocks, dtype=i32) * rows
    experts = jnp.arange(N_EXPERTS, dtype=i32)
    blk_e = jnp.minimum(jnp.sum(region_end[None, :] <= blk_start[:, None], axis=1), N_EXPERTS - 1).astype(i32)
    content_end = region_start + expert_rows
    blk_end = jnp.sum(jnp.where(blk_e[:, None] == experts[None, :], content_end[None, :], 0), axis=1)
    blk_valid = jnp.clip(blk_end - blk_start, 0, rows).astype(i32)
    n_used = (region_end[-1:] // rows).astype(i32)
    last_used = jnp.maximum(n_used - 1, 0)
    tail_n = ((region - expert_rows) // SUBLANES).astype(i32)
    flat = lambda a: a.reshape(-1).astype(i32)
    tables = (flat(global_off), flat(local_base), flat(seg), pieces.astype(i32))

    xs = _dispatch_call(tables + (content_end.astype(i32), tail_n, n_used), lat, cx, n, n_lat_tiles, gain, mod, row,
                        loc, n_blocks * rows, tile)
    ys = _gmm_call(layer, blk_e, blk_valid, last_used, xs, w_gu, b_gu, w_down, b_down)
    return _combine_call(tables, ys, loc.T, gate.T, lat, cx, n, n_lat_tiles, mod, row, final_gain, tile, final)


def _time_tile_blocks(n_lat_t, n_ctx_t, n_lat_blocks, b):
    def index(i):
        lat_blk = b * n_lat_t + jnp.minimum(i, n_lat_t - 1)
        ctx_blk = n_lat_blocks + b * n_ctx_t + jnp.maximum(i - n_lat_t, 0)
        return (jnp.where(i < n_lat_t, lat_blk, ctx_blk), 0)
    return index


def _lru_in_kernel(*refs, batch, n_lat_t):
    x_refs = refs[:batch]
    gain_ref, sc_ref, sh_ref, scc_ref, shc_ref, w_ref, gg_ref, u_ref, tm_s = refs[batch:]
    tt = x_refs[0].shape[0]
    d_rnn = gg_ref.shape[1]
    is_ctx = pl.program_id(0) >= n_lat_t
    hs = []
    for b in range(batch):
        scale = jnp.where(is_ctx, scc_ref[...], sc_ref[b])
        shift = jnp.where(is_ctx, shc_ref[...], sh_ref[b])
        hs.append(_norm_mod(x_refs[b][...], gain_ref[...], scale, shift).astype(bf16))
    gu = jnp.dot(jnp.concatenate(hs, axis=0), w_ref[...], preferred_element_type=f32)
    gate = gu[:, :d_rnn]
    gelu = 0.5 * gate * (1.0 + jnp.tanh(GELU_C * (gate + 0.044715 * (gate * gate * gate))))
    for out_ref, val in ((gg_ref, gelu), (u_ref, gu[:, d_rnn:])):
        for b in range(batch):
            for s in range(d_rnn // LANES):
                tm_s[s, pl.ds(b, tt, stride=batch), :] = val[b * tt:(b + 1) * tt, s * LANES:(s + 1) * LANES]
        for s in range(d_rnn // LANES):
            out_ref[:, s * LANES:(s + 1) * LANES] = tm_s[s]


def _lru_in_call(tokens, gain, mod, ctx_row, w_in_bf, batch, seq, ctx_len, tt):
    n, d = tokens.shape
    d2 = w_in_bf.shape[1]
    d_rnn = d2 // 2
    n_lat_t, n_ctx_t = seq // tt, ctx_len // tt
    n_lat_blocks = batch * n_lat_t
    x_specs = [pl.BlockSpec((tt, d), _time_tile_blocks(n_lat_t, n_ctx_t, n_lat_blocks, b)) for b in range(batch)]
    mod_lat = lambda chunk: pl.BlockSpec((batch, 1, d), lambda i: (0, 0, chunk))
    mod_ctx = lambda chunk: pl.BlockSpec((None, 1, d), lambda i: (ctx_row, 0, chunk))
    out_spec = pl.BlockSpec((tt * batch, d_rnn), lambda i: (i, 0))
    out_shape = jax.ShapeDtypeStruct(((seq + ctx_len) * batch, d_rnn), f32)
    return pl.pallas_call(
        functools.partial(_lru_in_kernel, batch=batch, n_lat_t=n_lat_t), grid=(n_lat_t + n_ctx_t,),
        in_specs=x_specs + [pl.BlockSpec((1, d), lambda i: (0, 0)), mod_lat(1), mod_lat(0), mod_ctx(1), mod_ctx(0),
                            pl.BlockSpec((d, d2), lambda i: (0, 0))],
        out_specs=[out_spec, out_spec], out_shape=[out_shape, out_shape],
        scratch_shapes=[pltpu.VMEM((d_rnn // LANES, tt * batch, LANES), f32)],
        compiler_params=_params("arbitrary"), name="lru_in",
    )(*([tokens] * batch), gain, mod, mod, mod, mod, w_in_bf)


def _lru_scan_kernel(*refs, batch, n_ctx_c, n_lat_c, reverse):
    if reverse:
        (prev_ref, cur_ref, next_ref, cw_ref, cb_ref, wr_ref, br_ref, wi_ref, bi_ref, lam_ref, hf_ref, gg_ref,
         o_ref, a_s, b_s, h_s, z_s) = refs
    else:
        (prev_ref, cur_ref, next_ref, cw_ref, cb_ref, wr_ref, br_ref, wi_ref, bi_ref, lam_ref,
         o_ref, a_s, b_s, h_s) = refs
    s = pl.program_id(0)
    rows, d_rnn = cur_ref.shape
    steps = rows // batch
    wb = d_rnn // N_LRU_BLOCKS

    @pl.when(s == 0)
    def _():
        h_s[...] = jnp.zeros(h_s.shape, f32)

    c = jnp.where(s < n_ctx_c, s, s - n_ctx_c)
    seg_chunks = jnp.where(s < n_ctx_c, n_ctx_c, n_lat_c)
    c = (seg_chunks - 1 - c) if reverse else c
    has_prev = c > 0
    has_next = c < seg_chunks - 1
    ext = jnp.concatenate([jnp.where(has_prev, prev_ref[...], 0.0), cur_ref[...],
                           jnp.where(has_next, next_ref[...], 0.0)], axis=0)
    u = cb_ref[...]
    for k in range(CONV_WIDTH):
        u = u + ext[k * batch:k * batch + rows] * cw_ref[k:k + 1, :]

    lam = lam_ref[...]
    softplus_neg = jnp.maximum(-lam, 0.0) + jnp.log1p(jnp.exp(-jnp.abs(lam)))
    decay = (-0.5 * LRU_C) * softplus_neg
    for n in range(N_LRU_BLOCKS):
        cols = slice(n * wb, (n + 1) * wb)
        un = u[:, cols]
        ub = un.astype(bf16)
        half_r = jnp.dot(ub, (0.5 * wr_ref[n]).astype(bf16), preferred_element_type=f32) + 0.5 * br_ref[:, cols]
        half_i = jnp.dot(ub, (0.5 * wi_ref[n]).astype(bf16), preferred_element_type=f32) + 0.5 * bi_ref[:, cols]
        log_a = decay[:, cols] * (1.0 + jnp.tanh(half_r))
        a = jnp.exp(log_a)
        mult = jnp.sqrt(-jnp.tanh(log_a) * (a * a + 1.0))
        a_s[:, cols] = a
        b_s[:, cols] = (mult * (0.5 * un)) * (1.0 + jnp.tanh(half_i))

    def step(q, h):
        for j in range(SCAN_UNROLL):
            t = q * SCAN_UNROLL + j
            t = (steps - 1 - t) if reverse else t
            row = pl.multiple_of(t * batch, batch)
            h = a_s[pl.ds(row, batch), :] * h + b_s[pl.ds(row, batch), :]
            if reverse:
                z_s[pl.ds(row, batch), :] = h
            else:
                o_ref[pl.ds(row, batch), :] = h
        return h
    h_s[...] = lax.fori_loop(0, steps // SCAN_UNROLL, step, h_s[...])
    if reverse:
        o_ref[...] = (gg_ref[...] * (hf_ref[...] + z_s[...])).astype(o_ref.dtype)


def _lru_scan_call(u, hf, gg, conv_w, conv_b, w_r, b_r, w_i, b_i, lam, batch, seq, ctx_len, chunk, reverse):
    n_rows, d_rnn = u.shape
    rows = chunk * batch
    n_lat_c, n_ctx_c = seq // chunk, ctx_len // chunk
    pre, post = CONV_LEFT * batch, (CONV_WIDTH - 1 - CONV_LEFT) * batch
    wb = d_rnn // N_LRU_BLOCKS

    def blk(s):
        ci = jnp.where(s < n_ctx_c, s, s - n_ctx_c)
        if reverse:
            ci = jnp.where(s < n_ctx_c, n_ctx_c - 1 - ci, n_lat_c - 1 - ci)
        return jnp.where(s < n_ctx_c, n_lat_c + ci, ci)

    last_post = n_rows // post - 1
    cur = pl.BlockSpec((rows, d_rnn), lambda s: (blk(s), 0))
    in_specs = [pl.BlockSpec((pre, d_rnn), lambda s: (jnp.maximum(blk(s) * (rows // pre) - 1, 0), 0)),
                cur,
                pl.BlockSpec((post, d_rnn), lambda s: (jnp.minimum((blk(s) + 1) * (rows // post), last_post), 0)),
                pl.BlockSpec((CONV_WIDTH, d_rnn), lambda s: (0, 0)),
                pl.BlockSpec((1, d_rnn), lambda s: (0, 0)),
                pl.BlockSpec((N_LRU_BLOCKS, wb, wb), lambda s: (0, 0, 0)),
                pl.BlockSpec((1, d_rnn), lambda s: (0, 0)),
                pl.BlockSpec((N_LRU_BLOCKS, wb, wb), lambda s: (0, 0, 0)),
                pl.BlockSpec((1, d_rnn), lambda s: (0, 0)),
                pl.BlockSpec((1, d_rnn), lambda s: (0, 0))]
    args = [u, u, u, conv_w, conv_b.reshape(1, d_rnn), w_r, b_r.reshape(1, d_rnn), w_i, b_i.reshape(1, d_rnn),
            lam.reshape(1, d_rnn)]
    scratch = [pltpu.VMEM((rows, d_rnn), f32), pltpu.VMEM((rows, d_rnn), f32), pltpu.VMEM((batch, d_rnn), f32)]
    if reverse:
        in_specs += [cur, cur]
        args += [hf, gg]
        scratch.append(pltpu.VMEM((rows, d_rnn), f32))
    return pl.pallas_call(
        functools.partial(_lru_scan_kernel, batch=batch, n_ctx_c=n_ctx_c, n_lat_c=n_lat_c, reverse=reverse),
        grid=(n_ctx_c + n_lat_c,), in_specs=in_specs, out_specs=cur,
        out_shape=jax.ShapeDtypeStruct((n_rows, d_rnn), bf16 if reverse else f32),
        scratch_shapes=scratch, compiler_params=_params("arbitrary"),
        name="lru_scan_rev" if reverse else "lru_scan_fwd",
    )(*args)


def _lru_out_kernel(*refs, batch):
    z_ref = refs[0]
    x_refs = refs[1:1 + batch]
    w_ref, g1_ref, o_ref, y_s = refs[1 + batch:]
    tt = x_refs[0].shape[0]
    y = jnp.dot(z_ref[...], w_ref[...], preferred_element_type=f32)
    n_planes = y.shape[1] // LANES
    for s in range(n_planes):
        y_s[s] = y[:, s * LANES:(s + 1) * LANES]
    for b in range(batch):
        y_b = jnp.concatenate([y_s[s, pl.ds(b, tt, stride=batch), :] for s in range(n_planes)], axis=-1)
        o_ref[b] = x_refs[b][...] + g1_ref[b] * y_b


def _lru_out_call(z, tokens, w_out_bf, mod, batch, seq, tt):
    d = tokens.shape[1]
    d_rnn = z.shape[1]
    n_t = seq // tt
    x_specs = [pl.BlockSpec((tt, d), lambda i, b=b: (b * n_t + i, 0)) for b in range(batch)]
    return pl.pallas_call(
        functools.partial(_lru_out_kernel, batch=batch), grid=(n_t,),
        in_specs=[pl.BlockSpec((tt * batch, d_rnn), lambda i: (i, 0))] + x_specs
                 + [pl.BlockSpec((d_rnn, d), lambda i: (0, 0)), pl.BlockSpec((batch, 1, d), lambda i: (0, 0, 2))],
        out_specs=pl.BlockSpec((batch, tt, d), lambda i: (0, i, 0)),
        out_shape=jax.ShapeDtypeStruct((batch, seq, d), f32),
        scratch_shapes=[pltpu.VMEM((d // LANES, tt * batch, LANES), f32)],
        compiler_params=_params("arbitrary"), name="lru_out",
    )(z, *([tokens] * batch), w_out_bf, mod)


def kernel(x, c, ctx, c_ctx, ada_w, ada_b, norm_mix, norm_ffn, pool_w, pool_scale, lru_w_in, lru_conv_w, lru_conv_b, lru_w_r, lru_b_r, lru_w_i, lru_b_i, lru_lam, lru_w_out, router_w, router_b, exp_w_gu, exp_b_gu, exp_w_down, exp_b_down, final_norm):
    batch, seq, d = x.shape
    ctx_len = ctx.shape[1]
    assert d == SUBLANES * LANES and seq % GRID_W == 0
    n_lat, n_ctx = batch * seq, batch * ctx_len
    ctx_row = batch

    mod_rows = -(-(batch + 1) // SUBLANES) * SUBLANES
    cc = jnp.zeros((mod_rows, d), f32).at[:batch].set(c).at[batch].set(c_ctx)
    mod_all = _ada_call(cc, ada_w, ada_b)
    mods = [mod_all[l].reshape(mod_rows, 1, 6 * d) for l in range(ada_w.shape[0])]

    lat = x.reshape(n_lat, d)
    cx = ctx.reshape(n_ctx, d)
    tile = _pick_tile(ROUTE_TILE, seq, n_ctx)
    gain = lambda g: g.reshape(1, d)

    lat = _pool_lat_call(lat, _rms_call(lat, tile), gain(norm_mix[0]), pool_w[0], gain(pool_scale[0]), mods[0],
                         batch, seq)
    cx = _pool_ctx_call(cx, gain(norm_mix[0]), pool_w[0], gain(pool_scale[0]), mods[0], ctx_row, batch, ctx_len)
    tokens = _moe_layer(0, lat, cx, gain(norm_ffn[0]), mods[0], seq, ctx_row, router_w, router_b,
                        exp_w_gu, exp_b_gu, exp_w_down, exp_b_down, gain(final_norm), False)

    assert batch == SUBLANES
    tt = _pick_tile(LRU_TILE, seq, ctx_len)
    chunk = _pick_tile(SCAN_CHUNK, seq, ctx_len)
    gg, u = _lru_in_call(tokens, gain(norm_mix[1]), mods[1], ctx_row, lru_w_in[0].astype(bf16), batch, seq,
                         ctx_len, tt)
    scan = functools.partial(_lru_scan_call, conv_w=lru_conv_w[0], conv_b=lru_conv_b[0], batch=batch, seq=seq,
                             ctx_len=ctx_len, chunk=chunk)
    hf = scan(u, None, None, w_r=lru_w_r[0, 0], b_r=lru_b_r[0, 0], w_i=lru_w_i[0, 0], b_i=lru_b_i[0, 0],
              lam=lru_lam[0, 0], reverse=False)
    z = scan(u, hf, gg, w_r=lru_w_r[0, 1], b_r=lru_b_r[0, 1], w_i=lru_w_i[0, 1], b_i=lru_b_i[0, 1],
             lam=lru_lam[0, 1], reverse=True)
    lat = _lru_out_call(z, tokens, lru_w_out[0].astype(bf16), mods[1], batch, seq, tt).reshape(n_lat, d)
    out = _moe_layer(1, lat, None, gain(norm_ffn[1]), mods[1], seq, ctx_row, router_w, router_b,
                     exp_w_gu, exp_b_gu, exp_w_down, exp_b_down, gain(final_norm), True)
    return out.reshape(batch, seq, d)
```

```python
import functools

import numpy as np
import jax
import jax.numpy as jnp
from jax import lax
from jax.experimental import pallas as pl
from jax.experimental.pallas import tpu as pltpu

f32 = jnp.float32
bf16 = jnp.bfloat16
i32 = jnp.int32
i16 = jnp.int16
HIGHEST = lax.Precision.HIGHEST

LANES = 128
SUBLANES = 8
VMEM_LIMIT = 56 * 1024 * 1024

GRID_W = 64
GRID_SHIFT = 6
POOL_WINDOWS = (2, 4, 8, 16)
N_POOL_GROUPS = 4
POOL_BLOCK = 256
N_LRU_BLOCKS = 4
CONV_LEFT = 2
CONV_WIDTH = 4
LRU_C = 8.0
N_EXPERTS = 32
TOP_K = 4
SWIGLU_LIMIT = 7.0
SWIGLU_ALPHA = 1.702
NORM_EPS = 1e-6
GELU_C = 0.7978845608028654

EXPERT_ROWS = 1024
MLP_ROWS = 256
BUILD_ROWS = 128
DISPATCH_BUFFERS = 3
ROUTE_TILE = 512
LRU_TILE = 128
SCAN_CHUNK = 64
SCAN_UNROLL = 8


def _params(*sem):
    return pltpu.CompilerParams(dimension_semantics=sem, vmem_limit_bytes=VMEM_LIMIT)


def _pick_tile(pref, *sizes):
    t = pref
    while any(s % t for s in sizes):
        t //= 2
    assert t >= SUBLANES
    return t


def _norm_mod(x, gain, scale, shift):
    ms = jnp.mean(x * x, axis=-1, keepdims=True)
    y = x * lax.rsqrt(ms + NORM_EPS)
    return (y * gain) * (1.0 + scale) + shift


def _row_map(n_lat_tiles, tiles_per_batch, ctx_row):
    def row(i):
        return jnp.where(i < n_lat_tiles, i // tiles_per_batch, ctx_row)
    return row


def _mod_spec(row, chunk, width):
    return pl.BlockSpec((None, 1, width), lambda i, *_: (row(i), 0, chunk))


def _token_specs(tile, d, n_lat_tiles):
    return [pl.BlockSpec((tile, d), lambda i, *_: (jnp.minimum(i, n_lat_tiles - 1), 0)),
            pl.BlockSpec((tile, d), lambda i, *_: (jnp.maximum(i - n_lat_tiles, 0), 0))]


def _token_tile(x_ref, xc_ref, n_lat_tiles):
    return jnp.where(pl.program_id(0) >= n_lat_tiles, xc_ref[...], x_ref[...])


def _ada_kernel(c_ref, w_ref, b_ref, o_ref):
    c = c_ref[...]
    s = c * jax.nn.sigmoid(c)
    o_ref[...] = jnp.dot(s, w_ref[...], preferred_element_type=f32, precision=HIGHEST) + b_ref[...]


def _ada_call(cc, ada_w, ada_b):
    depth, d, d6 = ada_w.shape
    rows = cc.shape[0]
    bn = 1536
    return pl.pallas_call(
        _ada_kernel, grid=(depth, d6 // bn),
        in_specs=[pl.BlockSpec((rows, d), lambda l, j: (0, 0)),
                  pl.BlockSpec((None, d, bn), lambda l, j: (l, 0, j)),
                  pl.BlockSpec((None, 1, bn), lambda l, j: (l, 0, j))],
        out_specs=pl.BlockSpec((None, rows, bn), lambda l, j: (l, 0, j)),
        out_shape=jax.ShapeDtypeStruct((depth, rows, d6), f32),
        compiler_params=_params("arbitrary", "arbitrary"), name="ada_mod",
    )(cc, ada_w, ada_b.reshape(depth, 1, d6))


def _rms_kernel(x_ref, o_ref):
    x = x_ref[...]
    o_ref[...] = lax.rsqrt(jnp.mean(x * x, axis=-1, keepdims=True) + NORM_EPS)


def _rms_call(tokens, tile):
    n, d = tokens.shape
    return pl.pallas_call(
        _rms_kernel, grid=(n // tile,),
        in_specs=[pl.BlockSpec((tile, d), lambda i: (i, 0))],
        out_specs=pl.BlockSpec((tile, 1), lambda i: (i, 0)),
        out_shape=jax.ShapeDtypeStruct((n, 1), f32),
        compiler_params=_params("arbitrary"), name="row_rms",
    )(tokens)


def _shift(x, j, unit, pos, limit):
    n = x.shape[0]
    rolled = pltpu.roll(x, (j * unit) % n, 0)
    ok = (pos >= j) if j > 0 else (pos < limit + j)
    return jnp.where(ok, rolled, 0.0)


def _window_sum(x, w, unit, pos, limit):
    m = w // 2
    trail, lead, s = x, x, 1
    while s < m:
        trail = trail + _shift(trail, s, unit, pos, limit)
        lead = lead + _shift(lead, -s, unit, pos, limit)
        s *= 2
    return _shift(trail, 1, unit, pos, limit) + lead


def _window_count(w, pos, limit):
    m = w // 2
    return jnp.minimum(pos + m, limit) - jnp.maximum(pos - m, 0)


def _col_window_sum(h, w):
    blk = POOL_BLOCK
    i = lax.broadcasted_iota(i32, (blk, blk), 0)
    j = lax.broadcasted_iota(i32, (blk, blk), 1)
    off = (j & (GRID_W - 1)) - (i & (GRID_W - 1))
    same_row = (i >> GRID_SHIFT) == (j >> GRID_SHIFT)
    band = jnp.where(jnp.logical_and(same_row, jnp.logical_and(off >= -(w // 2), off < w // 2)), 1.0, 0.0)
    band = band.astype(bf16)
    out = []
    for r0 in range(0, h.shape[0], blk):
        hb = h[r0:r0 + blk]
        hi = hb.astype(bf16)
        lo = (hb - hi.astype(f32)).astype(bf16)
        out.append(jnp.dot(band, hi, preferred_element_type=f32) + jnp.dot(band, lo, preferred_element_type=f32))
    return jnp.concatenate(out, axis=0)


def _pool_grid(h, w):
    t = lax.broadcasted_iota(i32, h.shape, 0)
    col, row = t & (GRID_W - 1), t >> GRID_SHIFT
    rows = h.shape[0] // GRID_W
    total = _window_sum(_col_window_sum(h, w), w, GRID_W, row, rows)
    count = (_window_count(w, row, rows) * _window_count(w, col, GRID_W)).astype(f32)
    return total / count


def _pool_seq(h, w):
    t = lax.broadcasted_iota(i32, h.shape, 0)
    n = h.shape[0]
    return _window_sum(h, w, 1, t, n) / _window_count(w, t, n).astype(f32)


def _pool_lat_kernel(x_ref, r_ref, gain_ref, sc_ref, sh_ref, w_ref, ps_ref, g1_ref, o_ref):
    grp = pl.program_id(1)
    for gi, w in enumerate(POOL_WINDOWS):
        @pl.when(grp == gi)
        def _(w=w):
            x = x_ref[...]
            h = ((x * r_ref[...]) * gain_ref[...]) * (1.0 + sc_ref[...]) + sh_ref[...]
            d = _pool_grid(h, w) - h
            y = jnp.dot(d.astype(bf16), w_ref[...].astype(bf16), preferred_element_type=f32)
            o_ref[...] = x + g1_ref[...] * (y * ps_ref[...])


def _pool_lat_call(x, rms, gain, pool_w, pool_scale, mod, batch, seq):
    n, d = x.shape
    cg = d // N_POOL_GROUPS
    mod_chunk = lambda c: pl.BlockSpec((None, 1, cg), lambda b, g: (b, 0, c * N_POOL_GROUPS + g))
    return pl.pallas_call(
        _pool_lat_kernel, grid=(batch, N_POOL_GROUPS),
        in_specs=[pl.BlockSpec((seq, cg), lambda b, g: (b, g)),
                  pl.BlockSpec((seq, 1), lambda b, g: (b, 0)),
                  pl.BlockSpec((1, cg), lambda b, g: (0, g)),
                  mod_chunk(1), mod_chunk(0),
                  pl.BlockSpec((None, cg, cg), lambda b, g: (g, 0, 0)),
                  pl.BlockSpec((1, cg), lambda b, g: (0, g)),
                  mod_chunk(2)],
        out_specs=pl.BlockSpec((seq, cg), lambda b, g: (b, g)),
        out_shape=jax.ShapeDtypeStruct((n, d), f32),
        compiler_params=_params("arbitrary", "arbitrary"), name="pool_lat",
    )(x, rms, gain, mod, mod, pool_w, pool_scale, mod)


def _pool_ctx_kernel(x_ref, gain_ref, sc_ref, sh_ref, w_ref, ps_ref, g1_ref, o_ref):
    x = x_ref[...]
    h = _norm_mod(x, gain_ref[...], sc_ref[...], sh_ref[...])
    cg = x.shape[1] // N_POOL_GROUPS
    for gi, w in enumerate(POOL_WINDOWS):
        sl = slice(gi * cg, (gi + 1) * cg)
        hg = h[:, sl]
        d = _pool_seq(hg, w) - hg
        y = jnp.dot(d.astype(bf16), w_ref[gi].astype(bf16), preferred_element_type=f32)
        o_ref[:, sl] = x[:, sl] + g1_ref[:, sl] * (y * ps_ref[:, sl])


def _pool_ctx_call(cx, gain, pool_w, pool_scale, mod, ctx_row, batch, ctx_len):
    n, d = cx.shape
    cg = d // N_POOL_GROUPS
    row = lambda i: ctx_row
    return pl.pallas_call(
        _pool_ctx_kernel, grid=(batch,),
        in_specs=[pl.BlockSpec((ctx_len, d), lambda b: (b, 0)),
                  pl.BlockSpec((1, d), lambda b: (0, 0)),
                  _mod_spec(row, 1, d), _mod_spec(row, 0, d),
                  pl.BlockSpec((N_POOL_GROUPS, cg, cg), lambda b: (0, 0, 0)),
                  pl.BlockSpec((1, d), lambda b: (0, 0)),
                  _mod_spec(row, 2, d)],
        out_specs=pl.BlockSpec((ctx_len, d), lambda b: (b, 0)),
        out_shape=jax.ShapeDtypeStruct((n, d), f32),
        compiler_params=_params("arbitrary"), name="pool_ctx",
    )(cx, gain, mod, mod, pool_w, pool_scale, mod)


def _route_kernel(x_ref, xc_ref, gain_ref, sc_ref, sh_ref, wr_ref, rb_ref, loc_ref, gate_ref, cnt_ref, *, n_lat_tiles):
    tt = x_ref.shape[0]
    h = _norm_mod(_token_tile(x_ref, xc_ref, n_lat_tiles), gain_ref[...], sc_ref[...], sh_ref[...])

    def split(v):
        hi = v.astype(bf16)
        return hi, (v - hi.astype(f32)).astype(bf16)
    contract_last = (((1,), (1,)), ((), ()))
    w_hi, w_lo = split(wr_ref[...])
    h_hi, h_lo = split(h)
    by_hi = lax.dot_general(jnp.concatenate([w_hi, w_lo], axis=0), h_hi, contract_last, preferred_element_type=f32)
    by_lo = lax.dot_general(w_hi, h_lo, contract_last, preferred_element_type=f32)
    logits = by_hi[:N_EXPERTS] + (by_hi[N_EXPERTS:] + by_lo) + rb_ref[...]
    e_iota = lax.broadcasted_iota(i32, logits.shape, 0)
    vals, onehots = [], []
    work = logits
    for k in range(TOP_K):
        m = jnp.max(work, axis=0, keepdims=True)
        idx = jnp.min(jnp.where(work == m, e_iota, N_EXPERTS), axis=0, keepdims=True)
        hit = e_iota == idx
        vals.append(m)
        onehots.append(jnp.where(hit, 1.0, 0.0))
        work = jnp.where(hit, -jnp.inf, work)

    ex = [jnp.exp(v - vals[0]) for v in vals]
    den = ex[0] + ex[1] + ex[2] + ex[3]
    for k in range(TOP_K):
        gate_ref[k:k + 1, :] = ex[k] / den

    tri = jnp.where(lax.broadcasted_iota(i32, (tt, tt), 0) <= lax.broadcasted_iota(i32, (tt, tt), 1), 1.0, 0.0)
    planes = jnp.concatenate(onehots, axis=0).astype(bf16)
    incl = jnp.dot(planes, tri.astype(bf16), preferred_element_type=f32)
    totals = [incl[k * N_EXPERTS:(k + 1) * N_EXPERTS, tt - 1:tt] for k in range(TOP_K)]
    count = jnp.broadcast_to(totals[0] + totals[1] + totals[2] + totals[3], (N_EXPERTS, LANES))
    padded = (((count.astype(i32) + (SUBLANES - 1)) // SUBLANES) * SUBLANES).astype(f32)
    below = jnp.where(lax.broadcasted_iota(i32, (N_EXPERTS, N_EXPERTS), 1)
                      < lax.broadcasted_iota(i32, (N_EXPERTS, N_EXPERTS), 0), 1.0, 0.0)
    offset = jnp.dot(below.astype(bf16), padded.astype(bf16), preferred_element_type=f32)[:, 0:1]
    for k in range(TOP_K):
        inc_k = incl[k * N_EXPERTS:(k + 1) * N_EXPERTS]
        before = inc_k - onehots[k] + offset
        loc_ref[k:k + 1, :] = jnp.sum(onehots[k] * before, axis=0, keepdims=True).astype(i32)
        offset = offset + totals[k]
    cnt_ref[...] = count


def _route_call(lat, cx, n, n_lat_tiles, gain, mod, row, w_router_t, b_router, tile):
    d = lat.shape[1]
    e = w_router_t.shape[0]
    return pl.pallas_call(
        functools.partial(_route_kernel, n_lat_tiles=n_lat_tiles), grid=(n // tile,),
        in_specs=_token_specs(tile, d, n_lat_tiles) + [
                  pl.BlockSpec((1, d), lambda i: (0, 0)),
                  _mod_spec(row, 4, d), _mod_spec(row, 3, d),
                  pl.BlockSpec((e, d), lambda i: (0, 0)),
                  pl.BlockSpec((e, 1), lambda i: (0, 0))],
        out_specs=[pl.BlockSpec((TOP_K, tile), lambda i: (0, i)),
                   pl.BlockSpec((TOP_K, tile), lambda i: (0, i)),
                   pl.BlockSpec((None, e, LANES), lambda i: (i, 0, 0))],
        out_shape=[jax.ShapeDtypeStruct((TOP_K, n), i32),
                   jax.ShapeDtypeStruct((TOP_K, n), f32),
                   jax.ShapeDtypeStruct((n // tile, e, LANES), f32)],
        compiler_params=_params("arbitrary"), name="moe_route",
    )(lat, cx, gain, mod, mod, w_router_t, b_router.reshape(e, 1))


def _copy_run(src_at, dst_at, n_pieces, n_bits, sem, wait=False):
    for k in reversed(range(n_bits)):
        size = SUBLANES << k
        done = (n_pieces >> (k + 1)) << (k + 1)

        @pl.when(((n_pieces >> k) & 1) == 1)
        def _(size=size, done=done):
            off = 0 if wait else pl.multiple_of(done * SUBLANES, SUBLANES)
            copy = pltpu.make_async_copy(src_at(off, size), dst_at(off, size), sem)
            copy.wait() if wait else copy.start()


def _dispatch_kernel(go_ref, lb_ref, p8_ref, np_ref, ts_ref, tn_ref, nu_ref,
                     x_ref, xc_ref, gain_ref, sc_ref, sh_ref, loc_ref, xs_hbm, xloc, zbuf, sel_s, sem,
                     *, n_tiles, n_blocks, n_lat_tiles):
    i = pl.program_id(0)
    tt = x_ref.shape[0]
    rl = xloc.shape[1]
    slot = i % DISPATCH_BUFFERS

    def build(loc, dst):
        rows = [loc[k:k + 1, :].astype(i16) for k in range(TOP_K)]
        one, zero = jnp.ones((), bf16), jnp.zeros((), bf16)
        for r0 in range(0, rl, BUILD_ROWS):
            r_iota = lax.broadcasted_iota(i16, (BUILD_ROWS, tt), 0) + r0
            sel = None
            for k in range(TOP_K):
                m = jnp.where(r_iota == rows[k], one, zero)
                sel = m if sel is None else sel + m
            dst[r0:r0 + BUILD_ROWS, :] = sel

    fill_sem = sem.at[DISPATCH_BUFFERS]
    tail_bits = (EXPERT_ROWS // SUBLANES - 1).bit_length()
    zero_at = lambda off, size: zbuf.at[pl.ds(0, size), :]

    @pl.when(i == 0)
    def _():
        zbuf[...] = jnp.zeros(zbuf.shape, f32)

        def tail(e, carry):
            t0 = pl.multiple_of(ts_ref[e], SUBLANES)
            _copy_run(zero_at, lambda off, size: xs_hbm.at[pl.ds(t0 + off, size), :], tn_ref[e], tail_bits,
                      fill_sem)
            return carry
        lax.fori_loop(0, N_EXPERTS, tail, 0)

        def spare(b, carry):
            t = pl.multiple_of(b * EXPERT_ROWS, EXPERT_ROWS)
            pltpu.make_async_copy(zbuf, xs_hbm.at[pl.ds(t, EXPERT_ROWS), :], fill_sem).start()
            return carry
        lax.fori_loop(nu_ref[0], n_blocks, spare, 0)

    seg_bits = (tt // SUBLANES).bit_length()
    tile_bits = (rl // SUBLANES).bit_length()
    buf_at = lambda off, size: xloc.at[0, pl.ds(off, size), :]
    hbm_at = lambda off, size: xs_hbm.at[pl.ds(off, size), :]

    def segment_copies(tile, e, buf, live):
        idx = tile * N_EXPERTS + e
        src0 = pl.multiple_of(lb_ref[idx], SUBLANES)
        dst0 = pl.multiple_of(go_ref[idx], SUBLANES)
        _copy_run(lambda off, size: xloc.at[buf, pl.ds(src0 + off, size), :],
                  lambda off, size: xs_hbm.at[pl.ds(dst0 + off, size), :],
                  jnp.where(live, p8_ref[idx] // SUBLANES, 0), seg_bits, sem.at[buf])

    @pl.when(i >= DISPATCH_BUFFERS)
    def _():
        _copy_run(buf_at, hbm_at, np_ref[jnp.maximum(i - DISPATCH_BUFFERS, 0)], tile_bits, sem.at[slot], wait=True)

    for cur in range(DISPATCH_BUFFERS):
        @pl.when(slot == cur)
        def _(cur=cur):
            for e in range(N_EXPERTS):
                segment_copies(jnp.maximum(i - 1, 0), e, (cur - 1) % DISPATCH_BUFFERS, i >= 1)
            build(loc_ref, sel_s)
            h = _norm_mod(_token_tile(x_ref, xc_ref, n_lat_tiles), gain_ref[...], sc_ref[...], sh_ref[...])
            xloc[cur] = jnp.dot(sel_s[...], h.astype(bf16), preferred_element_type=f32)

    @pl.when(i == n_tiles - 1)
    def _():
        def per_expert(e, carry):
            segment_copies(i, e, slot, True)
            return carry
        lax.fori_loop(0, N_EXPERTS, per_expert, 0)

        for back in range(DISPATCH_BUFFERS - 1, -1, -1):
            @pl.when(i >= back)
            def _(back=back):
                t = jnp.maximum(i - back, 0)
                _copy_run(buf_at, hbm_at, np_ref[t], tile_bits, sem.at[t % DISPATCH_BUFFERS], wait=True)

        def tail_wait(e, carry):
            _copy_run(zero_at, hbm_at, tn_ref[e], tail_bits, fill_sem, wait=True)
            return carry
        lax.fori_loop(0, N_EXPERTS, tail_wait, 0)

        def spare_wait(b, carry):
            pltpu.make_async_copy(zbuf, xs_hbm.at[pl.ds(0, EXPERT_ROWS), :], fill_sem).wait()
            return carry
        lax.fori_loop(nu_ref[0], n_blocks, spare_wait, 0)


def _dispatch_call(tables, lat, cx, n, n_lat_tiles, gain, mod, row, loc, n_rows, tile):
    d = lat.shape[1]
    n_tiles = n // tile
    rl = TOP_K * tile + N_EXPERTS * SUBLANES
    spec = lambda shape, fn: pl.BlockSpec(shape, lambda i, *_: fn(i))
    grid_spec = pltpu.PrefetchScalarGridSpec(
        num_scalar_prefetch=7, grid=(n_tiles,),
        in_specs=_token_specs(tile, d, n_lat_tiles) + [
                  spec((1, d), lambda i: (0, 0)),
                  _mod_spec(row, 4, d), _mod_spec(row, 3, d),
                  spec((TOP_K, tile), lambda i: (0, i))],
        out_specs=pl.BlockSpec(memory_space=pl.ANY),
        scratch_shapes=[pltpu.VMEM((DISPATCH_BUFFERS, rl, d), f32), pltpu.VMEM((EXPERT_ROWS, d), f32),
                        pltpu.VMEM((rl, tile), bf16), pltpu.SemaphoreType.DMA((DISPATCH_BUFFERS + 1,))])
    return pl.pallas_call(
        functools.partial(_dispatch_kernel, n_tiles=n_tiles, n_blocks=n_rows // EXPERT_ROWS, n_lat_tiles=n_lat_tiles),
        grid_spec=grid_spec,
        out_shape=jax.ShapeDtypeStruct((n_rows, d), f32),
        compiler_params=_params("arbitrary"), name="moe_dispatch",
    )(*tables, lat, cx, gain, mod, mod, loc)


def _gmm_kernel(be_ref, bv_ref, lu_ref, x_ref, wgu_ref, bgu_ref, wd_ref, bd_ref, o_ref, wgu_bf, wd_bf):
    j = pl.program_id(0)
    d_ff = wd_ref.shape[0]
    rows = x_ref.shape[0]
    valid = bv_ref[j]

    @pl.when(valid > 0)
    def _():
        new_expert = jnp.logical_or(j == 0, be_ref[j] != be_ref[jnp.maximum(j - 1, 0)])

        @pl.when(new_expert)
        def _():
            wgu_bf[...] = wgu_ref[...].astype(bf16)
            wd_bf[...] = wd_ref[...].astype(bf16)

    def mlp(r0):
        rs = slice(r0, r0 + MLP_ROWS)
        x = x_ref[rs, :].astype(bf16)
        gu = jnp.dot(x, wgu_bf[...], preferred_element_type=f32) + bgu_ref[...]
        g = jnp.minimum(gu[:, :d_ff], SWIGLU_LIMIT)
        u = jnp.clip(gu[:, d_ff:], -SWIGLU_LIMIT, SWIGLU_LIMIT)
        act = (u + 1.0) * g * jax.nn.sigmoid(SWIGLU_ALPHA * g)
        o_ref[rs, :] = jnp.dot(act.astype(bf16), wd_bf[...], preferred_element_type=f32) + bd_ref[...]

    n_groups = rows // MLP_ROWS
    for full in range(n_groups + 1):
        lo, hi = (full - 1) * MLP_ROWS, full * MLP_ROWS
        cond = (valid == 0) if full == 0 else jnp.logical_and(valid > lo, valid <= hi)
        if full == n_groups:
            cond = valid > lo

        @pl.when(cond)
        def _(full=full):
            for gidx in range(full):
                mlp(gidx * MLP_ROWS)
            if full < n_groups:
                o_ref[full * MLP_ROWS:, :] = jnp.zeros((rows - full * MLP_ROWS, o_ref.shape[1]), f32)


def _gmm_call(layer, blk_e, blk_valid, last_used, xs, w_gu, b_gu, w_down, b_down):
    depth, e, d, f2 = w_gu.shape
    d_ff = w_down.shape[2]
    rows = EXPERT_ROWS
    n_blocks = xs.shape[0] // rows
    grid_spec = pltpu.PrefetchScalarGridSpec(
        num_scalar_prefetch=3, grid=(n_blocks,),
        in_specs=[pl.BlockSpec((rows, d), lambda j, be, bv, lu: (jnp.minimum(j, lu[0]), 0)),
                  pl.BlockSpec((None, None, d, f2), lambda j, be, bv, lu: (layer, be[j], 0, 0)),
                  pl.BlockSpec((None, None, 1, f2), lambda j, be, bv, lu: (layer, be[j], 0, 0)),
                  pl.BlockSpec((None, None, d_ff, d), lambda j, be, bv, lu: (layer, be[j], 0, 0)),
                  pl.BlockSpec((None, None, 1, d), lambda j, be, bv, lu: (layer, be[j], 0, 0))],
        out_specs=pl.BlockSpec((rows, d), lambda j, be, bv, lu: (j, 0)),
        scratch_shapes=[pltpu.VMEM((d, f2), bf16), pltpu.VMEM((d_ff, d), bf16)])
    return pl.pallas_call(
        _gmm_kernel, grid_spec=grid_spec,
        out_shape=jax.ShapeDtypeStruct((n_blocks * rows, d), f32),
        compiler_params=_params("arbitrary"), name="moe_experts",
    )(blk_e, blk_valid, last_used, xs, w_gu, b_gu.reshape(depth, e, 1, f2), w_down, b_down.reshape(depth, e, 1, d))


def _combine_kernel(go_ref, lb_ref, p8_ref, np_ref,
                    ys_hbm, loc_ref, gate_ref, x_ref, xc_ref, g2_ref, fn_ref, o_ref, yloc, w_s, sem,
                    *, n_tiles, n_lat_tiles, final):
    i = pl.program_id(0)
    tt = x_ref.shape[0]
    rl = yloc.shape[1]
    slot = i % 2

    def build(loc, gate, dst):
        cols = [(loc[:, k:k + 1].astype(i16), gate[:, k:k + 1].astype(bf16)) for k in range(TOP_K)]
        zero = jnp.zeros((), bf16)
        for c0 in range(0, rl, BUILD_ROWS):
            r_iota = lax.broadcasted_iota(i16, (tt, BUILD_ROWS), 1) + c0
            weights = None
            for at, g in cols:
                m = jnp.where(r_iota == at, g, zero)
                weights = m if weights is None else weights + m
            dst[:, c0:c0 + BUILD_ROWS] = weights

    seg_bits = (tt // SUBLANES).bit_length()
    tile_bits = (rl // SUBLANES).bit_length()

    def segment_fetch(tile, e, buf, live):
        idx = tile * N_EXPERTS + e
        src0 = pl.multiple_of(go_ref[idx], SUBLANES)
        dst0 = pl.multiple_of(lb_ref[idx], SUBLANES)
        _copy_run(lambda off, size: ys_hbm.at[pl.ds(src0 + off, size), :],
                  lambda off, size: yloc.at[buf, pl.ds(dst0 + off, size), :],
                  jnp.where(live, p8_ref[idx] // SUBLANES, 0), seg_bits, sem.at[buf])

    @pl.when(i == 0)
    def _():
        yloc[...] = jnp.zeros(yloc.shape, f32)

        def per_expert(e, carry):
            segment_fetch(0, e, 0, True)
            return carry
        lax.fori_loop(0, N_EXPERTS, per_expert, 0)

    _copy_run(lambda off, size: ys_hbm.at[pl.ds(off, size), :], lambda off, size: yloc.at[0, pl.ds(off, size), :],
              np_ref[i], tile_bits, sem.at[slot], wait=True)

    for parity in range(2):
        @pl.when(slot == parity)
        def _(parity=parity):
            for e in range(N_EXPERTS):
                segment_fetch(jnp.minimum(i + 1, n_tiles - 1), e, 1 - parity, i + 1 < n_tiles)
            build(loc_ref, gate_ref, w_s)
            y = jnp.dot(w_s[...], yloc[parity].astype(bf16), preferred_element_type=f32)
            out = _token_tile(x_ref, xc_ref, n_lat_tiles) + g2_ref[...] * y
            if final:
                ms = jnp.mean(out * out, axis=-1, keepdims=True)
                out = (out * lax.rsqrt(ms + NORM_EPS)) * fn_ref[...]
            o_ref[...] = out


def _combine_call(tables, ys, loc_t, gates_t, lat, cx, n, n_lat_tiles, mod, row, final_gain, tile, final):
    d = lat.shape[1]
    n_tiles = n // tile
    rl = TOP_K * tile + N_EXPERTS * SUBLANES
    spec = lambda shape, fn: pl.BlockSpec(shape, lambda i, *_: fn(i))
    grid_spec = pltpu.PrefetchScalarGridSpec(
        num_scalar_prefetch=4, grid=(n_tiles,),
        in_specs=[pl.BlockSpec(memory_space=pl.ANY),
                  spec((tile, TOP_K), lambda i: (i, 0)),
                  spec((tile, TOP_K), lambda i: (i, 0))] + _token_specs(tile, d, n_lat_tiles) + [
                  _mod_spec(row, 5, d),
                  spec((1, d), lambda i: (0, 0))],
        out_specs=spec((tile, d), lambda i: (i, 0)),
        scratch_shapes=[pltpu.VMEM((2, rl, d), f32), pltpu.VMEM((tile, rl), bf16),
                        pltpu.SemaphoreType.DMA((2,))])
    return pl.pallas_call(
        functools.partial(_combine_kernel, n_tiles=n_tiles, n_lat_tiles=n_lat_tiles, final=final), grid_spec=grid_spec,
        out_shape=jax.ShapeDtypeStruct((n, d), f32),
        compiler_params=_params("arbitrary"), name="moe_combine",
    )(*tables, ys, loc_t, gates_t, lat, cx, mod, final_gain)


def _moe_layer(layer, lat, cx, gain, mod, seq, ctx_row, w_router, b_router, w_gu, b_gu, w_down, b_down,
               final_gain, final):
    n_lat, d = lat.shape
    n = n_lat + (0 if cx is None else cx.shape[0])
    tile = _pick_tile(ROUTE_TILE, n_lat, n - n_lat if n > n_lat else n_lat, seq)
    n_tiles, n_lat_tiles = n // tile, n_lat // tile
    cx = lat if cx is None else cx
    row = _row_map(n_lat_tiles, seq // tile, ctx_row)

    loc, gate, cnt = _route_call(lat, cx, n, n_lat_tiles, gain, mod, row, w_router[layer].T, b_router[layer], tile)

    rows = EXPERT_ROWS
    count = cnt[:, :, 0].astype(i32)
    seg = (count + SUBLANES - 1) // SUBLANES * SUBLANES
    local_base = jnp.cumsum(seg, axis=1) - seg
    pieces = jnp.sum(seg, axis=1) // SUBLANES
    expert_rows = jnp.sum(seg, axis=0)
    region = (expert_rows + rows - 1) // rows * rows
    region_end = jnp.cumsum(region)
    region_start = region_end - region
    global_off = region_start[None, :] + jnp.cumsum(seg, axis=0) - seg
    n_blocks = (n * TOP_K + n_tiles * N_EXPERTS * (SUBLANES - 1) + N_EXPERTS * (rows - 1) + rows - 1) // rows
    blk_start = jnp.arange(n_blocks, dtype=i32) * rows
    experts = jnp.arange(N_EXPERTS, dtype=i32)
    blk_e = jnp.minimum(jnp.sum(region_end[None, :] <= blk_start[:, None], axis=1), N_EXPERTS - 1).astype(i32)
    content_end = region_start + expert_rows
    blk_end = jnp.sum(jnp.where(blk_e[:, None] == experts[None, :], content_end[None, :], 0), axis=1)
    blk_valid = jnp.clip(blk_end - blk_start, 0, rows).astype(i32)
    n_used = (region_end[-1:] // rows).astype(i32)
    last_used = jnp.maximum(n_used - 1, 0)
    tail_n = ((region - expert_rows) // SUBLANES).astype(i32)
    flat = lambda a: a.reshape(-1).astype(i32)
    tables = (flat(global_off), flat(local_base), flat(seg), pieces.astype(i32))

    xs = _dispatch_call(tables + (content_end.astype(i32), tail_n, n_used), lat, cx, n, n_lat_tiles, gain, mod, row,
                        loc, n_blocks * rows, tile)
    ys = _gmm_call(layer, blk_e, blk_valid, last_used, xs, w_gu, b_gu, w_down, b_down)
    return _combine_call(tables, ys, loc.T, gate.T, lat, cx, n, n_lat_tiles, mod, row, final_gain, tile, final)


def _time_tile_blocks(n_lat_t, n_ctx_t, n_lat_blocks, b):
    def index(i):
        lat_blk = b * n_lat_t + jnp.minimum(i, n_lat_t - 1)
        ctx_blk = n_lat_blocks + b * n_ctx_t + jnp.maximum(i - n_lat_t, 0)
        return (jnp.where(i < n_lat_t, lat_blk, ctx_blk), 0)
    return index


def _lru_in_kernel(*refs, batch, n_lat_t):
    x_refs = refs[:batch]
    gain_ref, sc_ref, sh_ref, scc_ref, shc_ref, w_ref, gg_ref, u_ref, tm_s = refs[batch:]
    tt = x_refs[0].shape[0]
    d_rnn = gg_ref.shape[1]
    is_ctx = pl.program_id(0) >= n_lat_t
    hs = []
    for b in range(batch):
        scale = jnp.where(is_ctx, scc_ref[...], sc_ref[b])
        shift = jnp.where(is_ctx, shc_ref[...], sh_ref[b])
        hs.append(_norm_mod(x_refs[b][...], gain_ref[...], scale, shift).astype(bf16))
    gu = jnp.dot(jnp.concatenate(hs, axis=0), w_ref[...], preferred_element_type=f32)
    gate = gu[:, :d_rnn]
    gelu = 0.5 * gate * (1.0 + jnp.tanh(GELU_C * (gate + 0.044715 * (gate * gate * gate))))
    for out_ref, val in ((gg_ref, gelu), (u_ref, gu[:, d_rnn:])):
        for b in range(batch):
            for s in range(d_rnn // LANES):
                tm_s[s, pl.ds(b, tt, stride=batch), :] = val[b * tt:(b + 1) * tt, s * LANES:(s + 1) * LANES]
        for s in range(d_rnn // LANES):
            out_ref[:, s * LANES:(s + 1) * LANES] = tm_s[s]


def _lru_in_call(tokens, gain, mod, ctx_row, w_in_bf, batch, seq, ctx_len, tt):
    n, d = tokens.shape
    d2 = w_in_bf.shape[1]
    d_rnn = d2 // 2
    n_lat_t, n_ctx_t = seq // tt, ctx_len // tt
    n_lat_blocks = batch * n_lat_t
    x_specs = [pl.BlockSpec((tt, d), _time_tile_blocks(n_lat_t, n_ctx_t, n_lat_blocks, b)) for b in range(batch)]
    mod_lat = lambda chunk: pl.BlockSpec((batch, 1, d), lambda i: (0, 0, chunk))
    mod_ctx = lambda chunk: pl.BlockSpec((None, 1, d), lambda i: (ctx_row, 0, chunk))
    out_spec = pl.BlockSpec((tt * batch, d_rnn), lambda i: (i, 0))
    out_shape = jax.ShapeDtypeStruct(((seq + ctx_len) * batch, d_rnn), f32)
    return pl.pallas_call(
        functools.partial(_lru_in_kernel, batch=batch, n_lat_t=n_lat_t), grid=(n_lat_t + n_ctx_t,),
        in_specs=x_specs + [pl.BlockSpec((1, d), lambda i: (0, 0)), mod_lat(1), mod_lat(0), mod_ctx(1), mod_ctx(0),
                            pl.BlockSpec((d, d2), lambda i: (0, 0))],
        out_specs=[out_spec, out_spec], out_shape=[out_shape, out_shape],
        scratch_shapes=[pltpu.VMEM((d_rnn // LANES, tt * batch, LANES), f32)],
        compiler_params=_params("arbitrary"), name="lru_in",
    )(*([tokens] * batch), gain, mod, mod, mod, mod, w_in_bf)


def _lru_scan_kernel(*refs, batch, n_ctx_c, n_lat_c, reverse):
    if reverse:
        (prev_ref, cur_ref, next_ref, cw_ref, cb_ref, wr_ref, br_ref, wi_ref, bi_ref, lam_ref, hf_ref, gg_ref,
         o_ref, a_s, b_s, h_s, z_s) = refs
    else:
        (prev_ref, cur_ref, next_ref, cw_ref, cb_ref, wr_ref, br_ref, wi_ref, bi_ref, lam_ref,
         o_ref, a_s, b_s, h_s) = refs
    s = pl.program_id(0)
    rows, d_rnn = cur_ref.shape
    steps = rows // batch
    wb = d_rnn // N_LRU_BLOCKS

    @pl.when(s == 0)
    def _():
        h_s[...] = jnp.zeros(h_s.shape, f32)

    c = jnp.where(s < n_ctx_c, s, s - n_ctx_c)
    seg_chunks = jnp.where(s < n_ctx_c, n_ctx_c, n_lat_c)
    c = (seg_chunks - 1 - c) if reverse else c
    has_prev = c > 0
    has_next = c < seg_chunks - 1
    ext = jnp.concatenate([jnp.where(has_prev, prev_ref[...], 0.0), cur_ref[...],
                           jnp.where(has_next, next_ref[...], 0.0)], axis=0)
    u = cb_ref[...]
    for k in range(CONV_WIDTH):
        u = u + ext[k * batch:k * batch + rows] * cw_ref[k:k + 1, :]

    lam = lam_ref[...]
    softplus_neg = jnp.maximum(-lam, 0.0) + jnp.log1p(jnp.exp(-jnp.abs(lam)))
    decay = (-0.5 * LRU_C) * softplus_neg
    for n in range(N_LRU_BLOCKS):
        cols = slice(n * wb, (n + 1) * wb)
        un = u[:, cols]
        ub = un.astype(bf16)
        half_r = jnp.dot(ub, (0.5 * wr_ref[n]).astype(bf16), preferred_element_type=f32) + 0.5 * br_ref[:, cols]
        half_i = jnp.dot(ub, (0.5 * wi_ref[n]).astype(bf16), preferred_element_type=f32) + 0.5 * bi_ref[:, cols]
        log_a = decay[:, cols] * (1.0 + jnp.tanh(half_r))
        a = jnp.exp(log_a)
        mult = jnp.sqrt(-jnp.tanh(log_a) * (a * a + 1.0))
        a_s[:, cols] = a
        b_s[:, cols] = (mult * (0.5 * un)) * (1.0 + jnp.tanh(half_i))

    def step(q, h):
        for j in range(SCAN_UNROLL):
            t = q * SCAN_UNROLL + j
            t = (steps - 1 - t) if reverse else t
            row = pl.multiple_of(t * batch, batch)
            h = a_s[pl.ds(row, batch), :] * h + b_s[pl.ds(row, batch), :]
            if reverse:
                z_s[pl.ds(row, batch), :] = h
            else:
                o_ref[pl.ds(row, batch), :] = h
        return h
    h_s[...] = lax.fori_loop(0, steps // SCAN_UNROLL, step, h_s[...])
    if reverse:
        o_ref[...] = (gg_ref[...] * (hf_ref[...] + z_s[...])).astype(o_ref.dtype)


def _lru_scan_call(u, hf, gg, conv_w, conv_b, w_r, b_r, w_i, b_i, lam, batch, seq, ctx_len, chunk, reverse):
    n_rows, d_rnn = u.shape
    rows = chunk * batch
    n_lat_c, n_ctx_c = seq // chunk, ctx_len // chunk
    pre, post = CONV_LEFT * batch, (CONV_WIDTH - 1 - CONV_LEFT) * batch
    wb = d_rnn // N_LRU_BLOCKS

    def blk(s):
        ci = jnp.where(s < n_ctx_c, s, s - n_ctx_c)
        if reverse:
            ci = jnp.where(s < n_ctx_c, n_ctx_c - 1 - ci, n_lat_c - 1 - ci)
        return jnp.where(s < n_ctx_c, n_lat_c + ci, ci)

    last_post = n_rows // post - 1
    cur = pl.BlockSpec((rows, d_rnn), lambda s: (blk(s), 0))
    in_specs = [pl.BlockSpec((pre, d_rnn), lambda s: (jnp.maximum(blk(s) * (rows // pre) - 1, 0), 0)),
                cur,
                pl.BlockSpec((post, d_rnn), lambda s: (jnp.minimum((blk(s) + 1) * (rows // post), last_post), 0)),
                pl.BlockSpec((CONV_WIDTH, d_rnn), lambda s: (0, 0)),
                pl.BlockSpec((1, d_rnn), lambda s: (0, 0)),
                pl.BlockSpec((N_LRU_BLOCKS, wb, wb), lambda s: (0, 0, 0)),
                pl.BlockSpec((1, d_rnn), lambda s: (0, 0)),
                pl.BlockSpec((N_LRU_BLOCKS, wb, wb), lambda s: (0, 0, 0)),
                pl.BlockSpec((1, d_rnn), lambda s: (0, 0)),
                pl.BlockSpec((1, d_rnn), lambda s: (0, 0))]
    args = [u, u, u, conv_w, conv_b.reshape(1, d_rnn), w_r, b_r.reshape(1, d_rnn), w_i, b_i.reshape(1, d_rnn),
            lam.reshape(1, d_rnn)]
    scratch = [pltpu.VMEM((rows, d_rnn), f32), pltpu.VMEM((rows, d_rnn), f32), pltpu.VMEM((batch, d_rnn), f32)]
    if reverse:
        in_specs += [cur, cur]
        args += [hf, gg]
        scratch.append(pltpu.VMEM((rows, d_rnn), f32))
    return pl.pallas_call(
        functools.partial(_lru_scan_kernel, batch=batch, n_ctx_c=n_ctx_c, n_lat_c=n_lat_c, reverse=reverse),
        grid=(n_ctx_c + n_lat_c,), in_specs=in_specs, out_specs=cur,
        out_shape=jax.ShapeDtypeStruct((n_rows, d_rnn), bf16 if reverse else f32),
        scratch_shapes=scratch, compiler_params=_params("arbitrary"),
        name="lru_scan_rev" if reverse else "lru_scan_fwd",
    )(*args)


def _lru_out_kernel(*refs, batch):
    z_ref = refs[0]
    x_refs = refs[1:1 + batch]
    w_ref, g1_ref, o_ref, y_s = refs[1 + batch:]
    tt = x_refs[0].shape[0]
    y = jnp.dot(z_ref[...], w_ref[...], preferred_element_type=f32)
    n_planes = y.shape[1] // LANES
    for s in range(n_planes):
        y_s[s] = y[:, s * LANES:(s + 1) * LANES]
    for b in range(batch):
        y_b = jnp.concatenate([y_s[s, pl.ds(b, tt, stride=batch), :] for s in range(n_planes)], axis=-1)
        o_ref[b] = x_refs[b][...] + g1_ref[b] * y_b


def _lru_out_call(z, tokens, w_out_bf, mod, batch, seq, tt):
    d = tokens.shape[1]
    d_rnn = z.shape[1]
    n_t = seq // tt
    x_specs = [pl.BlockSpec((tt, d), lambda i, b=b: (b * n_t + i, 0)) for b in range(batch)]
    return pl.pallas_call(
        functools.partial(_lru_out_kernel, batch=batch), grid=(n_t,),
        in_specs=[pl.BlockSpec((tt * batch, d_rnn), lambda i: (i, 0))] + x_specs
                 + [pl.BlockSpec((d_rnn, d), lambda i: (0, 0)), pl.BlockSpec((batch, 1, d), lambda i: (0, 0, 2))],
        out_specs=pl.BlockSpec((batch, tt, d), lambda i: (0, i, 0)),
        out_shape=jax.ShapeDtypeStruct((batch, seq, d), f32),
        scratch_shapes=[pltpu.VMEM((d // LANES, tt * batch, LANES), f32)],
        compiler_params=_params("arbitrary"), name="lru_out",
    )(z, *([tokens] * batch), w_out_bf, mod)


def kernel(x, c, ctx, c_ctx, ada_w, ada_b, norm_mix, norm_ffn, pool_w, pool_scale, lru_w_in, lru_conv_w, lru_conv_b, lru_w_r, lru_b_r, lru_w_i, lru_b_i, lru_lam, lru_w_out, router_w, router_b, exp_w_gu, exp_b_gu, exp_w_down, exp_b_down, final_norm):
    batch, seq, d = x.shape
    ctx_len = ctx.shape[1]
    assert d == SUBLANES * LANES and seq % GRID_W == 0
    n_lat, n_ctx = batch * seq, batch * ctx_len
    ctx_row = batch

    mod_rows = -(-(batch + 1) // SUBLANES) * SUBLANES
    cc = jnp.zeros((mod_rows, d), f32).at[:batch].set(c).at[batch].set(c_ctx)
    mod_all = _ada_call(cc, ada_w, ada_b)
    mods = [mod_all[l].reshape(mod_rows, 1, 6 * d) for l in range(ada_w.shape[0])]

    lat = x.reshape(n_lat, d)
    cx = ctx.reshape(n_ctx, d)
    tile = _pick_tile(ROUTE_TILE, seq, n_ctx)
    gain = lambda g: g.reshape(1, d)

    lat = _pool_lat_call(lat, _rms_call(lat, tile), gain(norm_mix[0]), pool_w[0], gain(pool_scale[0]), mods[0],
                         batch, seq)
    cx = _pool_ctx_call(cx, gain(norm_mix[0]), pool_w[0], gain(pool_scale[0]), mods[0], ctx_row, batch, ctx_len)
    tokens = _moe_layer(0, lat, cx, gain(norm_ffn[0]), mods[0], seq, ctx_row, router_w, router_b,
                        exp_w_gu, exp_b_gu, exp_w_down, exp_b_down, gain(final_norm), False)

    assert batch == SUBLANES
    tt = _pick_tile(LRU_TILE, seq, ctx_len)
    chunk = _pick_tile(SCAN_CHUNK, seq, ctx_len)
    gg, u = _lru_in_call(tokens, gain(norm_mix[1]), mods[1], ctx_row, lru_w_in[0].astype(bf16), batch, seq,
                         ctx_len, tt)
    scan = functools.partial(_lru_scan_call, conv_w=lru_conv_w[0], conv_b=lru_conv_b[0], batch=batch, seq=seq,
                             ctx_len=ctx_len, chunk=chunk)
    hf = scan(u, None, None, w_r=lru_w_r[0, 0], b_r=lru_b_r[0, 0], w_i=lru_w_i[0, 0], b_i=lru_b_i[0, 0],
              lam=lru_lam[0, 0], reverse=False)
    z = scan(u, hf, gg, w_r=lru_w_r[0, 1], b_r=lru_b_r[0, 1], w_i=lru_w_i[0, 1], b_i=lru_b_i[0, 1],
             lam=lru_lam[0, 1], reverse=True)
    lat = _lru_out_call(z, tokens, lru_w_out[0].astype(bf16), mods[1], batch, seq, tt).reshape(n_lat, d)
    out = _moe_layer(1, lat, None, gain(norm_ffn[1]), mods[1], seq, ctx_row, router_w, router_b,
                     exp_w_gu, exp_b_gu, exp_w_down, exp_b_down, gain(final_norm), True)
    return out.reshape(batch, seq, d)
```
